```python
import functools
import jax, jax.numpy as jnp
from jax import lax
import numpy as np

D_MODEL = 4096
BATCH = 2
SEQ = 4096
DEPTH = 1
DEC_BATCH = 32
DEC_SEQ = 1
PAST_LEN = 8192
PAGE_SIZE = 128

H_M = 8
DK_M = D_MODEL // 16
DV_M = D_MODEL // 8
MLSTM_CHUNK = 128
WINDOWS = (128, 512, 2048)
DILATIONS = (1, 4, 16)
N_GROUPS = 3
H_A = 8
DH_A = 128
Q_BLOCK = 128
ALIBI_MAX_EXP = 8.0
D_FF = ((8 * D_MODEL // 3 + 255) // 256) * 256
EPS = 1e-6
NEG_INF = -1e30

QK_M_COLS = H_M * DK_M
V_M_COLS = H_M * DV_M
A_COLS = N_GROUPS * H_A * DH_A
IN_SIZES = (QK_M_COLS, QK_M_COLS, V_M_COLS, V_M_COLS, H_M, H_M, A_COLS, A_COLS, A_COLS, 2 * D_MODEL)
SPLIT_POINTS = tuple(sum(IN_SIZES[:i + 1]) for i in range(len(IN_SIZES) - 1))
N_IN = sum(IN_SIZES)

kernel_name = 'mlstm_dilated_swa_gated_macaron_step'


def rmsnorm(x, g):
    xf = x.astype(jnp.float32)
    y = xf * lax.rsqrt(jnp.mean(xf * xf, axis=-1, keepdims=True) + EPS) * g.astype(jnp.float32)
    return y.astype(x.dtype)


def swiglu(x, w_in, w_out):
    gate, up = jnp.split(x @ w_in, 2, axis=-1)
    return (jax.nn.silu(gate) * up) @ w_out


def alibi_slopes():
    n = N_GROUPS * H_A
    e = jnp.arange(1, n + 1, dtype=jnp.float32) * (-ALIBI_MAX_EXP / n)
    return jnp.exp2(e).reshape(N_GROUPS, H_A)


def mixer_inputs(u, w_in, b_gate, b_igate, b_fgate):
    b, t, _ = u.shape
    f32 = jnp.float32
    q_m, k_m, v_m, o_m, ig, fg, q_a, k_a, v_a, gt = jnp.split(u @ w_in, SPLIT_POINTS, axis=-1)
    q_m = q_m.reshape(b, t, H_M, DK_M).astype(f32)
    k_m = k_m.reshape(b, t, H_M, DK_M).astype(f32) * (DK_M ** -0.5)
    v_m = v_m.reshape(b, t, H_M, DV_M).astype(f32)
    ig = (ig + b_igate).astype(f32)
    lf = jax.nn.log_sigmoid((fg + b_fgate).astype(f32))
    shape_a = (b, t, N_GROUPS, H_A, DH_A)
    gates = jax.nn.sigmoid(gt + b_gate).reshape(b, t, 2, D_MODEL)
    return (q_m, k_m, v_m, o_m, ig, lf,
            q_a.reshape(shape_a), k_a.reshape(shape_a), v_a.reshape(shape_a), gates)


def mlstm_chunk(state, inp):
    c, n, m = state
    q, k, v, ig, lf = inp
    l = q.shape[2]
    b = jnp.cumsum(lf, axis=-1)
    causal = jnp.tril(jnp.ones((l, l), dtype=bool))
    d = jnp.where(causal, b[..., :, None] - b[..., None, :] + ig[..., None, :], -jnp.inf)
    m_inter = b + m[..., None]
    m_t = jnp.maximum(m_inter, jnp.max(d, axis=-1))
    w_inter = jnp.exp(m_inter - m_t)
    a = jnp.einsum('bhtk,bhsk->bhts', q, k) * jnp.exp(d - m_t[..., None])
    num = jnp.einsum('bhts,bhsv->bhtv', a, v) + w_inter[..., None] * jnp.einsum('bhtk,bhkv->bhtv', q, c)
    den = jnp.sum(a, axis=-1) + w_inter * jnp.einsum('bhtk,bhk->bht', q, n)
    h = num / jnp.maximum(jnp.abs(den), jnp.exp(-m_t))[..., None]
    m_new = m_t[..., -1]
    w_end = jnp.exp(b[..., -1:] - b + ig - m_new[..., None])
    decay = jnp.exp(b[..., -1] + m - m_new)
    c_new = decay[..., None, None] * c + jnp.einsum('bhsk,bhsv->bhkv', k * w_end[..., None], v)
    n_new = decay[..., None] * n + jnp.einsum('bhsk,bhs->bhk', k, w_end)
    return (c_new, n_new, m_new), h


def mlstm_prompt(q, k, v, ig, lf):
    bsz, t = q.shape[:2]
    nc = t // MLSTM_CHUNK

    def chunks(a):
        a = a.reshape((bsz, nc, MLSTM_CHUNK) + a.shape[2:])
        return jnp.moveaxis(a, (1, 3), (0, 2))

    f32 = jnp.float32
    init = (jnp.zeros((bsz, H_M, DK_M, DV_M), f32), jnp.zeros((bsz, H_M, DK_M), f32),
            jnp.zeros((bsz, H_M), f32))
    state, h = lax.scan(mlstm_chunk, init, tuple(chunks(a) for a in (q, k, v, ig, lf)))
    h = jnp.moveaxis(h, (0, 2), (1, 3))
    return state, h.reshape(bsz, t, V_M_COLS)


def mlstm_sample(q, k, v, ig, lf, c0, n0, m0):
    bsz, t = q.shape[:2]
    f32 = jnp.float32
    state0 = (c0.astype(f32), n0.astype(f32), m0.astype(f32))
    state, h = mlstm_chunk(state0, tuple(jnp.swapaxes(a, 1, 2) for a in (q, k, v, ig, lf)))
    return state, jnp.swapaxes(h, 1, 2).reshape(bsz, t, V_M_COLS)


def window_attn_group(q, k_ext, v_ext, q_idx, dil, n_keys, slopes):
    dist = jnp.arange(n_keys) * dil
    kidx = q_idx[:, None] - dist[None, :]
    valid = kidx >= 0
    kidx = jnp.maximum(kidx, 0)
    kg = jnp.take(k_ext, kidx, axis=1)
    vg = jnp.take(v_ext, kidx, axis=1)
    s = jnp.einsum('bqhd,bqjhd->bhqj', q, kg).astype(jnp.float32) * (DH_A ** -0.5)
    s = s - slopes[None, :, None, None] * dist.astype(jnp.float32)
    s = jnp.where(valid[None, None], s, NEG_INF)
    lse = jax.nn.logsumexp(s, axis=-1)
    p = jnp.exp(s - lse[..., None])
    o = jnp.einsum('bhqj,bqjhd->bqhd', p, vg.astype(jnp.float32))
    return o, jnp.swapaxes(lse, 1, 2)


def dilated_mixture(q_a, ks, vs, q_idxs):
    slopes = alibi_slopes()
    outs, lses = [], []
    for g in range(N_GROUPS):
        o, lse = window_attn_group(q_a[:, :, g], ks[g], vs[g], q_idxs[g], DILATIONS[g],
                                   WINDOWS[g] // DILATIONS[g] + 1, slopes[g])
        outs.append(o)
        lses.append(lse)
    wts = jax.nn.softmax(jnp.stack(lses), axis=0)
    return jnp.sum(wts[..., None] * jnp.stack(outs), axis=0)


def dilated_attn_prompt(q_a, k_a, v_a):
    bsz, t = q_a.shape[:2]
    nb = t // Q_BLOCK
    ks = [k_a[:, :, g] for g in range(N_GROUPS)]
    vs = [v_a[:, :, g] for g in range(N_GROUPS)]
    q_blocks = jnp.swapaxes(q_a.reshape(bsz, nb, Q_BLOCK, N_GROUPS, H_A, DH_A), 0, 1)

    def one_block(args):
        qb, start = args
        q_idx = start + jnp.arange(Q_BLOCK)
        return dilated_mixture(qb, ks, vs, [q_idx] * N_GROUPS)

    o = lax.map(one_block, (q_blocks, jnp.arange(nb) * Q_BLOCK))
    return jnp.swapaxes(o, 0, 1).reshape(bsz, t, H_A * DH_A)


def dilated_attn_sample(q_a, k_a, v_a, caches):
    bsz, t = q_a.shape[:2]
    ks = [jnp.concatenate([cch[:, :, 0], k_a[:, :, g]], axis=1) for g, cch in enumerate(caches)]
    vs = [jnp.concatenate([cch[:, :, 1], v_a[:, :, g]], axis=1) for g, cch in enumerate(caches)]
    q_idxs = [cch.shape[1] + jnp.arange(t) for cch in caches]
    return dilated_mixture(q_a, ks, vs, q_idxs).reshape(bsz, t, H_A * DH_A)


def mix_prompt(q_m, k_m, v_m, ig, lf, q_a, k_a, v_a):
    (c, n, m), h_m = mlstm_prompt(q_m, k_m, v_m, ig, lf)
    h_a = dilated_attn_prompt(q_a, k_a, v_a)
    t = q_a.shape[1]
    kv = [jnp.stack([k_a[:, t - min(w, t):, g], v_a[:, t - min(w, t):, g]], axis=2)
          for g, w in enumerate(WINDOWS)]
    return h_m, h_a, (c, n, m, kv[0], kv[1], kv[2])


def mix_sample(q_m, k_m, v_m, ig, lf, q_a, k_a, v_a, c0, n0, m0, kv1, kv2, kv3):
    (c, n, m), h_m = mlstm_sample(q_m, k_m, v_m, ig, lf, c0, n0, m0)
    h_a = dilated_attn_sample(q_a, k_a, v_a, (kv1, kv2, kv3))
    kv = [jnp.stack([k_a[:, :, g], v_a[:, :, g]], axis=2) for g in range(N_GROUPS)]
    return h_m, h_a, (c, n, m, kv[0], kv[1], kv[2])


def decoder_layer(x, mix_fn, g_ffn1, w_ffn1_in, w_ffn1_out, g_mix, w_in, b_gate, b_igate, b_fgate,
                  w_mlstm_out, w_attn_out, w_out, g_ffn2, w_ffn2_in, w_ffn2_out):
    x = x + 0.5 * swiglu(rmsnorm(x, g_ffn1), w_ffn1_in, w_ffn1_out)
    q_m, k_m, v_m, o_m, ig, lf, q_a, k_a, v_a, gates = mixer_inputs(
        rmsnorm(x, g_mix), w_in, b_gate, b_igate, b_fgate)
    h_m, h_a, new_state = mix_fn(q_m, k_m, v_m, ig, lf, q_a, k_a, v_a)
    dt = o_m.dtype
    br_m = (jax.nn.sigmoid(o_m) * h_m.astype(dt)) @ w_mlstm_out
    br_a = h_a.astype(dt) @ w_attn_out
    x = x + (gates[:, :, 0] * br_m + gates[:, :, 1] * br_a) @ w_out
    x = x + 0.5 * swiglu(rmsnorm(x, g_ffn2), w_ffn2_in, w_ffn2_out)
    return x, new_state


def setup_inputs(seed: int = 0) -> dict:
    key = jax.random.key(seed)
    ks = jax.random.split(key, 24)
    f32 = jnp.float32

    def nrm(k, shape, scale):
        return jax.random.normal(k, shape, f32) * scale

    win = [min(w, PAST_LEN) for w in WINDOWS]
    d = D_MODEL
    return {
        'x_prompt': nrm(ks[0], (BATCH, SEQ, d), 1.0),
        'x_sample': nrm(ks[1], (DEC_BATCH, DEC_SEQ, d), 1.0),
        'state_mlstm_C': nrm(ks[2], (DEPTH, DEC_BATCH, H_M, DK_M, DV_M), 0.1),
        'state_mlstm_n': nrm(ks[3], (DEPTH, DEC_BATCH, H_M, DK_M), 0.1),
        'state_mlstm_m': nrm(ks[4], (DEPTH, DEC_BATCH, H_M), 0.5),
        'cache_win1_kv': nrm(ks[5], (DEPTH, DEC_BATCH, win[0], 2, H_A, DH_A), 1.0),
        'cache_win2_kv': nrm(ks[6], (DEPTH, DEC_BATCH, win[1], 2, H_A, DH_A), 1.0),
        'cache_win3_kv': nrm(ks[7], (DEPTH, DEC_BATCH, win[2], 2, H_A, DH_A), 1.0),
        'g_ffn1': 1.0 + nrm(ks[8], (DEPTH, d), 0.02),
        'w_ffn1_in': nrm(ks[9], (DEPTH, d, 2 * D_FF), d ** -0.5),
        'w_ffn1_out': nrm(ks[10], (DEPTH, D_FF, d), D_FF ** -0.5),
        'g_mix': 1.0 + nrm(ks[11], (DEPTH, d), 0.02),
        'w_in': nrm(ks[12], (DEPTH, d, N_IN), d ** -0.5),
        'b_gate': nrm(ks[13], (DEPTH, 2 * d), 0.02),
        'b_igate': nrm(ks[14], (DEPTH, H_M), 0.1),
        'b_fgate': jnp.linspace(3.0, 6.0, H_M, dtype=f32) + nrm(ks[15], (DEPTH, H_M), 0.1),
        'w_mlstm_out': nrm(ks[16], (DEPTH, V_M_COLS, d), V_M_COLS ** -0.5),
        'w_attn_out': nrm(ks[17], (DEPTH, H_A * DH_A, d), (H_A * DH_A) ** -0.5),
        'w_out': nrm(ks[18], (DEPTH, d, d), d ** -0.5),
        'g_ffn2': 1.0 + nrm(ks[19], (DEPTH, d), 0.02),
        'w_ffn2_in': nrm(ks[20], (DEPTH, d, 2 * D_FF), d ** -0.5),
        'w_ffn2_out': nrm(ks[21], (DEPTH, D_FF, d), D_FF ** -0.5),
        'g_final': 1.0 + nrm(ks[22], (d,), 0.02),
    }


def reference(x_prompt, x_sample, state_mlstm_C, state_mlstm_n, state_mlstm_m,
              cache_win1_kv, cache_win2_kv, cache_win3_kv,
              g_ffn1, w_ffn1_in, w_ffn1_out, g_mix, w_in, b_gate, b_igate, b_fgate,
              w_mlstm_out, w_attn_out, w_out, g_ffn2, w_ffn2_in, w_ffn2_out, g_final):
    yp, ys = x_prompt, x_sample
    new_p = [[] for _ in range(6)]
    new_s = [[] for _ in range(6)]
    for l in range(DEPTH):
        lw = (g_ffn1[l], w_ffn1_in[l], w_ffn1_out[l], g_mix[l], w_in[l], b_gate[l], b_igate[l],
              b_fgate[l], w_mlstm_out[l], w_attn_out[l], w_out[l], g_ffn2[l], w_ffn2_in[l],
              w_ffn2_out[l])
        yp, st_p = decoder_layer(yp, mix_prompt, *lw)
        mix_s = functools.partial(mix_sample, c0=state_mlstm_C[l], n0=state_mlstm_n[l],
                                  m0=state_mlstm_m[l], kv1=cache_win1_kv[l],
                                  kv2=cache_win2_kv[l], kv3=cache_win3_kv[l])
        ys, st_s = decoder_layer(ys, mix_s, *lw)
        for i in range(6):
            new_p[i].append(st_p[i])
            new_s[i].append(st_s[i])
    return (rmsnorm(yp, g_final), rmsnorm(ys, g_final),
            jnp.stack(new_p[0]), jnp.stack(new_p[1]), jnp.stack(new_p[2]),
            jnp.stack(new_s[0]), jnp.stack(new_s[1]), jnp.stack(new_s[2]),
            jnp.stack(new_p[3]), jnp.stack(new_p[4]), jnp.stack(new_p[5]),
            jnp.stack(new_s[3]), jnp.stack(new_s[4]), jnp.stack(new_s[5]))
```

```python
import functools

import numpy as np
import jax
import jax.numpy as jnp
from jax import lax
from jax.experimental import pallas as pl
from jax.experimental.pallas import tpu as pltpu

F32 = jnp.float32
BF16 = jnp.bfloat16

D_MODEL = 4096
BATCH = 2
SEQ = 4096
DEC_BATCH = 32
H_M = 8
DK_M = 256
DV_M = 512
MLSTM_CHUNK = 128
WINDOWS = (128, 512, 2048)
DILATIONS = (1, 4, 16)
N_GROUPS = 3
H_A = 8
DH_A = 128
Q_BLOCK = 128
ALIBI_MAX_EXP = 8.0
D_FF = 11008
EPS = 1e-6
NEG_INF = -1e30

QK_M_COLS = H_M * DK_M
V_M_COLS = H_M * DV_M
A_GROUP_COLS = H_A * DH_A
A_COLS = N_GROUPS * A_GROUP_COLS
COL_O_M = 2 * QK_M_COLS + V_M_COLS
COL_IF = COL_O_M + V_M_COLS
COL_Q_A = COL_IF + 2 * H_M
COL_GATE = COL_Q_A + 3 * A_COLS

N_PROMPT = BATCH * SEQ
SAMPLE_PAD = 64
M_ROWS = N_PROMPT + SAMPLE_PAD

V7X_VMEM_BYTES = 64 * 1024 * 1024
VMEM_LIMIT_BYTES = V7X_VMEM_BYTES - 8 * 1024 * 1024

ROW_TILE = 192
MM_TM = 1376
MM_TN = 512
FFN_TM = 688
FFN_TF = 256
FFN_TN = 512


def _params(*semantics):
    return pltpu.CompilerParams(dimension_semantics=semantics, vmem_limit_bytes=VMEM_LIMIT_BYTES)


def _rms(x, g):
    ms = jnp.mean(x * x, axis=-1, keepdims=True)
    return x * lax.rsqrt(ms + EPS) * g


def _rmsnorm_body(x_ref, g_ref, o_ref):
    o_ref[...] = _rms(x_ref[...], g_ref[...]).astype(o_ref.dtype)


def _rmsnorm(x, g, out_dtype):
    m, d = x.shape
    return pl.pallas_call(
        _rmsnorm_body,
        grid=(m // ROW_TILE,),
        in_specs=[pl.BlockSpec((ROW_TILE, d), lambda i: (i, 0)),
                  pl.BlockSpec((1, d), lambda i: (0, 0))],
        out_specs=pl.BlockSpec((ROW_TILE, d), lambda i: (i, 0)),
        out_shape=jax.ShapeDtypeStruct((m, d), out_dtype),
        compiler_params=_params("parallel"),
        name="rmsnorm",
    )(x, g.reshape(1, d))


def _add_rmsnorm_body(x_ref, y_ref, g_ref, xo_ref, no_ref, *, scale):
    x = x_ref[...] + scale * y_ref[...]
    xo_ref[...] = x
    no_ref[...] = _rms(x, g_ref[...]).astype(no_ref.dtype)


def _add_rmsnorm(x, y, g, scale):
    m, d = x.shape
    row = pl.BlockSpec((ROW_TILE, d), lambda i: (i, 0))
    return pl.pallas_call(
        functools.partial(_add_rmsnorm_body, scale=scale),
        grid=(m // ROW_TILE,),
        in_specs=[row, row, pl.BlockSpec((1, d), lambda i: (0, 0))],
        out_specs=[row, row],
        out_shape=[jax.ShapeDtypeStruct((m, d), F32), jax.ShapeDtypeStruct((m, d), BF16)],
        compiler_params=_params("parallel"),
        name="add_rmsnorm",
    )(x, y, g.reshape(1, d))


def _add_final_norm_body(x_ref, y_ref, g_ref, o_ref, *, scale):
    o_ref[...] = _rms(x_ref[...] + scale * y_ref[...], g_ref[...])


def _add_final_norm(x, y, g, scale):
    m, d = x.shape
    row = pl.BlockSpec((ROW_TILE, d), lambda i: (i, 0))
    return pl.pallas_call(
        functools.partial(_add_final_norm_body, scale=scale),
        grid=(m // ROW_TILE,),
        in_specs=[row, row, pl.BlockSpec((1, d), lambda i: (0, 0))],
        out_specs=row,
        out_shape=jax.ShapeDtypeStruct((m, d), F32),
        compiler_params=_params("parallel"),
        name="add_final_norm",
    )(x, y, g.reshape(1, d))


def _mm_body(*refs, a_of_dot, n_a, n_extra, epilogue):
    n_dot = len(a_of_dot)
    a_refs = refs[:n_a]
    w_refs = refs[n_a:n_a + n_dot]
    e_refs = refs[n_a + n_dot:n_a + n_dot + n_extra]
    o_ref = refs[n_a + n_dot + n_extra]
    prods = [jnp.dot(a_refs[ai][...], w_refs[d][...], preferred_element_type=F32)
             for d, ai in enumerate(a_of_dot)]
    o_ref[...] = epilogue(prods, [e[...] for e in e_refs]).astype(o_ref.dtype)


def _mm(name, a_list, w_list, extras, n, out_dtype, epilogue, tm=MM_TM, tn=MM_TN):
    m = a_list[0].shape[0]
    in_specs = [pl.BlockSpec((tm, a.shape[1]), lambda i, j: (i, 0)) for a in a_list]
    operands = list(a_list)
    for _, w, off in w_list:
        in_specs.append(pl.BlockSpec((w.shape[0], tn), lambda i, j, o=off // tn: (0, j + o)))
        operands.append(w)
    for arr, kind, off in extras:
        if kind == "tile":
            in_specs.append(pl.BlockSpec((tm, tn), lambda i, j, o=off // tn: (i, j + o)))
        else:
            in_specs.append(pl.BlockSpec((1, tn), lambda i, j, o=off // tn: (0, j + o)))
        operands.append(arr)
    body = functools.partial(_mm_body, a_of_dot=tuple(ai for ai, _, _ in w_list),
                             n_a=len(a_list), n_extra=len(extras), epilogue=epilogue)
    return pl.pallas_call(
        body,
        grid=(m // tm, n // tn),
        in_specs=in_specs,
        out_specs=pl.BlockSpec((tm, tn), lambda i, j: (i, j)),
        out_shape=jax.ShapeDtypeStruct((m, n), out_dtype),
        compiler_params=_params("parallel", "arbitrary"),
        name=name,
    )(*operands)


def _ffn_body(x_ref, wg_ref, wu_ref, wo_ref, o_ref):
    x = x_ref[...]
    g = jnp.dot(x, wg_ref[...], preferred_element_type=F32)
    u = jnp.dot(x, wu_ref[...], preferred_element_type=F32)
    h = (g * jax.nn.sigmoid(g) * u).astype(BF16)

    @pl.when(pl.program_id(1) == 0)
    def _():
        o_ref[...] = jnp.zeros_like(o_ref)

    for c in range(0, o_ref.shape[1], FFN_TN):
        o_ref[:, c:c + FFN_TN] += jnp.dot(h, wo_ref[:, c:c + FFN_TN], preferred_element_type=F32)


def _swiglu(xn, w_in, w_out):
    m, d = xn.shape
    nf = D_FF // FFN_TF
    return pl.pallas_call(
        _ffn_body,
        grid=(m // FFN_TM, nf),
        in_specs=[pl.BlockSpec((FFN_TM, d), lambda i, f: (i, 0)),
                  pl.BlockSpec((d, FFN_TF), lambda i, f: (0, f)),
                  pl.BlockSpec((d, FFN_TF), lambda i, f: (0, f + nf)),
                  pl.BlockSpec((FFN_TF, d), lambda i, f: (f, 0))],
        out_specs=pl.BlockSpec((FFN_TM, d), lambda i, f: (i, 0)),
        out_shape=jax.ShapeDtypeStruct((m, d), F32),
        compiler_params=_params("parallel", "arbitrary"),
        name="swiglu",
    )(xn, w_in, w_in, w_out)


def _log_sigmoid(x):
    return jnp.minimum(x, 0.0) - jnp.log1p(jnp.exp(-jnp.abs(x)))


def _mlstm_prompt_body(q_ref, k_ref, v_ref, so_ref, if_ref, bias_ref,
                       h_ref, c_ref, n_ref, m_ref):
    L = MLSTM_CHUNK

    @pl.when(pl.program_id(1) == 0)
    def _():
        c_ref[...] = jnp.zeros_like(c_ref)
        n_ref[...] = jnp.zeros_like(n_ref)
        m_ref[...] = jnp.zeros_like(m_ref)

    x = if_ref[...] + bias_ref[...]
    xt = x.T
    t_idx = lax.broadcasted_iota(jnp.int32, (L, L), 0)
    s_idx = lax.broadcasted_iota(jnp.int32, (L, L), 1)
    causal = s_idx <= t_idx
    for h in range(H_M):
        ig_col = x[:, h:h + 1]
        ig_row = xt[h:h + 1, :]
        lf_col = _log_sigmoid(x[:, H_M + h:H_M + h + 1])
        lf_row = _log_sigmoid(xt[H_M + h:H_M + h + 1, :])
        b_col = jnp.sum(jnp.where(causal, lf_row, 0.0), axis=1, keepdims=True)
        b_row = jnp.sum(jnp.where(t_idx <= s_idx, lf_col, 0.0), axis=0, keepdims=True)
        m_prev = m_ref[:, h:h + 1]
        d = jnp.where(causal, b_col - b_row + ig_row, -jnp.inf)
        m_inter = b_col + m_prev
        m_t = jnp.maximum(m_inter, jnp.max(d, axis=1, keepdims=True))
        w_inter = jnp.exp(m_inter - m_t)

        qh = q_ref[:, h * DK_M:(h + 1) * DK_M]
        kf = k_ref[:, h * DK_M:(h + 1) * DK_M].astype(F32) * (DK_M ** -0.5)
        vh = v_ref[:, h * DV_M:(h + 1) * DV_M]
        s = lax.dot_general(qh, kf.astype(BF16), (((1,), (1,)), ((), ())),
                            preferred_element_type=F32)
        a = s * jnp.exp(d - m_t)
        c_old = c_ref[h]
        n_old = n_ref[h:h + 1, :]
        num = (jnp.dot(a.astype(BF16), vh, preferred_element_type=F32)
               + w_inter * jnp.dot(qh, c_old.astype(BF16), preferred_element_type=F32))
        den = (jnp.sum(a, axis=1, keepdims=True)
               + w_inter * jnp.sum(qh.astype(F32) * n_old, axis=1, keepdims=True))
        hh = num / jnp.maximum(jnp.abs(den), jnp.exp(-m_t))
        gate = so_ref[:, h * DV_M:(h + 1) * DV_M].astype(F32)
        h_ref[:, h * DV_M:(h + 1) * DV_M] = (gate * hh).astype(h_ref.dtype)

        m_new = m_t[L - 1:L, :]
        b_last = b_col[L - 1:L, :]
        w_end = jnp.exp(b_last - b_col + ig_col - m_new)
        decay = jnp.exp(b_last + m_prev - m_new)
        kw = kf * w_end
        dc = jnp.dot(kw.T.astype(BF16), vh, preferred_element_type=F32)
        c_ref[h] = decay * c_old + dc
        n_ref[h:h + 1, :] = decay * n_old + jnp.sum(kw, axis=0, keepdims=True)
        m_ref[:, h:h + 1] = m_new


def _mlstm_prompt(qkv_m, sig_o, igfg, gate_bias):
    nc = SEQ // MLSTM_CHUNK
    L = MLSTM_CHUNK

    def rows(width, col):
        return pl.BlockSpec((L, width), lambda b, c: (b * nc + c, col))

    return pl.pallas_call(
        _mlstm_prompt_body,
        grid=(BATCH, nc),
        in_specs=[rows(QK_M_COLS, 0), rows(QK_M_COLS, 1), rows(V_M_COLS, 1), rows(V_M_COLS, 0),
                  rows(128, 0), pl.BlockSpec((1, 128), lambda b, c: (0, 0))],
        out_specs=[rows(V_M_COLS, 0),
                   pl.BlockSpec((None, H_M, DK_M, DV_M), lambda b, c: (b, 0, 0, 0)),
                   pl.BlockSpec((None, H_M, DK_M), lambda b, c: (b, 0, 0)),
                   pl.BlockSpec((None, 1, H_M), lambda b, c: (b, 0, 0))],
        out_shape=[jax.ShapeDtypeStruct((N_PROMPT, V_M_COLS), BF16),
                   jax.ShapeDtypeStruct((BATCH, H_M, DK_M, DV_M), F32),
                   jax.ShapeDtypeStruct((BATCH, H_M, DK_M), F32),
                   jax.ShapeDtypeStruct((BATCH, 1, H_M), F32)],
        compiler_params=_params("parallel", "arbitrary"),
        name="mlstm_prompt",
    )(qkv_m, qkv_m, qkv_m, sig_o, igfg, gate_bias)


def _mlstm_sample_body(q_ref, qt_ref, k_ref, kt_ref, v_ref, so_ref, ig_ref, fg_ref, m0_ref,
                       c0_ref, n0_ref, h_ref, c_ref, n_ref, m_ref):
    for h in range(H_M):
        ig = ig_ref[h:h + 1, 0:1]
        lf = _log_sigmoid(fg_ref[h:h + 1, 0:1])
        m_prev = m0_ref[h:h + 1, 0:1]
        m_inter = lf + m_prev
        m_t = jnp.maximum(m_inter, ig)
        w_inter = jnp.exp(m_inter - m_t)
        q_row = q_ref[h:h + 1, :]
        k_row = k_ref[h:h + 1, :] * (DK_M ** -0.5)
        q_col = qt_ref[:, h:h + 1]
        k_col = kt_ref[:, h:h + 1] * (DK_M ** -0.5)
        v_row = v_ref[h:h + 1, :]
        c_old = c0_ref[h]
        n_old = n0_ref[h:h + 1, :]
        a = jnp.sum(q_row * k_row, axis=1, keepdims=True) * jnp.exp(ig - m_t)
        num = a * v_row + w_inter * jnp.sum(q_col * c_old, axis=0, keepdims=True)
        den = a + w_inter * jnp.sum(q_row * n_old, axis=1, keepdims=True)
        hh = num / jnp.maximum(jnp.abs(den), jnp.exp(-m_t))
        h_ref[h:h + 1, :] = so_ref[h:h + 1, :] * hh
        w_end = jnp.exp(ig - m_t)
        decay = jnp.exp(m_inter - m_t)
        c_ref[h] = decay * c_old + (k_col * w_end) * v_row
        n_ref[h:h + 1, :] = decay * n_old + k_row * w_end
        m_ref[h:h + 1, :] = jnp.broadcast_to(m_t, (1, 128))


def _mlstm_sample(q, k, v, so, ig, fg, c0, n0, m0):
    nb = q.shape[0]

    def lanes(a):
        return jnp.broadcast_to(a[:, :, None], (nb, H_M, 128))

    def per_b(*shape):
        return pl.BlockSpec((None,) + shape, lambda b: (b,) + (0,) * len(shape))

    return pl.pallas_call(
        _mlstm_sample_body,
        grid=(nb,),
        in_specs=[per_b(H_M, DK_M), per_b(DK_M, H_M), per_b(H_M, DK_M), per_b(DK_M, H_M),
                  per_b(H_M, DV_M), per_b(H_M, DV_M), per_b(H_M, 128), per_b(H_M, 128),
                  per_b(H_M, 128), per_b(H_M, DK_M, DV_M), per_b(H_M, DK_M)],
        out_specs=[per_b(H_M, DV_M), per_b(H_M, DK_M, DV_M), per_b(H_M, DK_M), per_b(H_M, 128)],
        out_shape=[jax.ShapeDtypeStruct((nb, H_M, DV_M), F32),
                   jax.ShapeDtypeStruct((nb, H_M, DK_M, DV_M), F32),
                   jax.ShapeDtypeStruct((nb, H_M, DK_M), F32),
                   jax.ShapeDtypeStruct((nb, H_M, 128), F32)],
        compiler_params=_params("parallel"),
        name="mlstm_sample",
    )(q, jnp.swapaxes(q, 1, 2), k, jnp.swapaxes(k, 1, 2), v, so, lanes(ig), lanes(fg), lanes(m0),
      c0, n0)


def _alibi_slopes():
    n = N_GROUPS * H_A
    e = np.arange(1, n + 1, dtype=np.float64) * (-ALIBI_MAX_EXP / n)
    return np.exp2(e).reshape(N_GROUPS, H_A)


def _attn_prompt_body(q_ref, kc_ref, kp_ref, vc_ref, vp_ref, o_ref, lse_ref, *, dil, slopes):
    L = Q_BLOCK
    has_prev = pl.program_id(2) > 0
    t_idx = lax.broadcasted_iota(jnp.int32, (L, L), 0)
    s_idx = lax.broadcasted_iota(jnp.int32, (L, L), 1)
    jc = t_idx - s_idx
    jp = jc + L
    valid_c = jc >= 0
    valid_p = jnp.logical_and(jp <= L, has_prev)
    dist_c = (jc * dil).astype(F32)
    dist_p = (jp * dil).astype(F32)
    nt = (((1,), (1,)), ((), ()))
    lse_ref[...] = jnp.zeros_like(lse_ref)
    for h in range(H_A):
        sl = slice(h * DH_A, (h + 1) * DH_A)
        q = q_ref[:, sl]
        sc = lax.dot_general(q, kc_ref[:, sl].astype(BF16), nt, preferred_element_type=F32)
        sp = lax.dot_general(q, kp_ref[:, sl].astype(BF16), nt, preferred_element_type=F32)
        sc = jnp.where(valid_c, sc * (DH_A ** -0.5) - slopes[h] * dist_c, NEG_INF)
        sp = jnp.where(valid_p, sp * (DH_A ** -0.5) - slopes[h] * dist_p, NEG_INF)
        m = jnp.maximum(jnp.max(sc, axis=1, keepdims=True), jnp.max(sp, axis=1, keepdims=True))
        pc = jnp.exp(sc - m)
        pp = jnp.exp(sp - m)
        l = jnp.sum(pc, axis=1, keepdims=True) + jnp.sum(pp, axis=1, keepdims=True)
        o = (jnp.dot(pc.astype(BF16), vc_ref[:, sl].astype(BF16), preferred_element_type=F32)
             + jnp.dot(pp.astype(BF16), vp_ref[:, sl].astype(BF16), preferred_element_type=F32))
        o_ref[:, sl] = o / l
        lse_ref[:, h:h + 1] = m + jnp.log(l)


def _attn_prompt_group(q_a, kv_a, g):
    dil = DILATIONS[g]
    assert WINDOWS[g] == Q_BLOCK * dil
    nb = SEQ // dil // Q_BLOCK
    qv = q_a.reshape(M_ROWS // dil, dil * A_COLS)
    kvv = kv_a.reshape(M_ROWS // dil, dil * 2 * A_COLS)
    L, W = Q_BLOCK, A_GROUP_COLS

    def cur(col_of_r):
        return pl.BlockSpec((L, W), lambda b, r, i: (b * nb + i, col_of_r(r)))

    def prev(col_of_r):
        return pl.BlockSpec((L, W), lambda b, r, i: (b * nb + jnp.maximum(i - 1, 0), col_of_r(r)))

    q_col = lambda r: r * N_GROUPS + g
    k_col = lambda r: r * 2 * N_GROUPS + g
    v_col = lambda r: r * 2 * N_GROUPS + N_GROUPS + g
    body = functools.partial(_attn_prompt_body, dil=dil, slopes=tuple(float(s) for s in _alibi_slopes()[g]))
    o, lse = pl.pallas_call(
        body,
        grid=(BATCH, dil, nb),
        in_specs=[cur(q_col), cur(k_col), prev(k_col), cur(v_col), prev(v_col)],
        out_specs=[pl.BlockSpec((L, W), lambda b, r, i: (b * nb + i, r)),
                   pl.BlockSpec((L, 128), lambda b, r, i: (b * nb + i, r))],
        out_shape=[jax.ShapeDtypeStruct((N_PROMPT // dil, dil * W), F32),
                   jax.ShapeDtypeStruct((N_PROMPT // dil, dil * 128), F32)],
        compiler_params=_params("parallel", "parallel", "arbitrary"),
        name=f"attn_prompt_g{g}",
    )(qv, kvv, kvv, kvv, kvv)
    return o.reshape(N_PROMPT, W), lse.reshape(N_PROMPT, 128)


def _merge_groups_body(o0_ref, o1_ref, o2_ref, l0_ref, l1_ref, l2_ref, out_ref):
    o_refs = (o0_ref, o1_ref, o2_ref)
    l_refs = (l0_ref, l1_ref, l2_ref)
    for h in range(H_A):
        sl = slice(h * DH_A, (h + 1) * DH_A)
        lse = [l[:, h:h + 1] for l in l_refs]
        m = jnp.maximum(jnp.maximum(lse[0], lse[1]), lse[2])
        e = [jnp.exp(x - m) for x in lse]
        z = e[0] + e[1] + e[2]
        acc = (e[0] / z) * o_refs[0][:, sl]
        for g in (1, 2):
            acc = acc + (e[g] / z) * o_refs[g][:, sl]
        out_ref[:, sl] = acc.astype(out_ref.dtype)


def _merge_groups(outs, lses):
    tr = 256
    o_spec = pl.BlockSpec((tr, A_GROUP_COLS), lambda i: (i, 0))
    l_spec = pl.BlockSpec((tr, 128), lambda i: (i, 0))
    return pl.pallas_call(
        _merge_groups_body,
        grid=(N_PROMPT // tr,),
        in_specs=[o_spec] * 3 + [l_spec] * 3,
        out_specs=o_spec,
        out_shape=jax.ShapeDtypeStruct((N_PROMPT, A_GROUP_COLS), BF16),
        compiler_params=_params("parallel"),
        name="attn_merge",
    )(*outs, *lses)


def _attn_sample_body(q_ref, kn_ref, vn_ref, c0_ref, c1_ref, c2_ref, o_ref, *, slopes):
    L = Q_BLOCK
    c_refs = (c0_ref, c1_ref, c2_ref)
    steps = (L - lax.broadcasted_iota(jnp.int32, (L, 1), 0)).astype(F32)
    for h in range(H_A):
        sl = slice(h * DH_A, (h + 1) * DH_A)
        vsl = slice(A_GROUP_COLS + h * DH_A, A_GROUP_COLS + (h + 1) * DH_A)
        outs, lses = [], []
        for g in range(N_GROUPS):
            q = q_ref[g:g + 1, sl]
            s = (jnp.sum(c_refs[g][:, sl] * q, axis=1, keepdims=True) * (DH_A ** -0.5)
                 - slopes[g][h] * DILATIONS[g] * steps)
            s_new = jnp.sum(q * kn_ref[g:g + 1, sl], axis=1, keepdims=True) * (DH_A ** -0.5)
            m = jnp.maximum(jnp.max(s, axis=0, keepdims=True), s_new)
            p = jnp.exp(s - m)
            p_new = jnp.exp(s_new - m)
            l = jnp.sum(p, axis=0, keepdims=True) + p_new
            o = (jnp.sum(p * c_refs[g][:, vsl], axis=0, keepdims=True) + p_new * vn_ref[g:g + 1, sl]) / l
            outs.append(o)
            lses.append(m + jnp.log(l))
        m = jnp.maximum(jnp.maximum(lses[0], lses[1]), lses[2])
        e = [jnp.exp(x - m) for x in lses]
        z = e[0] + e[1] + e[2]
        acc = (e[0] / z) * outs[0]
        for g in (1, 2):
            acc = acc + (e[g] / z) * outs[g]
        o_ref[:, sl] = acc


def _attn_sample(q, kn, vn, caches):
    nb = q.shape[0]
    W = A_GROUP_COLS
    small = pl.BlockSpec((None, N_GROUPS, W), lambda b: (b, 0, 0))
    cache_specs, cache_views = [], []
    for g, c in enumerate(caches):
        dil = DILATIONS[g]
        assert c.shape[1] == Q_BLOCK * dil
        cache_views.append(c.reshape(nb, Q_BLOCK, dil * 2 * W))
        cache_specs.append(pl.BlockSpec((None, Q_BLOCK, 2 * W), lambda b: (b, 0, 0)))
    slopes = tuple(tuple(float(s) for s in row) for row in _alibi_slopes())
    return pl.pallas_call(
        functools.partial(_attn_sample_body, slopes=slopes),
        grid=(nb,),
        in_specs=[small, small, small] + cache_specs,
        out_specs=pl.BlockSpec((None, 1, W), lambda b: (b, 0, 0)),
        out_shape=jax.ShapeDtypeStruct((nb, 1, W), F32),
        compiler_params=_params("parallel"),
        name="attn_sample",
    )(q, kn, vn, *cache_views)


def _sigmoid_epilogue(prods, extras):
    return jax.nn.sigmoid(prods[0])


def _gate_epilogue(prods, extras):
    return jax.nn.sigmoid(prods[0] + extras[0])


def _plain_epilogue(prods, extras):
    return prods[0]


def _mix_epilogue(prods, extras):
    return extras[0].astype(F32) * prods[0] + extras[1].astype(F32) * prods[1]


def _residual_epilogue(prods, extras):
    return extras[0] + prods[0]


def kernel(x_prompt, x_sample, state_mlstm_C, state_mlstm_n, state_mlstm_m, cache_win1_kv, cache_win2_kv, cache_win3_kv, g_ffn1, w_ffn1_in, w_ffn1_out, g_mix, w_in, b_gate, b_igate, b_fgate, w_mlstm_out, w_attn_out, w_out, g_ffn2, w_ffn2_in, w_ffn2_out, g_final):
    d = D_MODEL
    nb = DEC_BATCH
    x0 = jnp.concatenate([x_prompt.reshape(N_PROMPT, d), x_sample.reshape(nb, d),
                          jnp.zeros((SAMPLE_PAD - nb, d), F32)], axis=0)

    w1_in, w1_out = w_ffn1_in[0].astype(BF16), w_ffn1_out[0].astype(BF16)
    w2_in, w2_out = w_ffn2_in[0].astype(BF16), w_ffn2_out[0].astype(BF16)
    w_m = w_in[0, :, :COL_IF].astype(BF16)
    w_if = jnp.pad(w_in[0, :, COL_IF:COL_Q_A], ((0, 0), (0, 128 - 2 * H_M))).astype(BF16)
    w_a = w_in[0, :, COL_Q_A:COL_GATE].astype(BF16)
    w_g = w_in[0, :, COL_GATE:].astype(BF16)
    w_mo, w_ao, w_o = w_mlstm_out[0].astype(BF16), w_attn_out[0].astype(BF16), w_out[0].astype(BF16)
    gate_bias = jnp.pad(jnp.concatenate([b_igate[0], b_fgate[0]]), (0, 128 - 2 * H_M)).reshape(1, 128)

    xn = _rmsnorm(x0, g_ffn1[0], BF16)
    y = _swiglu(xn, w1_in, w1_out)
    x1, u = _add_rmsnorm(x0, y, g_mix[0], 0.5)

    qkv_m = _mm("proj_qkv_m", [u], [(0, w_m, 0)], [], COL_O_M, BF16, _plain_epilogue)
    sig_o = _mm("proj_o_m", [u], [(0, w_m, COL_O_M)], [], V_M_COLS, BF16, _sigmoid_epilogue)
    igfg = _mm("proj_if", [u], [(0, w_if, 0)], [], 128, F32, _plain_epilogue, tn=128)
    q_a = _mm("proj_q_a", [u], [(0, w_a, 0)], [], A_COLS, BF16, _plain_epilogue)
    kv_a = _mm("proj_kv_a", [u], [(0, w_a, A_COLS)], [], 2 * A_COLS, F32, _plain_epilogue)
    gates = _mm("proj_gates", [u], [(0, w_g, 0)], [(b_gate[0].reshape(1, 2 * d), "row", 0)],
                2 * d, BF16, _gate_epilogue)

    hm_p, c_p, n_p, m_p = _mlstm_prompt(qkv_m, sig_o, igfg, gate_bias)
    qkv_s = qkv_m[N_PROMPT:N_PROMPT + nb].astype(F32)
    if_s = igfg[N_PROMPT:N_PROMPT + nb] + gate_bias
    hm_s, c_s, n_s, m_s = _mlstm_sample(
        qkv_s[:, :QK_M_COLS].reshape(nb, H_M, DK_M),
        qkv_s[:, QK_M_COLS:2 * QK_M_COLS].reshape(nb, H_M, DK_M),
        qkv_s[:, 2 * QK_M_COLS:].reshape(nb, H_M, DV_M),
        sig_o[N_PROMPT:N_PROMPT + nb].astype(F32).reshape(nb, H_M, DV_M),
        if_s[:, :H_M], if_s[:, H_M:2 * H_M],
        state_mlstm_C[0], state_mlstm_n[0], state_mlstm_m[0])
    hm = jnp.concatenate([hm_p, hm_s.reshape(nb, V_M_COLS).astype(BF16),
                          jnp.zeros((SAMPLE_PAD - nb, V_M_COLS), BF16)], axis=0)

    outs, lses = zip(*[_attn_prompt_group(q_a, kv_a, g) for g in range(N_GROUPS)])
    ha_p = _merge_groups(outs, lses)
    kv_s = kv_a[N_PROMPT:N_PROMPT + nb]
    ha_s = _attn_sample(q_a[N_PROMPT:N_PROMPT + nb].astype(F32).reshape(nb, N_GROUPS, A_GROUP_COLS),
                        kv_s[:, :A_COLS].reshape(nb, N_GROUPS, A_GROUP_COLS),
                        kv_s[:, A_COLS:].reshape(nb, N_GROUPS, A_GROUP_COLS),
                        (cache_win1_kv[0], cache_win2_kv[0], cache_win3_kv[0]))
    ha = jnp.concatenate([ha_p, ha_s.reshape(nb, A_GROUP_COLS).astype(BF16),
                          jnp.zeros((SAMPLE_PAD - nb, A_GROUP_COLS), BF16)], axis=0)

    z = _mm("mix", [hm, ha], [(0, w_mo, 0), (1, w_ao, 0)], [(gates, "tile", 0), (gates, "tile", d)],
            d, BF16, _mix_epilogue)
    x2 = _mm("proj_out", [z], [(0, w_o, 0)], [(x1, "tile", 0)], d, F32, _residual_epilogue)

    xn2 = _rmsnorm(x2, g_ffn2[0], BF16)
    y2 = _swiglu(xn2, w2_in, w2_out)
    out = _add_final_norm(x2, y2, g_final, 0.5)

    kv_p = kv_a[:N_PROMPT].reshape(BATCH, SEQ, 2, N_GROUPS, H_A, DH_A)
    kv_sr = kv_s.reshape(nb, 1, 2, N_GROUPS, H_A, DH_A)
    win_p = [kv_p[None, :, SEQ - min(w, SEQ):, :, g] for g, w in enumerate(WINDOWS)]
    win_s = [kv_sr[None, :, :, :, g] for g in range(N_GROUPS)]
    return (out[:N_PROMPT].reshape(BATCH, SEQ, d), out[N_PROMPT:N_PROMPT + nb].reshape(nb, 1, d),
            c_p[None], n_p[None], m_p.reshape(1, BATCH, H_M),
            c_s[None], n_s[None], m_s[:, :, 0][None],
            win_p[0], win_p[1], win_p[2], win_s[0], win_s[1], win_s[2])
```

```python
import functools

import numpy as np
import jax
import jax.numpy as jnp
from jax import lax
from jax.experimental import pallas as pl
from jax.experimental.pallas import tpu as pltpu

F32 = jnp.float32
BF16 = jnp.bfloat16

D_MODEL = 4096
BATCH = 2
SEQ = 4096
DEC_BATCH = 32
H_M = 8
DK_M = 256
DV_M = 512
MLSTM_CHUNK = 128
WINDOWS = (128, 512, 2048)
DILATIONS = (1, 4, 16)
N_GROUPS = 3
H_A = 8
DH_A = 128
Q_BLOCK = 128
ALIBI_MAX_EXP = 8.0
D_FF = 11008
EPS = 1e-6
NEG_INF = -1e30

QK_M_COLS = H_M * DK_M
V_M_COLS = H_M * DV_M
A_GROUP_COLS = H_A * DH_A
A_COLS = N_GROUPS * A_GROUP_COLS
COL_O_M = 2 * QK_M_COLS + V_M_COLS
COL_IF = COL_O_M + V_M_COLS
COL_Q_A = COL_IF + 2 * H_M
COL_GATE = COL_Q_A + 3 * A_COLS
LANES = 128

N_PROMPT = BATCH * SEQ
SAMPLE_PAD = 64
M_ROWS = N_PROMPT + SAMPLE_PAD

V7X_VMEM_BYTES = 64 * 1024 * 1024
VMEM_LIMIT_BYTES = V7X_VMEM_BYTES - 8 * 1024 * 1024

ROW_TILE = 192
OUT_ROW_TILE = 128
MM_TM = 1376
MM_TN = 512
FFN_TM = 688
FFN_TF = 256
FFN_TN = 512
ATT_SB = 2048
CAST_TR = 512


def _params(*semantics):
    return pltpu.CompilerParams(dimension_semantics=semantics, vmem_limit_bytes=VMEM_LIMIT_BYTES)


def _rms(x, g):
    ms = jnp.mean(x * x, axis=-1, keepdims=True)
    return x * lax.rsqrt(ms + EPS) * g


def _rmsnorm_body(x_ref, g_ref, o_ref):
    o_ref[...] = _rms(x_ref[...], g_ref[...]).astype(o_ref.dtype)


def _rmsnorm(x, g, out_dtype):
    m, d = x.shape
    return pl.pallas_call(
        _rmsnorm_body,
        grid=(m // ROW_TILE,),
        in_specs=[pl.BlockSpec((ROW_TILE, d), lambda i: (i, 0)),
                  pl.BlockSpec((1, d), lambda i: (0, 0))],
        out_specs=pl.BlockSpec((ROW_TILE, d), lambda i: (i, 0)),
        out_shape=jax.ShapeDtypeStruct((m, d), out_dtype),
        compiler_params=_params("parallel"),
        name="rmsnorm",
    )(x, g.reshape(1, d))


def _add_rmsnorm_body(x_ref, y_ref, g_ref, xo_ref, no_ref, *, scale):
    x = x_ref[...] + scale * y_ref[...]
    xo_ref[...] = x
    no_ref[...] = _rms(x, g_ref[...]).astype(no_ref.dtype)


def _add_rmsnorm(x, y, g, scale):
    m, d = x.shape
    row = pl.BlockSpec((ROW_TILE, d), lambda i: (i, 0))
    return pl.pallas_call(
        functools.partial(_add_rmsnorm_body, scale=scale),
        grid=(m // ROW_TILE,),
        in_specs=[row, row, pl.BlockSpec((1, d), lambda i: (0, 0))],
        out_specs=[row, row],
        out_shape=[jax.ShapeDtypeStruct((m, d), F32), jax.ShapeDtypeStruct((m, d), BF16)],
        compiler_params=_params("parallel"),
        name="add_rmsnorm",
    )(x, y, g.reshape(1, d))


def _add_final_norm_body(x_ref, y_ref, g_ref, op_ref, os_ref, *, scale):
    out = _rms(x_ref[...] + scale * y_ref[...], g_ref[...])
    is_sample_tile = pl.program_id(0) == N_PROMPT // OUT_ROW_TILE

    @pl.when(jnp.logical_not(is_sample_tile))
    def _():
        op_ref[...] = out

    @pl.when(is_sample_tile)
    def _():
        os_ref[...] = out[:SAMPLE_PAD]


def _add_final_norm(x, y, g, scale):
    m, d = x.shape
    n_p = N_PROMPT // OUT_ROW_TILE
    row = pl.BlockSpec((OUT_ROW_TILE, d), lambda i: (i, 0))
    return pl.pallas_call(
        functools.partial(_add_final_norm_body, scale=scale),
        grid=(n_p + 1,),
        in_specs=[row, row, pl.BlockSpec((1, d), lambda i: (0, 0))],
        out_specs=[pl.BlockSpec((OUT_ROW_TILE, d), lambda i: (jnp.minimum(i, n_p - 1), 0)),
                   pl.BlockSpec((SAMPLE_PAD, d), lambda i: (0, 0))],
        out_shape=[jax.ShapeDtypeStruct((N_PROMPT, d), F32), jax.ShapeDtypeStruct((SAMPLE_PAD, d), F32)],
        compiler_params=_params("arbitrary"),
        name="add_final_norm",
    )(x, y, g.reshape(1, d))


def _cast_rows_body(*refs, shift):
    if shift:
        a_ref, b_ref, o_ref = refs
        tall = jnp.concatenate([a_ref[...], b_ref[...]], axis=0)
        o_ref[...] = tall[shift:shift + o_ref.shape[0], :].astype(o_ref.dtype)
    else:
        a_ref, o_ref = refs
        o_ref[...] = a_ref[...].astype(o_ref.dtype)


def _cast_rows(w, row0, n, shift=0):
    k = w.shape[1]
    tr = min(CAST_TR, n)
    in_specs = [pl.BlockSpec((tr, k), lambda j: (row0 // tr + j, 0))]
    operands = [w]
    if shift:
        in_specs.append(pl.BlockSpec((shift, k), lambda j: ((row0 + (j + 1) * tr) // shift, 0)))
        operands.append(w)
    return pl.pallas_call(
        functools.partial(_cast_rows_body, shift=shift),
        grid=(n // tr,),
        in_specs=in_specs,
        out_specs=pl.BlockSpec((tr, k), lambda j: (j, 0)),
        out_shape=jax.ShapeDtypeStruct((n, k), BF16),
        compiler_params=_params("parallel"),
        name="cast_rows",
    )(*operands)


def _mm_body(*refs, dots, n_a, n_extra, epilogue):
    n_dot = len(dots)
    a_refs = refs[:n_a]
    w_refs = refs[n_a:n_a + n_dot]
    e_refs = refs[n_a + n_dot:n_a + n_dot + n_extra]
    o_ref = refs[n_a + n_dot + n_extra]
    prods = []
    for d, (ai, w_is_nk) in enumerate(dots):
        contract = (((1,), (1,)), ((), ())) if w_is_nk else (((1,), (0,)), ((), ()))
        prods.append(lax.dot_general(a_refs[ai][...], w_refs[d][...], contract, preferred_element_type=F32))
    o_ref[...] = epilogue(prods, [e[...] for e in e_refs]).astype(o_ref.dtype)


def _mm(name, a_list, w_list, extras, n, out_dtype, epilogue, tm=MM_TM, tn=MM_TN):
    m = a_list[0].shape[0]
    in_specs = [pl.BlockSpec((tm, a.shape[1]), lambda i, j: (i, 0)) for a in a_list]
    operands = list(a_list)
    for _, w, off, w_is_nk in w_list:
        if w_is_nk:
            in_specs.append(pl.BlockSpec((tn, w.shape[1]), lambda i, j, o=off // tn: (j + o, 0)))
        else:
            in_specs.append(pl.BlockSpec((w.shape[0], tn), lambda i, j, o=off // tn: (0, j + o)))
        operands.append(w)
    for arr, kind, off in extras:
        if kind == "tile":
            in_specs.append(pl.BlockSpec((tm, tn), lambda i, j, o=off // tn: (i, j + o)))
        else:
            in_specs.append(pl.BlockSpec((1, tn), lambda i, j, o=off // tn: (0, j + o)))
        operands.append(arr)
    body = functools.partial(_mm_body, dots=tuple((ai, w_is_nk) for ai, _, _, w_is_nk in w_list),
                             n_a=len(a_list), n_extra=len(extras), epilogue=epilogue)
    return pl.pallas_call(
        body,
        grid=(m // tm, n // tn),
        in_specs=in_specs,
        out_specs=pl.BlockSpec((tm, tn), lambda i, j: (i, j)),
        out_shape=jax.ShapeDtypeStruct((m, n), out_dtype),
        compiler_params=_params("parallel", "arbitrary"),
        name=name,
    )(*operands)


def _ffn_body(x_ref, wg_ref, wu_ref, wo_ref, o_ref):
    x = x_ref[...]
    g = jnp.dot(x, wg_ref[...], preferred_element_type=F32)
    u = jnp.dot(x, wu_ref[...], preferred_element_type=F32)
    h = (g * jax.nn.sigmoid(g) * u).astype(BF16)

    @pl.when(pl.program_id(1) == 0)
    def _():
        o_ref[...] = jnp.zeros_like(o_ref)

    for c in range(0, o_ref.shape[1], FFN_TN):
        o_ref[:, c:c + FFN_TN] += jnp.dot(h, wo_ref[:, c:c + FFN_TN], preferred_element_type=F32)


def _swiglu(xn, w_in, w_out):
    m, d = xn.shape
    nf = D_FF // FFN_TF
    return pl.pallas_call(
        _ffn_body,
        grid=(m // FFN_TM, nf),
        in_specs=[pl.BlockSpec((FFN_TM, d), lambda i, f: (i, 0)),
                  pl.BlockSpec((d, FFN_TF), lambda i, f: (0, f)),
                  pl.BlockSpec((d, FFN_TF), lambda i, f: (0, f + nf)),
                  pl.BlockSpec((FFN_TF, d), lambda i, f: (f, 0))],
        out_specs=pl.BlockSpec((FFN_TM, d), lambda i, f: (i, 0)),
        out_shape=jax.ShapeDtypeStruct((m, d), F32),
        compiler_params=_params("parallel", "arbitrary"),
        name="swiglu",
    )(xn, w_in, w_in, w_out)


def _log_sigmoid(x):
    return jnp.minimum(x, 0.0) - jnp.log1p(jnp.exp(-jnp.abs(x)))


def _mlstm_prompt_body(q_ref, k_ref, v_ref, so_ref, if_ref, bias_ref,
                       h_ref, c_ref, n_ref, m_ref):
    L = MLSTM_CHUNK

    @pl.when(pl.program_id(1) == 0)
    def _():
        c_ref[...] = jnp.zeros_like(c_ref)
        n_ref[...] = jnp.zeros_like(n_ref)
        m_ref[...] = jnp.zeros_like(m_ref)

    x = if_ref[...] + bias_ref[...]
    xt = x.T
    t_idx = lax.broadcasted_iota(jnp.int32, (L, L), 0)
    s_idx = lax.broadcasted_iota(jnp.int32, (L, L), 1)
    causal = s_idx <= t_idx
    for h in range(H_M):
        ig_col = x[:, h:h + 1]
        ig_row = xt[h:h + 1, :]
        lf_col = _log_sigmoid(x[:, H_M + h:H_M + h + 1])
        lf_row = _log_sigmoid(xt[H_M + h:H_M + h + 1, :])
        b_col = jnp.sum(jnp.where(causal, lf_row, 0.0), axis=1, keepdims=True)
        b_row = jnp.sum(jnp.where(t_idx <= s_idx, lf_col, 0.0), axis=0, keepdims=True)
        m_prev = m_ref[:, h:h + 1]
        d = jnp.where(causal, b_col - b_row + ig_row, -jnp.inf)
        m_inter = b_col + m_prev
        m_t = jnp.maximum(m_inter, jnp.max(d, axis=1, keepdims=True))
        w_inter = jnp.exp(m_inter - m_t)

        qh = q_ref[:, h * DK_M:(h + 1) * DK_M]
        kf = k_ref[:, h * DK_M:(h + 1) * DK_M].astype(F32) * (DK_M ** -0.5)
        vh = v_ref[:, h * DV_M:(h + 1) * DV_M]
        s = lax.dot_general(qh, kf.astype(BF16), (((1,), (1,)), ((), ())),
                            preferred_element_type=F32)
        a = s * jnp.exp(d - m_t)
        c_old = c_ref[h]
        n_old = n_ref[h:h + 1, :]
        num = (jnp.dot(a.astype(BF16), vh, preferred_element_type=F32)
               + w_inter * jnp.dot(qh, c_old.astype(BF16), preferred_element_type=F32))
        den = (jnp.sum(a, axis=1, keepdims=True)
               + w_inter * jnp.sum(qh.astype(F32) * n_old, axis=1, keepdims=True))
        hh = num / jnp.maximum(jnp.abs(den), jnp.exp(-m_t))
        gate = so_ref[:, h * DV_M:(h + 1) * DV_M].astype(F32)
        h_ref[:, h * DV_M:(h + 1) * DV_M] = (gate * hh).astype(h_ref.dtype)

        m_new = m_t[L - 1:L, :]
        b_last = b_col[L - 1:L, :]
        w_end = jnp.exp(b_last - b_col + ig_col - m_new)
        decay = jnp.exp(b_last + m_prev - m_new)
        kw = kf * w_end
        dc = jnp.dot(kw.T.astype(BF16), vh, preferred_element_type=F32)
        c_ref[h] = decay * c_old + dc
        n_ref[h:h + 1, :] = decay * n_old + jnp.sum(kw, axis=0, keepdims=True)
        m_ref[:, h:h + 1] = m_new


def _mlstm_prompt(qkv_m, sig_o, igfg, gate_bias):
    nc = SEQ // MLSTM_CHUNK
    L = MLSTM_CHUNK

    def rows(width, col):
        return pl.BlockSpec((L, width), lambda b, c: (b * nc + c, col))

    return pl.pallas_call(
        _mlstm_prompt_body,
        grid=(BATCH, nc),
        in_specs=[rows(QK_M_COLS, 0), rows(QK_M_COLS, 1), rows(V_M_COLS, 1), rows(V_M_COLS, 0),
                  rows(LANES, 0), pl.BlockSpec((1, LANES), lambda b, c: (0, 0))],
        out_specs=[rows(V_M_COLS, 0),
                   pl.BlockSpec((None, H_M, DK_M, DV_M), lambda b, c: (b, 0, 0, 0)),
                   pl.BlockSpec((None, H_M, DK_M), lambda b, c: (b, 0, 0)),
                   pl.BlockSpec((None, 1, H_M), lambda b, c: (b, 0, 0))],
        out_shape=[jax.ShapeDtypeStruct((M_ROWS, V_M_COLS), BF16),
                   jax.ShapeDtypeStruct((BATCH, H_M, DK_M, DV_M), F32),
                   jax.ShapeDtypeStruct((BATCH, H_M, DK_M), F32),
                   jax.ShapeDtypeStruct((BATCH, 1, H_M), F32)],
        compiler_params=_params("parallel", "arbitrary"),
        name="mlstm_prompt",
    )(qkv_m, qkv_m, qkv_m, sig_o, igfg, gate_bias)


def _mlstm_sample_body(q_ref, qt_ref, k_ref, kt_ref, v_ref, so_ref, ig_ref, fg_ref, m0_ref,
                       c0_ref, n0_ref, h_ref, c_ref, n_ref, m_ref):
    for h in range(H_M):
        ig = ig_ref[h:h + 1, 0:1]
        lf = _log_sigmoid(fg_ref[h:h + 1, 0:1])
        m_prev = m0_ref[h:h + 1, 0:1]
        m_inter = lf + m_prev
        m_t = jnp.maximum(m_inter, ig)
        w_inter = jnp.exp(m_inter - m_t)
        q_row = q_ref[h:h + 1, :]
        k_row = k_ref[h:h + 1, :] * (DK_M ** -0.5)
        q_col = qt_ref[:, h:h + 1]
        k_col = kt_ref[:, h:h + 1] * (DK_M ** -0.5)
        v_row = v_ref[h:h + 1, :]
        c_old = c0_ref[h]
        n_old = n0_ref[h:h + 1, :]
        a = jnp.sum(q_row * k_row, axis=1, keepdims=True) * jnp.exp(ig - m_t)
        num = a * v_row + w_inter * jnp.sum(q_col * c_old, axis=0, keepdims=True)
        den = a + w_inter * jnp.sum(q_row * n_old, axis=1, keepdims=True)
        hh = num / jnp.maximum(jnp.abs(den), jnp.exp(-m_t))
        h_ref[h:h + 1, :] = so_ref[h:h + 1, :] * hh
        w_end = jnp.exp(ig - m_t)
        decay = jnp.exp(m_inter - m_t)
        c_ref[h] = decay * c_old + (k_col * w_end) * v_row
        n_ref[h:h + 1, :] = decay * n_old + k_row * w_end
        m_ref[h:h + 1, :] = jnp.broadcast_to(m_t, (1, LANES))


def _mlstm_sample(q, k, v, so, ig, fg, c0, n0, m0):
    nb = q.shape[0]

    def lanes(a):
        return jnp.broadcast_to(a[:, :, None], (nb, H_M, LANES))

    def per_b(*shape):
        return pl.BlockSpec((None,) + shape, lambda b: (b,) + (0,) * len(shape))

    return pl.pallas_call(
        _mlstm_sample_body,
        grid=(nb,),
        in_specs=[per_b(H_M, DK_M), per_b(DK_M, H_M), per_b(H_M, DK_M), per_b(DK_M, H_M),
                  per_b(H_M, DV_M), per_b(H_M, DV_M), per_b(H_M, LANES), per_b(H_M, LANES),
                  per_b(H_M, LANES), per_b(H_M, DK_M, DV_M), per_b(H_M, DK_M)],
        out_specs=[per_b(H_M, DV_M), per_b(H_M, DK_M, DV_M), per_b(H_M, DK_M), per_b(H_M, LANES)],
        out_shape=[jax.ShapeDtypeStruct((nb, H_M, DV_M), F32),
                   jax.ShapeDtypeStruct((nb, H_M, DK_M, DV_M), F32),
                   jax.ShapeDtypeStruct((nb, H_M, DK_M), F32),
                   jax.ShapeDtypeStruct((nb, H_M, LANES), F32)],
        compiler_params=_params("parallel"),
        name="mlstm_sample",
    )(q, jnp.swapaxes(q, 1, 2), k, jnp.swapaxes(k, 1, 2), v, so, lanes(ig), lanes(fg), lanes(m0),
      c0, n0)


def _alibi_slopes():
    n = N_GROUPS * H_A
    e = np.arange(1, n + 1, dtype=np.float64) * (-ALIBI_MAX_EXP / n)
    return np.exp2(e).reshape(N_GROUPS, H_A)


def _softmax_mix(lses, outs):
    m = jnp.maximum(jnp.maximum(lses[0], lses[1]), lses[2])
    e = [jnp.exp(x - m) for x in lses]
    z = e[0] + e[1] + e[2]
    acc = (e[0] / z) * outs[0]
    for g in (1, 2):
        acc = acc + (e[g] / z) * outs[g]
    return acc


def _attn_prompt_body(slopes_ref, *refs):
    G = N_GROUPS
    q_refs, kc_refs, kp_refs = refs[0:G], refs[G:2 * G], refs[2 * G:3 * G]
    vc_refs, vp_refs = refs[3 * G:4 * G], refs[4 * G:5 * G]
    o_ref = refs[5 * G]
    o_scr, l_scr = refs[5 * G + 1:6 * G + 1], refs[6 * G + 1:7 * G + 1]
    L = Q_BLOCK
    head = pl.program_id(1)
    first_key = jnp.where(pl.program_id(2) == 0, L, 0)
    t_idx = lax.broadcasted_iota(jnp.int32, (L, 2 * L), 0)
    s_idx = lax.broadcasted_iota(jnp.int32, (L, 2 * L), 1)
    step = t_idx + L - s_idx
    in_window = jnp.logical_and(step >= 0, step <= L)
    in_window_first = jnp.logical_and(in_window, s_idx >= first_key)
    nt = (((1,), (1,)), ((), ()))

    def rows(start, size, dil):
        return pl.ds(start, size) if dil == 1 else pl.ds(start, size, stride=dil)

    for g in range(G):
        dil = DILATIONS[g]
        bias = step.astype(F32) * (slopes_ref[g, head] * dil)

        def sub_block(j, r, g=g, dil=dil, bias=bias):
            start = j * L * dil + r
            q = q_refs[g][rows(start, L, dil), :].astype(BF16)
            if j == 0:
                k2 = jnp.concatenate([kp_refs[g][rows(r, L, dil), :], kc_refs[g][rows(start, L, dil), :]], axis=0)
                v2 = jnp.concatenate([vp_refs[g][rows(r, L, dil), :], vc_refs[g][rows(start, L, dil), :]], axis=0)
                mask = in_window_first
            else:
                k2 = kc_refs[g][rows(start - L * dil, 2 * L, dil), :]
                v2 = vc_refs[g][rows(start - L * dil, 2 * L, dil), :]
                mask = in_window
            s = lax.dot_general(q, k2.astype(BF16), nt, preferred_element_type=F32)
            s = jnp.where(mask, s * (DH_A ** -0.5) - bias, NEG_INF)
            m = jnp.max(s, axis=1, keepdims=True)
            p = jnp.exp(s - m)
            l = jnp.sum(p, axis=1, keepdims=True)
            o = jnp.dot(p.astype(BF16), v2.astype(BF16), preferred_element_type=F32) / l
            o_scr[g][rows(start, L, dil), :] = o
            l_scr[g][rows(start, L, dil), :] = jnp.broadcast_to(m + jnp.log(l), (L, LANES))

        for j in range(ATT_SB // (L * dil)):
            if dil == 1:
                sub_block(j, 0)
            else:
                def residue(r, carry, j=j):
                    sub_block(j, r)
                    return carry
                lax.fori_loop(0, dil, residue, 0, unroll=4)

    chunk = 256
    for c in range(0, ATT_SB, chunk):
        lses = [l_scr[g][c:c + chunk, :] for g in range(G)]
        outs = [o_scr[g][c:c + chunk, :] for g in range(G)]
        o_ref[c:c + chunk, :] = _softmax_mix(lses, outs).astype(o_ref.dtype)


def _attn_prompt(qkv_a):
    ns = SEQ // ATT_SB
    L = Q_BLOCK
    heads_per_part = N_GROUPS * H_A

    def cur(part, g):
        return pl.BlockSpec((ATT_SB, DH_A), lambda b, h, i: (b * ns + i, part * heads_per_part + g * H_A + h))

    def prev(part, g):
        span = L * DILATIONS[g]
        f = ATT_SB // span
        return pl.BlockSpec((span, DH_A),
                            lambda b, h, i: (b * ns * f + jnp.maximum(i * f - 1, 0),
                                             part * heads_per_part + g * H_A + h))

    groups = range(N_GROUPS)
    in_specs = ([pl.BlockSpec(memory_space=pltpu.SMEM)]
                + [cur(0, g) for g in groups] + [cur(1, g) for g in groups] + [prev(1, g) for g in groups]
                + [cur(2, g) for g in groups] + [prev(2, g) for g in groups])
    slopes = jnp.asarray(_alibi_slopes(), F32)
    return pl.pallas_call(
        _attn_prompt_body,
        grid=(BATCH, H_A, ns),
        in_specs=in_specs,
        out_specs=pl.BlockSpec((ATT_SB, DH_A), lambda b, h, i: (b * ns + i, h)),
        out_shape=jax.ShapeDtypeStruct((M_ROWS, A_GROUP_COLS), BF16),
        scratch_shapes=[pltpu.VMEM((ATT_SB, DH_A), F32)] * (2 * N_GROUPS),
        compiler_params=_params("parallel", "parallel", "arbitrary"),
        name="attn_prompt",
    )(slopes, *([qkv_a] * (5 * N_GROUPS)))


def _attn_sample_body(slopes_ref, q_ref, kn_ref, vn_ref, c0_ref, c1_ref, c2_ref, o_ref):
    L = Q_BLOCK
    c_refs = (c0_ref, c1_ref, c2_ref)
    steps = (L - lax.broadcasted_iota(jnp.int32, (L, 1, 1), 0)).astype(F32)
    outs, lses = [], []
    for g in range(N_GROUPS):
        q = q_ref[g]
        slope = slopes_ref[g][:, 0:1]
        s = (jnp.sum(c_refs[g][:, 0] * q[None], axis=-1, keepdims=True) * (DH_A ** -0.5)
             - (slope * DILATIONS[g])[None] * steps)
        s_new = jnp.sum(q * kn_ref[g], axis=-1, keepdims=True) * (DH_A ** -0.5)
        m = jnp.maximum(jnp.max(s, axis=0), s_new)
        p = jnp.exp(s - m[None])
        p_new = jnp.exp(s_new - m)
        l = jnp.sum(p, axis=0) + p_new
        outs.append((jnp.sum(p * c_refs[g][:, 1], axis=0) + p_new * vn_ref[g]) / l)
        lses.append(m + jnp.log(l))
    o_ref[...] = _softmax_mix(lses, outs)


def _attn_sample(q, kn, vn, caches):
    nb = q.shape[0]
    small = pl.BlockSpec((None, N_GROUPS, H_A, DH_A), lambda b: (b, 0, 0, 0))
    cache_specs, cache_views = [], []
    for g, c in enumerate(caches):
        dil = DILATIONS[g]
        assert c.shape[1] == Q_BLOCK * dil
        cache_views.append(c.reshape(nb, Q_BLOCK, dil, 2, H_A, DH_A))
        cache_specs.append(pl.BlockSpec((None, Q_BLOCK, None, 2, H_A, DH_A), lambda b: (b, 0, 0, 0, 0, 0)))
    slopes = jnp.broadcast_to(jnp.asarray(_alibi_slopes(), F32)[:, :, None], (N_GROUPS, H_A, LANES))
    return pl.pallas_call(
        _attn_sample_body,
        grid=(nb,),
        in_specs=[pl.BlockSpec((N_GROUPS, H_A, LANES), lambda b: (0, 0, 0)), small, small, small] + cache_specs,
        out_specs=pl.BlockSpec((None, H_A, DH_A), lambda b: (b, 0, 0)),
        out_shape=jax.ShapeDtypeStruct((nb, H_A, DH_A), F32),
        compiler_params=_params("parallel"),
        name="attn_sample",
    )(slopes, q, kn, vn, *cache_views)


def _sigmoid_epilogue(prods, extras):
    return jax.nn.sigmoid(prods[0])


def _gate_epilogue(prods, extras):
    return jax.nn.sigmoid(prods[0] + extras[0])


def _plain_epilogue(prods, extras):
    return prods[0]


def _mix_epilogue(prods, extras):
    return extras[0].astype(F32) * prods[0] + extras[1].astype(F32) * prods[1]


def _residual_epilogue(prods, extras):
    return extras[0] + prods[0]


def _with_sample_rows(prompt_rows_array, sample_rows):
    nb, c = sample_rows.shape
    block = jnp.concatenate([sample_rows, jnp.zeros((SAMPLE_PAD - nb, c), sample_rows.dtype)], axis=0)
    return lax.dynamic_update_slice(prompt_rows_array, block.astype(prompt_rows_array.dtype), (N_PROMPT, 0))


def kernel(x_prompt, x_sample, state_mlstm_C, state_mlstm_n, state_mlstm_m, cache_win1_kv, cache_win2_kv, cache_win3_kv, g_ffn1, w_ffn1_in, w_ffn1_out, g_mix, w_in, b_gate, b_igate, b_fgate, w_mlstm_out, w_attn_out, w_out, g_ffn2, w_ffn2_in, w_ffn2_out, g_final):
    d = D_MODEL
    nb = DEC_BATCH
    x0 = jnp.concatenate([x_prompt.reshape(N_PROMPT, d), x_sample.reshape(nb, d),
                          jnp.zeros((SAMPLE_PAD - nb, d), F32)], axis=0)

    w1_in, w1_out = w_ffn1_in[0].astype(BF16), w_ffn1_out[0].astype(BF16)
    w2_in, w2_out = w_ffn2_in[0].astype(BF16), w_ffn2_out[0].astype(BF16)
    w_in_nk = w_in[0].T
    w_m = _cast_rows(w_in_nk, 0, COL_IF)
    w_if = _cast_rows(w_in_nk, COL_IF, LANES)
    w_ag = _cast_rows(w_in_nk, COL_IF, 3 * A_COLS + 2 * d, shift=COL_Q_A - COL_IF)
    w_mo, w_ao, w_o = w_mlstm_out[0].astype(BF16), w_attn_out[0].astype(BF16), w_out[0].astype(BF16)
    gate_bias = jnp.pad(jnp.concatenate([b_igate[0], b_fgate[0]]), (0, LANES - 2 * H_M)).reshape(1, LANES)

    xn = _rmsnorm(x0, g_ffn1[0], BF16)
    y = _swiglu(xn, w1_in, w1_out)
    x1, u = _add_rmsnorm(x0, y, g_mix[0], 0.5)

    qkv_m = _mm("proj_qkv_m", [u], [(0, w_m, 0, True)], [], COL_O_M, BF16, _plain_epilogue)
    sig_o = _mm("proj_o_m", [u], [(0, w_m, COL_O_M, True)], [], V_M_COLS, BF16, _sigmoid_epilogue)
    igfg = _mm("proj_if", [u], [(0, w_if, 0, True)], [], LANES, F32, _plain_epilogue, tn=LANES)
    qkv_a = _mm("proj_qkv_a", [u], [(0, w_ag, 0, True)], [], 3 * A_COLS, F32, _plain_epilogue)
    gates = _mm("proj_gates", [u], [(0, w_ag, 3 * A_COLS, True)], [(b_gate[0].reshape(1, 2 * d), "row", 0)],
                2 * d, BF16, _gate_epilogue)

    hm, c_p, n_p, m_p = _mlstm_prompt(qkv_m, sig_o, igfg, gate_bias)
    qkv_s = qkv_m[N_PROMPT:N_PROMPT + nb].astype(F32)
    if_s = igfg[N_PROMPT:N_PROMPT + nb] + gate_bias
    hm_s, c_s, n_s, m_s = _mlstm_sample(
        qkv_s[:, :QK_M_COLS].reshape(nb, H_M, DK_M),
        qkv_s[:, QK_M_COLS:2 * QK_M_COLS].reshape(nb, H_M, DK_M),
        qkv_s[:, 2 * QK_M_COLS:].reshape(nb, H_M, DV_M),
        sig_o[N_PROMPT:N_PROMPT + nb].astype(F32).reshape(nb, H_M, DV_M),
        if_s[:, :H_M], if_s[:, H_M:2 * H_M],
        state_mlstm_C[0], state_mlstm_n[0], state_mlstm_m[0])
    hm = _with_sample_rows(hm, hm_s.reshape(nb, V_M_COLS))

    ha = _attn_prompt(qkv_a)
    qkv_as = qkv_a[N_PROMPT:N_PROMPT + nb].reshape(nb, 3, N_GROUPS, H_A, DH_A)
    ha_s = _attn_sample(qkv_as[:, 0], qkv_as[:, 1], qkv_as[:, 2],
                        (cache_win1_kv[0], cache_win2_kv[0], cache_win3_kv[0]))
    ha = _with_sample_rows(ha, ha_s.reshape(nb, A_GROUP_COLS))

    z = _mm("mix", [hm, ha], [(0, w_mo, 0, False), (1, w_ao, 0, False)],
            [(gates, "tile", 0), (gates, "tile", d)], d, BF16, _mix_epilogue)
    x2 = _mm("proj_out", [z], [(0, w_o, 0, False)], [(x1, "tile", 0)], d, F32, _residual_epilogue)

    xn2 = _rmsnorm(x2, g_ffn2[0], BF16)
    y2 = _swiglu(xn2, w2_in, w2_out)
    out_p, out_s = _add_final_norm(x2, y2, g_final, 0.5)

    kv_rows = qkv_a[:N_PROMPT].reshape(BATCH, SEQ, 3 * A_COLS)
    win_p, win_s = [], []
    for g, w in enumerate(WINDOWS):
        last = kv_rows[:, SEQ - min(w, SEQ):]
        k_cols = last[:, :, A_COLS + g * A_GROUP_COLS:A_COLS + (g + 1) * A_GROUP_COLS]
        v_cols = last[:, :, 2 * A_COLS + g * A_GROUP_COLS:2 * A_COLS + (g + 1) * A_GROUP_COLS]
        win_p.append(jnp.stack([k_cols, v_cols], axis=2).reshape(1, BATCH, min(w, SEQ), 2, H_A, DH_A))
        win_s.append(qkv_as[:, 1:3, g].reshape(1, nb, 1, 2, H_A, DH_A))
    return (out_p.reshape(BATCH, SEQ, d), out_s[:nb].reshape(nb, 1, d),
            c_p[None], n_p[None], m_p.reshape(1, BATCH, H_M),
            c_s[None], n_s[None], m_s[:, :, 0][None],
            win_p[0], win_p[1], win_p[2], win_s[0], win_s[1], win_s[2])
```

```python
import functools

import numpy as np
import jax
import jax.numpy as jnp
from jax import lax
from jax.experimental import pallas as pl
from jax.experimental.pallas import tpu as pltpu

F32 = jnp.float32
BF16 = jnp.bfloat16

D_MODEL = 4096
BATCH = 2
SEQ = 4096
DEC_BATCH = 32
H_M = 8
DK_M = 256
DV_M = 512
MLSTM_CHUNK = 128
WINDOWS = (128, 512, 2048)
DILATIONS = (1, 4, 16)
N_GROUPS = 3
H_A = 8
DH_A = 128
Q_BLOCK = 128
ALIBI_MAX_EXP = 8.0
D_FF = 11008
EPS = 1e-6
NEG_INF = -1e30

QK_M_COLS = H_M * DK_M
V_M_COLS = H_M * DV_M
A_GROUP_COLS = H_A * DH_A
A_COLS = N_GROUPS * A_GROUP_COLS
COL_O_M = 2 * QK_M_COLS + V_M_COLS
COL_IF = COL_O_M + V_M_COLS
COL_Q_A = COL_IF + 2 * H_M
COL_GATE = COL_Q_A + 3 * A_COLS
LANES = 128

N_PROMPT = BATCH * SEQ
SAMPLE_PAD = 64
M_ROWS = N_PROMPT + SAMPLE_PAD

V7X_VMEM_BYTES = 64 * 1024 * 1024
VMEM_LIMIT_BYTES = V7X_VMEM_BYTES - 8 * 1024 * 1024

ROW_TILE = 192
OUT_ROW_TILE = 128
MM_TM = 1376
MM_TN = 512
FFN_TM = 688
FFN_TF = 256
FFN_TN = 512
ATT_SB = 2048
CAST_TR = 512


def _params(*semantics):
    return pltpu.CompilerParams(dimension_semantics=semantics, vmem_limit_bytes=VMEM_LIMIT_BYTES)


def _rms(x, g):
    ms = jnp.mean(x * x, axis=-1, keepdims=True)
    return x * lax.rsqrt(ms + EPS) * g


def _rmsnorm_body(x_ref, g_ref, o_ref):
    o_ref[...] = _rms(x_ref[...], g_ref[...]).astype(o_ref.dtype)


def _rmsnorm(x, g, out_dtype):
    m, d = x.shape
    return pl.pallas_call(
        _rmsnorm_body,
        grid=(m // ROW_TILE,),
        in_specs=[pl.BlockSpec((ROW_TILE, d), lambda i: (i, 0)),
                  pl.BlockSpec((1, d), lambda i: (0, 0))],
        out_specs=pl.BlockSpec((ROW_TILE, d), lambda i: (i, 0)),
        out_shape=jax.ShapeDtypeStruct((m, d), out_dtype),
        compiler_params=_params("parallel"),
        name="rmsnorm",
    )(x, g.reshape(1, d))


def _input_tile(xp_ref, xs_ref):
    is_sample_tile = pl.program_id(0) == N_PROMPT // OUT_ROW_TILE
    pad = jnp.zeros((OUT_ROW_TILE - xs_ref.shape[0], xs_ref.shape[1]), F32)
    return jnp.where(is_sample_tile, jnp.concatenate([xs_ref[...], pad], axis=0), xp_ref[...])


def _input_specs(d, nb):
    n_p = N_PROMPT // OUT_ROW_TILE
    return [pl.BlockSpec((OUT_ROW_TILE, d), lambda i: (jnp.minimum(i, n_p - 1), 0)),
            pl.BlockSpec((nb, d), lambda i: (0, 0))]


def _rmsnorm_inputs_body(xp_ref, xs_ref, g_ref, o_ref):
    o_ref[...] = _rms(_input_tile(xp_ref, xs_ref), g_ref[...]).astype(o_ref.dtype)


def _rmsnorm_inputs(xp, xs, g):
    d = xp.shape[1]
    row = pl.BlockSpec((OUT_ROW_TILE, d), lambda i: (i, 0))
    return pl.pallas_call(
        _rmsnorm_inputs_body,
        grid=(N_PROMPT // OUT_ROW_TILE + 1,),
        in_specs=_input_specs(d, xs.shape[0]) + [pl.BlockSpec((1, d), lambda i: (0, 0))],
        out_specs=row,
        out_shape=jax.ShapeDtypeStruct((M_ROWS, d), BF16),
        compiler_params=_params("parallel"),
        name="rmsnorm_inputs",
    )(xp, xs, g.reshape(1, d))


def _add_rmsnorm_inputs_body(xp_ref, xs_ref, y_ref, g_ref, xo_ref, no_ref, *, scale):
    x = _input_tile(xp_ref, xs_ref) + scale * y_ref[...]
    xo_ref[...] = x
    no_ref[...] = _rms(x, g_ref[...]).astype(no_ref.dtype)


def _add_rmsnorm_inputs(xp, xs, y, g, scale):
    d = xp.shape[1]
    row = pl.BlockSpec((OUT_ROW_TILE, d), lambda i: (i, 0))
    return pl.pallas_call(
        functools.partial(_add_rmsnorm_inputs_body, scale=scale),
        grid=(N_PROMPT // OUT_ROW_TILE + 1,),
        in_specs=_input_specs(d, xs.shape[0]) + [row, pl.BlockSpec((1, d), lambda i: (0, 0))],
        out_specs=[row, row],
        out_shape=[jax.ShapeDtypeStruct((M_ROWS, d), F32), jax.ShapeDtypeStruct((M_ROWS, d), BF16)],
        compiler_params=_params("parallel"),
        name="add_rmsnorm_inputs",
    )(xp, xs, y, g.reshape(1, d))


def _add_final_norm_body(x_ref, y_ref, g_ref, op_ref, os_ref, *, scale):
    out = _rms(x_ref[...] + scale * y_ref[...], g_ref[...])
    is_sample_tile = pl.program_id(0) == N_PROMPT // OUT_ROW_TILE

    @pl.when(jnp.logical_not(is_sample_tile))
    def _():
        op_ref[...] = out

    @pl.when(is_sample_tile)
    def _():
        os_ref[...] = out[:SAMPLE_PAD]


def _add_final_norm(x, y, g, scale):
    m, d = x.shape
    n_p = N_PROMPT // OUT_ROW_TILE
    row = pl.BlockSpec((OUT_ROW_TILE, d), lambda i: (i, 0))
    return pl.pallas_call(
        functools.partial(_add_final_norm_body, scale=scale),
        grid=(n_p + 1,),
        in_specs=[row, row, pl.BlockSpec((1, d), lambda i: (0, 0))],
        out_specs=[pl.BlockSpec((OUT_ROW_TILE, d), lambda i: (jnp.minimum(i, n_p - 1), 0)),
                   pl.BlockSpec((SAMPLE_PAD, d), lambda i: (0, 0))],
        out_shape=[jax.ShapeDtypeStruct((N_PROMPT, d), F32), jax.ShapeDtypeStruct((SAMPLE_PAD, d), F32)],
        compiler_params=_params("arbitrary"),
        name="add_final_norm",
    )(x, y, g.reshape(1, d))


def _cast_rows_body(*refs, shift):
    if shift:
        a_ref, b_ref, o_ref = refs
        tall = jnp.concatenate([a_ref[...], b_ref[...]], axis=0)
        o_ref[...] = tall[shift:shift + o_ref.shape[0], :].astype(o_ref.dtype)
    else:
        a_ref, o_ref = refs
        o_ref[...] = a_ref[...].astype(o_ref.dtype)


def _cast_rows(w, row0, n, shift=0):
    k = w.shape[1]
    tr = min(CAST_TR, n)
    in_specs = [pl.BlockSpec((tr, k), lambda j: (row0 // tr + j, 0))]
    operands = [w]
    if shift:
        in_specs.append(pl.BlockSpec((shift, k), lambda j: ((row0 + (j + 1) * tr) // shift, 0)))
        operands.append(w)
    return pl.pallas_call(
        functools.partial(_cast_rows_body, shift=shift),
        grid=(n // tr,),
        in_specs=in_specs,
        out_specs=pl.BlockSpec((tr, k), lambda j: (j, 0)),
        out_shape=jax.ShapeDtypeStruct((n, k), BF16),
        compiler_params=_params("parallel"),
        name="cast_rows",
    )(*operands)


def _side_cast_specs(side, n_inner):
    in_specs, out_specs, out_shapes = [], [], []
    for src, rb, cb in side:
        r, c = src.shape
        n_cb = c // cb
        last = (r // rb) * n_cb - 1

        def index(i, j, n_cb=n_cb, last=last):
            blk = jnp.minimum(i * n_inner + j, last)
            return lax.div(blk, n_cb), lax.rem(blk, n_cb)

        in_specs.append(pl.BlockSpec((rb, cb), index))
        out_specs.append(pl.BlockSpec((rb, cb), index))
        out_shapes.append(jax.ShapeDtypeStruct((r, c), BF16))
    return in_specs, out_specs, out_shapes


def _side_cast(src_refs, dst_refs):
    for src_ref, dst_ref in zip(src_refs, dst_refs):
        dst_ref[...] = src_ref[...].astype(dst_ref.dtype)


def _mm_body(*refs, dots, n_a, n_extra, n_side, epilogue):
    n_dot = len(dots)
    n_in = n_a + n_dot + n_extra
    a_refs = refs[:n_a]
    w_refs = refs[n_a:n_a + n_dot]
    e_refs = refs[n_a + n_dot:n_in]
    o_ref = refs[n_in + n_side]
    prods = []
    for d, (ai, w_is_nk) in enumerate(dots):
        contract = (((1,), (1,)), ((), ())) if w_is_nk else (((1,), (0,)), ((), ()))
        prods.append(lax.dot_general(a_refs[ai][...], w_refs[d][...], contract, preferred_element_type=F32))
    o_ref[...] = epilogue(prods, [e[...] for e in e_refs]).astype(o_ref.dtype)
    _side_cast(refs[n_in:n_in + n_side], refs[n_in + n_side + 1:])


def _mm(name, a_list, w_list, extras, n, out_dtype, epilogue, tm=MM_TM, tn=MM_TN, side=()):
    m = a_list[0].shape[0]
    in_specs = [pl.BlockSpec((tm, a.shape[1]), lambda i, j: (i, 0)) for a in a_list]
    operands = list(a_list)
    for _, w, off, w_is_nk in w_list:
        if w_is_nk:
            in_specs.append(pl.BlockSpec((tn, w.shape[1]), lambda i, j, o=off // tn: (j + o, 0)))
        else:
            in_specs.append(pl.BlockSpec((w.shape[0], tn), lambda i, j, o=off // tn: (0, j + o)))
        operands.append(w)
    for arr, kind, off in extras:
        if kind == "tile":
            in_specs.append(pl.BlockSpec((tm, tn), lambda i, j, o=off // tn: (i, j + o)))
        else:
            in_specs.append(pl.BlockSpec((1, tn), lambda i, j, o=off // tn: (0, j + o)))
        operands.append(arr)
    side_in, side_out, side_shapes = _side_cast_specs(side, n // tn)
    body = functools.partial(_mm_body, dots=tuple((ai, w_is_nk) for ai, _, _, w_is_nk in w_list),
                             n_a=len(a_list), n_extra=len(extras), n_side=len(side), epilogue=epilogue)
    outs = pl.pallas_call(
        body,
        grid=(m // tm, n // tn),
        in_specs=in_specs + side_in,
        out_specs=[pl.BlockSpec((tm, tn), lambda i, j: (i, j))] + side_out,
        out_shape=[jax.ShapeDtypeStruct((m, n), out_dtype)] + side_shapes,
        compiler_params=_params("arbitrary", "arbitrary"),
        name=name,
    )(*operands, *[src for src, _, _ in side])
    return outs if side else outs[0]


def _ffn_body(x_ref, wg_ref, wu_ref, wo_ref, *rest):
    n_side = (len(rest) - 1) // 2
    o_ref = rest[n_side]
    _side_cast(rest[:n_side], rest[n_side + 1:])
    x = x_ref[...]
    g = jnp.dot(x, wg_ref[...], preferred_element_type=F32)
    u = jnp.dot(x, wu_ref[...], preferred_element_type=F32)
    h = (g * jax.nn.sigmoid(g) * u).astype(BF16)

    @pl.when(pl.program_id(1) == 0)
    def _():
        o_ref[...] = jnp.zeros_like(o_ref)

    for c in range(0, o_ref.shape[1], FFN_TN):
        o_ref[:, c:c + FFN_TN] += jnp.dot(h, wo_ref[:, c:c + FFN_TN], preferred_element_type=F32)


def _swiglu(xn, w_in, w_out, side=()):
    m, d = xn.shape
    nf = D_FF // FFN_TF
    side_in, side_out, side_shapes = _side_cast_specs(side, nf)
    outs = pl.pallas_call(
        _ffn_body,
        grid=(m // FFN_TM, nf),
        in_specs=[pl.BlockSpec((FFN_TM, d), lambda i, f: (i, 0)),
                  pl.BlockSpec((d, FFN_TF), lambda i, f: (0, f)),
                  pl.BlockSpec((d, FFN_TF), lambda i, f: (0, f + nf)),
                  pl.BlockSpec((FFN_TF, d), lambda i, f: (f, 0))] + side_in,
        out_specs=[pl.BlockSpec((FFN_TM, d), lambda i, f: (i, 0))] + side_out,
        out_shape=[jax.ShapeDtypeStruct((m, d), F32)] + side_shapes,
        compiler_params=_params("arbitrary", "arbitrary"),
        name="swiglu",
    )(xn, w_in, w_in, w_out, *[src for src, _, _ in side])
    return outs if side else outs[0]


def _log_sigmoid(x):
    return jnp.minimum(x, 0.0) - jnp.log1p(jnp.exp(-jnp.abs(x)))


def _mlstm_prompt_body(q_ref, k_ref, v_ref, so_ref, if_ref, bias_ref,
                       h_ref, c_ref, n_ref, m_ref):
    L = MLSTM_CHUNK

    @pl.when(pl.program_id(1) == 0)
    def _():
        c_ref[...] = jnp.zeros_like(c_ref)
        n_ref[...] = jnp.zeros_like(n_ref)
        m_ref[...] = jnp.zeros_like(m_ref)

    x = if_ref[...] + bias_ref[...]
    xt = x.T
    t_idx = lax.broadcasted_iota(jnp.int32, (L, L), 0)
    s_idx = lax.broadcasted_iota(jnp.int32, (L, L), 1)
    causal = s_idx <= t_idx
    for h in range(H_M):
        ig_col = x[:, h:h + 1]
        ig_row = xt[h:h + 1, :]
        lf_col = _log_sigmoid(x[:, H_M + h:H_M + h + 1])
        lf_row = _log_sigmoid(xt[H_M + h:H_M + h + 1, :])
        b_col = jnp.sum(jnp.where(causal, lf_row, 0.0), axis=1, keepdims=True)
        b_row = jnp.sum(jnp.where(t_idx <= s_idx, lf_col, 0.0), axis=0, keepdims=True)
        m_prev = m_ref[:, h:h + 1]
        d = jnp.where(causal, b_col - b_row + ig_row, -jnp.inf)
        m_inter = b_col + m_prev
        m_t = jnp.maximum(m_inter, jnp.max(d, axis=1, keepdims=True))
        w_inter = jnp.exp(m_inter - m_t)

        qh = q_ref[:, h * DK_M:(h + 1) * DK_M]
        kf = k_ref[:, h * DK_M:(h + 1) * DK_M].astype(F32) * (DK_M ** -0.5)
        vh = v_ref[:, h * DV_M:(h + 1) * DV_M]
        s = lax.dot_general(qh, kf.astype(BF16), (((1,), (1,)), ((), ())),
                            preferred_element_type=F32)
        a = s * jnp.exp(d - m_t)
        c_old = c_ref[h]
        n_old = n_ref[h:h + 1, :]
        num = (jnp.dot(a.astype(BF16), vh, preferred_element_type=F32)
               + w_inter * jnp.dot(qh, c_old.astype(BF16), preferred_element_type=F32))
        den = (jnp.sum(a, axis=1, keepdims=True)
               + w_inter * jnp.sum(qh.astype(F32) * n_old, axis=1, keepdims=True))
        hh = num / jnp.maximum(jnp.abs(den), jnp.exp(-m_t))
        gate = so_ref[:, h * DV_M:(h + 1) * DV_M].astype(F32)
        h_ref[:, h * DV_M:(h + 1) * DV_M] = (gate * hh).astype(h_ref.dtype)

        m_new = m_t[L - 1:L, :]
        b_last = b_col[L - 1:L, :]
        w_end = jnp.exp(b_last - b_col + ig_col - m_new)
        decay = jnp.exp(b_last + m_prev - m_new)
        kw = kf * w_end
        dc = jnp.dot(kw.T.astype(BF16), vh, preferred_element_type=F32)
        c_ref[h] = decay * c_old + dc
        n_ref[h:h + 1, :] = decay * n_old + jnp.sum(kw, axis=0, keepdims=True)
        m_ref[:, h:h + 1] = m_new


def _mlstm_prompt(qkv_m, sig_o, igfg, gate_bias):
    nc = SEQ // MLSTM_CHUNK
    L = MLSTM_CHUNK

    def rows(width, col):
        return pl.BlockSpec((L, width), lambda b, c: (b * nc + c, col))

    return pl.pallas_call(
        _mlstm_prompt_body,
        grid=(BATCH, nc),
        in_specs=[rows(QK_M_COLS, 0), rows(QK_M_COLS, 1), rows(V_M_COLS, 1), rows(V_M_COLS, 0),
                  rows(LANES, 0), pl.BlockSpec((1, LANES), lambda b, c: (0, 0))],
        out_specs=[rows(V_M_COLS, 0),
                   pl.BlockSpec((None, H_M, DK_M, DV_M), lambda b, c: (b, 0, 0, 0)),
                   pl.BlockSpec((None, H_M, DK_M), lambda b, c: (b, 0, 0)),
                   pl.BlockSpec((None, 1, H_M), lambda b, c: (b, 0, 0))],
        out_shape=[jax.ShapeDtypeStruct((M_ROWS, V_M_COLS), BF16),
                   jax.ShapeDtypeStruct((BATCH, H_M, DK_M, DV_M), F32),
                   jax.ShapeDtypeStruct((BATCH, H_M, DK_M), F32),
                   jax.ShapeDtypeStruct((BATCH, 1, H_M), F32)],
        compiler_params=_params("parallel", "arbitrary"),
        name="mlstm_prompt",
    )(qkv_m, qkv_m, qkv_m, sig_o, igfg, gate_bias)


def _mlstm_sample_body(q_ref, qt_ref, k_ref, kt_ref, v_ref, so_ref, ig_ref, fg_ref, m0_ref,
                       c0_ref, n0_ref, h_ref, c_ref, n_ref, m_ref):
    for h in range(H_M):
        ig = ig_ref[h:h + 1, 0:1]
        lf = _log_sigmoid(fg_ref[h:h + 1, 0:1])
        m_prev = m0_ref[h:h + 1, 0:1]
        m_inter = lf + m_prev
        m_t = jnp.maximum(m_inter, ig)
        w_inter = jnp.exp(m_inter - m_t)
        q_row = q_ref[h:h + 1, :]
        k_row = k_ref[h:h + 1, :] * (DK_M ** -0.5)
        q_col = qt_ref[:, h:h + 1]
        k_col = kt_ref[:, h:h + 1] * (DK_M ** -0.5)
        v_row = v_ref[h:h + 1, :]
        c_old = c0_ref[h]
        n_old = n0_ref[h:h + 1, :]
        a = jnp.sum(q_row * k_row, axis=1, keepdims=True) * jnp.exp(ig - m_t)
        num = a * v_row + w_inter * jnp.sum(q_col * c_old, axis=0, keepdims=True)
        den = a + w_inter * jnp.sum(q_row * n_old, axis=1, keepdims=True)
        hh = num / jnp.maximum(jnp.abs(den), jnp.exp(-m_t))
        h_ref[h:h + 1, :] = so_ref[h:h + 1, :] * hh
        w_end = jnp.exp(ig - m_t)
        decay = jnp.exp(m_inter - m_t)
        c_ref[h] = decay * c_old + (k_col * w_end) * v_row
        n_ref[h:h + 1, :] = decay * n_old + k_row * w_end
        m_ref[h:h + 1, :] = jnp.broadcast_to(m_t, (1, LANES))


def _mlstm_sample(q, k, v, so, ig, fg, c0, n0, m0):
    nb = q.shape[0]

    def lanes(a):
        return jnp.broadcast_to(a[:, :, None], (nb, H_M, LANES))

    def per_b(*shape):
        return pl.BlockSpec((None,) + shape, lambda b: (b,) + (0,) * len(shape))

    return pl.pallas_call(
        _mlstm_sample_body,
        grid=(nb,),
        in_specs=[per_b(H_M, DK_M), per_b(DK_M, H_M), per_b(H_M, DK_M), per_b(DK_M, H_M),
                  per_b(H_M, DV_M), per_b(H_M, DV_M), per_b(H_M, LANES), per_b(H_M, LANES),
                  per_b(H_M, LANES), per_b(H_M, DK_M, DV_M), per_b(H_M, DK_M)],
        out_specs=[per_b(H_M, DV_M), per_b(H_M, DK_M, DV_M), per_b(H_M, DK_M), per_b(H_M, LANES)],
        out_shape=[jax.ShapeDtypeStruct((nb, H_M, DV_M), F32),
                   jax.ShapeDtypeStruct((nb, H_M, DK_M, DV_M), F32),
                   jax.ShapeDtypeStruct((nb, H_M, DK_M), F32),
                   jax.ShapeDtypeStruct((nb, H_M, LANES), F32)],
        compiler_params=_params("parallel"),
        name="mlstm_sample",
    )(q, jnp.swapaxes(q, 1, 2), k, jnp.swapaxes(k, 1, 2), v, so, lanes(ig), lanes(fg), lanes(m0),
      c0, n0)


def _alibi_slopes():
    n = N_GROUPS * H_A
    e = np.arange(1, n + 1, dtype=np.float64) * (-ALIBI_MAX_EXP / n)
    return np.exp2(e).reshape(N_GROUPS, H_A)


def _softmax_mix(lses, outs):
    m = jnp.maximum(jnp.maximum(lses[0], lses[1]), lses[2])
    e = [jnp.exp(x - m) for x in lses]
    z = e[0] + e[1] + e[2]
    acc = (e[0] / z) * outs[0]
    for g in (1, 2):
        acc = acc + (e[g] / z) * outs[g]
    return acc


def _attn_prompt_body(slopes_ref, *refs):
    G = N_GROUPS
    q_refs, kc_refs, kp_refs = refs[0:G], refs[G:2 * G], refs[2 * G:3 * G]
    vc_refs, vp_refs = refs[3 * G:4 * G], refs[4 * G:5 * G]
    o_ref = refs[5 * G]
    o_scr, l_scr = refs[5 * G + 1:6 * G + 1], refs[6 * G + 1:7 * G + 1]
    L = Q_BLOCK
    head = pl.program_id(1)
    first_key = jnp.where(pl.program_id(2) == 0, L, 0)
    t_idx = lax.broadcasted_iota(jnp.int32, (L, 2 * L), 0)
    s_idx = lax.broadcasted_iota(jnp.int32, (L, 2 * L), 1)
    step = t_idx + L - s_idx
    in_window = jnp.logical_and(step >= 0, step <= L)
    in_window_first = jnp.logical_and(in_window, s_idx >= first_key)
    nt = (((1,), (1,)), ((), ()))

    def rows(start, size, dil):
        return pl.ds(start, size) if dil == 1 else pl.ds(start, size, stride=dil)

    for g in range(G):
        dil = DILATIONS[g]
        bias = step.astype(F32) * (slopes_ref[g, head] * dil)

        def sub_block(j, r, g=g, dil=dil, bias=bias):
            start = j * L * dil + r
            q = q_refs[g][rows(start, L, dil), :].astype(BF16)
            if j == 0:
                k2 = jnp.concatenate([kp_refs[g][rows(r, L, dil), :], kc_refs[g][rows(start, L, dil), :]], axis=0)
                v2 = jnp.concatenate([vp_refs[g][rows(r, L, dil), :], vc_refs[g][rows(start, L, dil), :]], axis=0)
                mask = in_window_first
            else:
                k2 = kc_refs[g][rows(start - L * dil, 2 * L, dil), :]
                v2 = vc_refs[g][rows(start - L * dil, 2 * L, dil), :]
                mask = in_window
            s = lax.dot_general(q, k2.astype(BF16), nt, preferred_element_type=F32)
            s = jnp.where(mask, s * (DH_A ** -0.5) - bias, NEG_INF)
            m = jnp.max(s, axis=1, keepdims=True)
            p = jnp.exp(s - m)
            l = jnp.sum(p, axis=1, keepdims=True)
            o = jnp.dot(p.astype(BF16), v2.astype(BF16), preferred_element_type=F32) / l
            o_scr[g][rows(start, L, dil), :] = o
            l_scr[g][rows(start, L, dil), :] = jnp.broadcast_to(m + jnp.log(l), (L, LANES))

        for j in range(ATT_SB // (L * dil)):
            if dil == 1:
                sub_block(j, 0)
            else:
                def residue(r, carry, j=j):
                    sub_block(j, r)
                    return carry
                lax.fori_loop(0, dil, residue, 0, unroll=4)

    chunk = 256
    for c in range(0, ATT_SB, chunk):
        lses = [l_scr[g][c:c + chunk, :] for g in range(G)]
        outs = [o_scr[g][c:c + chunk, :] for g in range(G)]
        o_ref[c:c + chunk, :] = _softmax_mix(lses, outs).astype(o_ref.dtype)


def _attn_prompt(qkv_a):
    ns = SEQ // ATT_SB
    L = Q_BLOCK
    heads_per_part = N_GROUPS * H_A

    def cur(part, g):
        return pl.BlockSpec((ATT_SB, DH_A), lambda b, h, i: (b * ns + i, part * heads_per_part + g * H_A + h))

    def prev(part, g):
        span = L * DILATIONS[g]
        f = ATT_SB // span
        return pl.BlockSpec((span, DH_A),
                            lambda b, h, i: (b * ns * f + jnp.maximum(i * f - 1, 0),
                                             part * heads_per_part + g * H_A + h))

    groups = range(N_GROUPS)
    in_specs = ([pl.BlockSpec(memory_space=pltpu.SMEM)]
                + [cur(0, g) for g in groups] + [cur(1, g) for g in groups] + [prev(1, g) for g in groups]
                + [cur(2, g) for g in groups] + [prev(2, g) for g in groups])
    slopes = jnp.asarray(_alibi_slopes(), F32)
    return pl.pallas_call(
        _attn_prompt_body,
        grid=(BATCH, H_A, ns),
        in_specs=in_specs,
        out_specs=pl.BlockSpec((ATT_SB, DH_A), lambda b, h, i: (b * ns + i, h)),
        out_shape=jax.ShapeDtypeStruct((M_ROWS, A_GROUP_COLS), BF16),
        scratch_shapes=[pltpu.VMEM((ATT_SB, DH_A), F32)] * (2 * N_GROUPS),
        compiler_params=_params("parallel", "parallel", "arbitrary"),
        name="attn_prompt",
    )(slopes, *([qkv_a] * (5 * N_GROUPS)))


def _attn_sample_body(slopes_ref, q_ref, kn_ref, vn_ref, c0_ref, c1_ref, c2_ref, o_ref):
    L = Q_BLOCK
    c_refs = (c0_ref, c1_ref, c2_ref)
    steps = (L - lax.broadcasted_iota(jnp.int32, (L, 1, 1), 0)).astype(F32)
    outs, lses = [], []
    for g in range(N_GROUPS):
        q = q_ref[g]
        slope = slopes_ref[g][:, 0:1]
        s = (jnp.sum(c_refs[g][:, 0] * q[None], axis=-1, keepdims=True) * (DH_A ** -0.5)
             - (slope * DILATIONS[g])[None] * steps)
        s_new = jnp.sum(q * kn_ref[g], axis=-1, keepdims=True) * (DH_A ** -0.5)
        m = jnp.maximum(jnp.max(s, axis=0), s_new)
        p = jnp.exp(s - m[None])
        p_new = jnp.exp(s_new - m)
        l = jnp.sum(p, axis=0) + p_new
        outs.append((jnp.sum(p * c_refs[g][:, 1], axis=0) + p_new * vn_ref[g]) / l)
        lses.append(m + jnp.log(l))
    o_ref[...] = _softmax_mix(lses, outs)


def _attn_sample(q, kn, vn, caches):
    nb = q.shape[0]
    small = pl.BlockSpec((None, N_GROUPS, H_A, DH_A), lambda b: (b, 0, 0, 0))
    cache_specs, cache_views = [], []
    for g, c in enumerate(caches):
        dil = DILATIONS[g]
        assert c.shape[1] == Q_BLOCK * dil
        cache_views.append(c.reshape(nb, Q_BLOCK, dil, 2, H_A, DH_A))
        cache_specs.append(pl.BlockSpec((None, Q_BLOCK, None, 2, H_A, DH_A), lambda b: (b, 0, 0, 0, 0, 0)))
    slopes = jnp.broadcast_to(jnp.asarray(_alibi_slopes(), F32)[:, :, None], (N_GROUPS, H_A, LANES))
    return pl.pallas_call(
        _attn_sample_body,
        grid=(nb,),
        in_specs=[pl.BlockSpec((N_GROUPS, H_A, LANES), lambda b: (0, 0, 0)), small, small, small] + cache_specs,
        out_specs=pl.BlockSpec((None, H_A, DH_A), lambda b: (b, 0, 0)),
        out_shape=jax.ShapeDtypeStruct((nb, H_A, DH_A), F32),
        compiler_params=_params("parallel"),
        name="attn_sample",
    )(slopes, q, kn, vn, *cache_views)


def _sigmoid_epilogue(prods, extras):
    return jax.nn.sigmoid(prods[0])


def _gate_epilogue(prods, extras):
    return jax.nn.sigmoid(prods[0] + extras[0])


def _plain_epilogue(prods, extras):
    return prods[0]


def _mix_epilogue(prods, extras):
    return extras[0].astype(F32) * prods[0] + extras[1].astype(F32) * prods[1]


def _residual_epilogue(prods, extras):
    return extras[0] + prods[0]


def _with_sample_rows(prompt_rows_array, sample_rows):
    nb, c = sample_rows.shape
    block = jnp.concatenate([sample_rows, jnp.zeros((SAMPLE_PAD - nb, c), sample_rows.dtype)], axis=0)
    return lax.dynamic_update_slice(prompt_rows_array, block.astype(prompt_rows_array.dtype), (N_PROMPT, 0))


def kernel(x_prompt, x_sample, state_mlstm_C, state_mlstm_n, state_mlstm_m, cache_win1_kv, cache_win2_kv, cache_win3_kv, g_ffn1, w_ffn1_in, w_ffn1_out, g_mix, w_in, b_gate, b_igate, b_fgate, w_mlstm_out, w_attn_out, w_out, g_ffn2, w_ffn2_in, w_ffn2_out, g_final):
    d = D_MODEL
    nb = DEC_BATCH
    xp, xs = x_prompt.reshape(N_PROMPT, d), x_sample.reshape(nb, d)

    w1_in, w1_out = w_ffn1_in[0].astype(BF16), w_ffn1_out[0].astype(BF16)
    w_in_nk = w_in[0].T
    w_m = _cast_rows(w_in_nk, 0, COL_IF)
    w_if = _cast_rows(w_in_nk, COL_IF, LANES)
    w_ag = _cast_rows(w_in_nk, COL_IF, 3 * A_COLS + 2 * d, shift=COL_Q_A - COL_IF)
    gate_bias = jnp.pad(jnp.concatenate([b_igate[0], b_fgate[0]]), (0, LANES - 2 * H_M)).reshape(1, LANES)

    xn = _rmsnorm_inputs(xp, xs, g_ffn1[0])
    y, w2_in, w2_out = _swiglu(xn, w1_in, w1_out,
                               side=[(w_ffn2_in[0], 512, 512), (w_ffn2_out[0], 256, 512)])
    x1, u = _add_rmsnorm_inputs(xp, xs, y, g_mix[0], 0.5)

    qkv_m, w_mo = _mm("proj_qkv_m", [u], [(0, w_m, 0, True)], [], COL_O_M, BF16, _plain_epilogue,
                      side=[(w_mlstm_out[0], 512, 512)])
    sig_o, w_ao = _mm("proj_o_m", [u], [(0, w_m, COL_O_M, True)], [], V_M_COLS, BF16, _sigmoid_epilogue,
                      side=[(w_attn_out[0], 256, 512)])
    igfg = _mm("proj_if", [u], [(0, w_if, 0, True)], [], LANES, F32, _plain_epilogue, tn=LANES)
    qkv_a = _mm("proj_qkv_a", [u], [(0, w_ag, 0, True)], [], 3 * A_COLS, F32, _plain_epilogue)
    gates, w_o = _mm("proj_gates", [u], [(0, w_ag, 3 * A_COLS, True)],
                     [(b_gate[0].reshape(1, 2 * d), "row", 0)], 2 * d, BF16, _gate_epilogue,
                     side=[(w_out[0], 512, 512)])

    hm, c_p, n_p, m_p = _mlstm_prompt(qkv_m, sig_o, igfg, gate_bias)
    qkv_s = qkv_m[N_PROMPT:N_PROMPT + nb].astype(F32)
    if_s = igfg[N_PROMPT:N_PROMPT + nb] + gate_bias
    hm_s, c_s, n_s, m_s = _mlstm_sample(
        qkv_s[:, :QK_M_COLS].reshape(nb, H_M, DK_M),
        qkv_s[:, QK_M_COLS:2 * QK_M_COLS].reshape(nb, H_M, DK_M),
        qkv_s[:, 2 * QK_M_COLS:].reshape(nb, H_M, DV_M),
        sig_o[N_PROMPT:N_PROMPT + nb].astype(F32).reshape(nb, H_M, DV_M),
        if_s[:, :H_M], if_s[:, H_M:2 * H_M],
        state_mlstm_C[0], state_mlstm_n[0], state_mlstm_m[0])
    hm = _with_sample_rows(hm, hm_s.reshape(nb, V_M_COLS))

    ha = _attn_prompt(qkv_a)
    qkv_as = qkv_a[N_PROMPT:N_PROMPT + nb].reshape(nb, 3, N_GROUPS, H_A, DH_A)
    ha_s = _attn_sample(qkv_as[:, 0], qkv_as[:, 1], qkv_as[:, 2],
                        (cache_win1_kv[0], cache_win2_kv[0], cache_win3_kv[0]))
    ha = _with_sample_rows(ha, ha_s.reshape(nb, A_GROUP_COLS))

    z = _mm("mix", [hm, ha], [(0, w_mo, 0, False), (1, w_ao, 0, False)],
            [(gates, "tile", 0), (gates, "tile", d)], d, BF16, _mix_epilogue)
    x2 = _mm("proj_out", [z], [(0, w_o, 0, False)], [(x1, "tile", 0)], d, F32, _residual_epilogue)

    xn2 = _rmsnorm(x2, g_ffn2[0], BF16)
    y2 = _swiglu(xn2, w2_in, w2_out)
    out_p, out_s = _add_final_norm(x2, y2, g_final, 0.5)

    win_p, win_s = [], []
    for g, w in enumerate(WINDOWS):
        n_last = min(w, SEQ)
        k0, v0 = A_COLS + g * A_GROUP_COLS, 2 * A_COLS + g * A_GROUP_COLS
        per_seq = []
        for b in range(BATCH):
            last = qkv_a[(b + 1) * SEQ - n_last:(b + 1) * SEQ]
            per_seq.append(jnp.stack([last[:, k0:k0 + A_GROUP_COLS], last[:, v0:v0 + A_GROUP_COLS]], axis=1))
        win_p.append(jnp.stack(per_seq).reshape(1, BATCH, n_last, 2, H_A, DH_A))
        win_s.append(qkv_as[:, 1:3, g].reshape(1, nb, 1, 2, H_A, DH_A))
    return (out_p.reshape(BATCH, SEQ, d), out_s[:nb].reshape(nb, 1, d),
            c_p[None], n_p[None], m_p.reshape(1, BATCH, H_M),
            c_s[None], n_s[None], m_s[:, :, 0][None],
            win_p[0], win_p[1], win_p[2], win_s[0], win_s[1], win_s[2])
```

```python
import functools

import numpy as np
import jax
import jax.numpy as jnp
from jax import lax
from jax.experimental import pallas as pl
from jax.experimental.pallas import tpu as pltpu

F32 = jnp.float32
BF16 = jnp.bfloat16

D_MODEL = 4096
BATCH = 2
SEQ = 4096
DEC_BATCH = 32
H_M = 8
DK_M = 256
DV_M = 512
MLSTM_CHUNK = 128
WINDOWS = (128, 512, 2048)
DILATIONS = (1, 4, 16)
N_GROUPS = 3
H_A = 8
DH_A = 128
Q_BLOCK = 128
ALIBI_MAX_EXP = 8.0
D_FF = 11008
EPS = 1e-6
NEG_INF = -1e30

QK_M_COLS = H_M * DK_M
V_M_COLS = H_M * DV_M
A_GROUP_COLS = H_A * DH_A
A_COLS = N_GROUPS * A_GROUP_COLS
COL_O_M = 2 * QK_M_COLS + V_M_COLS
COL_IF = COL_O_M + V_M_COLS
COL_Q_A = COL_IF + 2 * H_M
COL_GATE = COL_Q_A + 3 * A_COLS
LANES = 128

N_PROMPT = BATCH * SEQ
SAMPLE_PAD = 64
M_ROWS = N_PROMPT + SAMPLE_PAD

V7X_VMEM_BYTES = 64 * 1024 * 1024
VMEM_LIMIT_BYTES = V7X_VMEM_BYTES - 8 * 1024 * 1024

ROW_TILE = 192
OUT_ROW_TILE = 128
MM_TM = 1376
MM_TN = 512
FFN_TM = 1376
FFN_TF = 256
FFN_TN = 512
ATT_SB = 2048


def _params(*semantics):
    return pltpu.CompilerParams(dimension_semantics=semantics, vmem_limit_bytes=VMEM_LIMIT_BYTES)


def _rms(x, g):
    ms = jnp.mean(x * x, axis=-1, keepdims=True)
    return x * lax.rsqrt(ms + EPS) * g


def _rmsnorm_body(x_ref, g_ref, o_ref):
    o_ref[...] = _rms(x_ref[...], g_ref[...]).astype(o_ref.dtype)


def _rmsnorm(x, g, out_dtype):
    m, d = x.shape
    return pl.pallas_call(
        _rmsnorm_body,
        grid=(m // ROW_TILE,),
        in_specs=[pl.BlockSpec((ROW_TILE, d), lambda i: (i, 0)),
                  pl.BlockSpec((1, d), lambda i: (0, 0))],
        out_specs=pl.BlockSpec((ROW_TILE, d), lambda i: (i, 0)),
        out_shape=jax.ShapeDtypeStruct((m, d), out_dtype),
        compiler_params=_params("parallel"),
        name="rmsnorm",
    )(x, g.reshape(1, d))


def _input_tile(xp_ref, xs_ref):
    is_sample_tile = pl.program_id(0) == N_PROMPT // OUT_ROW_TILE
    pad = jnp.zeros((OUT_ROW_TILE - xs_ref.shape[0], xs_ref.shape[1]), F32)
    return jnp.where(is_sample_tile, jnp.concatenate([xs_ref[...], pad], axis=0), xp_ref[...])


def _input_specs(d, nb):
    n_p = N_PROMPT // OUT_ROW_TILE
    return [pl.BlockSpec((OUT_ROW_TILE, d), lambda i: (jnp.minimum(i, n_p - 1), 0)),
            pl.BlockSpec((nb, d), lambda i: (0, 0))]


def _rmsnorm_inputs_body(xp_ref, xs_ref, g_ref, o_ref):
    o_ref[...] = _rms(_input_tile(xp_ref, xs_ref), g_ref[...]).astype(o_ref.dtype)


def _rmsnorm_inputs(xp, xs, g):
    d = xp.shape[1]
    row = pl.BlockSpec((OUT_ROW_TILE, d), lambda i: (i, 0))
    return pl.pallas_call(
        _rmsnorm_inputs_body,
        grid=(N_PROMPT // OUT_ROW_TILE + 1,),
        in_specs=_input_specs(d, xs.shape[0]) + [pl.BlockSpec((1, d), lambda i: (0, 0))],
        out_specs=row,
        out_shape=jax.ShapeDtypeStruct((M_ROWS, d), BF16),
        compiler_params=_params("parallel"),
        name="rmsnorm_inputs",
    )(xp, xs, g.reshape(1, d))


def _add_rmsnorm_inputs_body(xp_ref, xs_ref, y_ref, g_ref, xo_ref, no_ref, *, scale):
    x = _input_tile(xp_ref, xs_ref) + scale * y_ref[...]
    xo_ref[...] = x
    no_ref[...] = _rms(x, g_ref[...]).astype(no_ref.dtype)


def _add_rmsnorm_inputs(xp, xs, y, g, scale):
    d = xp.shape[1]
    row = pl.BlockSpec((OUT_ROW_TILE, d), lambda i: (i, 0))
    return pl.pallas_call(
        functools.partial(_add_rmsnorm_inputs_body, scale=scale),
        grid=(N_PROMPT // OUT_ROW_TILE + 1,),
        in_specs=_input_specs(d, xs.shape[0]) + [row, pl.BlockSpec((1, d), lambda i: (0, 0))],
        out_specs=[row, row],
        out_shape=[jax.ShapeDtypeStruct((M_ROWS, d), F32), jax.ShapeDtypeStruct((M_ROWS, d), BF16)],
        compiler_params=_params("parallel"),
        name="add_rmsnorm_inputs",
    )(xp, xs, y, g.reshape(1, d))


def _add_final_norm_body(x_ref, y_ref, g_ref, op_ref, os_ref, *, scale):
    out = _rms(x_ref[...] + scale * y_ref[...], g_ref[...])
    is_sample_tile = pl.program_id(0) == N_PROMPT // OUT_ROW_TILE

    @pl.when(jnp.logical_not(is_sample_tile))
    def _():
        op_ref[...] = out

    @pl.when(is_sample_tile)
    def _():
        os_ref[...] = out[:SAMPLE_PAD]


def _add_final_norm(x, y, g, scale):
    m, d = x.shape
    n_p = N_PROMPT // OUT_ROW_TILE
    row = pl.BlockSpec((OUT_ROW_TILE, d), lambda i: (i, 0))
    return pl.pallas_call(
        functools.partial(_add_final_norm_body, scale=scale),
        grid=(n_p + 1,),
        in_specs=[row, row, pl.BlockSpec((1, d), lambda i: (0, 0))],
        out_specs=[pl.BlockSpec((OUT_ROW_TILE, d), lambda i: (jnp.minimum(i, n_p - 1), 0)),
                   pl.BlockSpec((SAMPLE_PAD, d), lambda i: (0, 0))],
        out_shape=[jax.ShapeDtypeStruct((N_PROMPT, d), F32), jax.ShapeDtypeStruct((SAMPLE_PAD, d), F32)],
        compiler_params=_params("arbitrary"),
        name="add_final_norm",
    )(x, y, g.reshape(1, d))


def _side_cast_specs(side, n_inner):
    in_specs, out_specs, out_shapes = [], [], []
    for src, rb, cb in side:
        r, c = src.shape
        n_cb = c // cb
        last = (r // rb) * n_cb - 1

        def index(i, j, n_cb=n_cb, last=last):
            blk = jnp.minimum(i * n_inner + j, last)
            return lax.div(blk, n_cb), lax.rem(blk, n_cb)

        in_specs.append(pl.BlockSpec((rb, cb), index))
        out_specs.append(pl.BlockSpec((rb, cb), index))
        out_shapes.append(jax.ShapeDtypeStruct((r, c), BF16))
    return in_specs, out_specs, out_shapes


def _side_cast(src_refs, dst_refs):
    for src_ref, dst_ref in zip(src_refs, dst_refs):
        dst_ref[...] = src_ref[...].astype(dst_ref.dtype)


def _mm_body(*refs, dots, n_a, n_extra, n_side, epilogue):
    a_refs = refs[:n_a]
    pos = n_a
    prods = []
    for ai, w_is_nk, shift in dots:
        if shift:
            w = jnp.concatenate([refs[pos][shift:, :].astype(BF16), refs[pos + 1][...].astype(BF16)], axis=0)
            pos += 2
        else:
            w = refs[pos][...].astype(BF16)
            pos += 1
        contract = (((1,), (1,)), ((), ())) if w_is_nk else (((1,), (0,)), ((), ()))
        prods.append(lax.dot_general(a_refs[ai][...], w, contract, preferred_element_type=F32))
    n_in = pos + n_extra
    e_refs = refs[pos:n_in]
    o_ref = refs[n_in + n_side]
    o_ref[...] = epilogue(prods, [e[...] for e in e_refs]).astype(o_ref.dtype)
    _side_cast(refs[n_in:n_in + n_side], refs[n_in + n_side + 1:])


def _mm(name, a_list, w_list, extras, n, out_dtype, epilogue, tm=MM_TM, tn=MM_TN, side=()):
    m = a_list[0].shape[0]
    in_specs = [pl.BlockSpec((tm, a.shape[1]), lambda i, j: (i, 0)) for a in a_list]
    operands = list(a_list)
    dots = []
    for ai, w, off, w_is_nk in w_list:
        shift = off % tn if w_is_nk else 0
        if w_is_nk:
            in_specs.append(pl.BlockSpec((tn, w.shape[1]), lambda i, j, o=off // tn: (j + o, 0)))
        else:
            in_specs.append(pl.BlockSpec((w.shape[0], tn), lambda i, j, o=off // tn: (0, j + o)))
        operands.append(w)
        if shift:
            in_specs.append(pl.BlockSpec((shift, w.shape[1]),
                                         lambda i, j, o=off // tn, r=tn // shift: ((j + o + 1) * r, 0)))
            operands.append(w)
        dots.append((ai, w_is_nk, shift))
    for arr, kind, off in extras:
        if kind == "tile":
            in_specs.append(pl.BlockSpec((tm, tn), lambda i, j, o=off // tn: (i, j + o)))
        else:
            in_specs.append(pl.BlockSpec((1, tn), lambda i, j, o=off // tn: (0, j + o)))
        operands.append(arr)
    side_in, side_out, side_shapes = _side_cast_specs(side, n // tn)
    body = functools.partial(_mm_body, dots=tuple(dots),
                             n_a=len(a_list), n_extra=len(extras), n_side=len(side), epilogue=epilogue)
    outs = pl.pallas_call(
        body,
        grid=(m // tm, n // tn),
        in_specs=in_specs + side_in,
        out_specs=[pl.BlockSpec((tm, tn), lambda i, j: (i, j))] + side_out,
        out_shape=[jax.ShapeDtypeStruct((m, n), out_dtype)] + side_shapes,
        compiler_params=_params("arbitrary", "arbitrary"),
        name=name,
    )(*operands, *[src for src, _, _ in side])
    return outs if side else outs[0]


def _ffn_body(x_ref, wg_ref, wu_ref, wo_ref, *rest):
    n_side = (len(rest) - 1) // 2
    o_ref = rest[n_side]
    _side_cast(rest[:n_side], rest[n_side + 1:])
    x = x_ref[...]
    g = jnp.dot(x, wg_ref[...], preferred_element_type=F32)
    u = jnp.dot(x, wu_ref[...], preferred_element_type=F32)
    h = (g * jax.nn.sigmoid(g) * u).astype(BF16)

    @pl.when(pl.program_id(1) == 0)
    def _():
        o_ref[...] = jnp.zeros_like(o_ref)

    for c in range(0, o_ref.shape[1], FFN_TN):
        o_ref[:, c:c + FFN_TN] += jnp.dot(h, wo_ref[:, c:c + FFN_TN], preferred_element_type=F32)


def _swiglu(xn, w_in, w_out, side=()):
    m, d = xn.shape
    nf = D_FF // FFN_TF
    side_in, side_out, side_shapes = _side_cast_specs(side, nf)
    outs = pl.pallas_call(
        _ffn_body,
        grid=(m // FFN_TM, nf),
        in_specs=[pl.BlockSpec((FFN_TM, d), lambda i, f: (i, 0), pipeline_mode=pl.Buffered(1)),
                  pl.BlockSpec((d, FFN_TF), lambda i, f: (0, f)),
                  pl.BlockSpec((d, FFN_TF), lambda i, f: (0, f + nf)),
                  pl.BlockSpec((FFN_TF, d), lambda i, f: (f, 0))] + side_in,
        out_specs=[pl.BlockSpec((FFN_TM, d), lambda i, f: (i, 0), pipeline_mode=pl.Buffered(1))] + side_out,
        out_shape=[jax.ShapeDtypeStruct((m, d), F32)] + side_shapes,
        compiler_params=_params("arbitrary", "arbitrary"),
        name="swiglu",
    )(xn, w_in, w_in, w_out, *[src for src, _, _ in side])
    return outs if side else outs[0]


def _log_sigmoid(x):
    return jnp.minimum(x, 0.0) - jnp.log1p(jnp.exp(-jnp.abs(x)))


def _mlstm_prompt_body(q_ref, k_ref, v_ref, so_ref, if_ref, bias_ref,
                       h_ref, c_ref, n_ref, m_ref):
    L = MLSTM_CHUNK

    @pl.when(pl.program_id(1) == 0)
    def _():
        c_ref[...] = jnp.zeros_like(c_ref)
        n_ref[...] = jnp.zeros_like(n_ref)
        m_ref[...] = jnp.zeros_like(m_ref)

    x = if_ref[...] + bias_ref[...]
    xt = x.T
    t_idx = lax.broadcasted_iota(jnp.int32, (L, L), 0)
    s_idx = lax.broadcasted_iota(jnp.int32, (L, L), 1)
    causal = s_idx <= t_idx
    for h in range(H_M):
        ig_col = x[:, h:h + 1]
        ig_row = xt[h:h + 1, :]
        lf_col = _log_sigmoid(x[:, H_M + h:H_M + h + 1])
        lf_row = _log_sigmoid(xt[H_M + h:H_M + h + 1, :])
        b_col = jnp.sum(jnp.where(causal, lf_row, 0.0), axis=1, keepdims=True)
        b_row = jnp.sum(jnp.where(t_idx <= s_idx, lf_col, 0.0), axis=0, keepdims=True)
        m_prev = m_ref[:, h:h + 1]
        d = jnp.where(causal, b_col - b_row + ig_row, -jnp.inf)
        m_inter = b_col + m_prev
        m_t = jnp.maximum(m_inter, jnp.max(d, axis=1, keepdims=True))
        w_inter = jnp.exp(m_inter - m_t)

        qh = q_ref[:, h * DK_M:(h + 1) * DK_M]
        kf = k_ref[:, h * DK_M:(h + 1) * DK_M].astype(F32) * (DK_M ** -0.5)
        vh = v_ref[:, h * DV_M:(h + 1) * DV_M]
        s = lax.dot_general(qh, kf.astype(BF16), (((1,), (1,)), ((), ())),
                            preferred_element_type=F32)
        a = s * jnp.exp(d - m_t)
        c_old = c_ref[h]
        n_old = n_ref[h:h + 1, :]
        num = (jnp.dot(a.astype(BF16), vh, preferred_element_type=F32)
               + w_inter * jnp.dot(qh, c_old.astype(BF16), preferred_element_type=F32))
        den = (jnp.sum(a, axis=1, keepdims=True)
               + w_inter * jnp.sum(qh.astype(F32) * n_old, axis=1, keepdims=True))
        hh = num / jnp.maximum(jnp.abs(den), jnp.exp(-m_t))
        gate = so_ref[:, h * DV_M:(h + 1) * DV_M].astype(F32)
        h_ref[:, h * DV_M:(h + 1) * DV_M] = (gate * hh).astype(h_ref.dtype)

        m_new = m_t[L - 1:L, :]
        b_last = b_col[L - 1:L, :]
        w_end = jnp.exp(b_last - b_col + ig_col - m_new)
        decay = jnp.exp(b_last + m_prev - m_new)
        kw = kf * w_end
        dc = jnp.dot(kw.T.astype(BF16), vh, preferred_element_type=F32)
        c_ref[h] = decay * c_old + dc
        n_ref[h:h + 1, :] = decay * n_old + jnp.sum(kw, axis=0, keepdims=True)
        m_ref[:, h:h + 1] = m_new


def _mlstm_prompt(qkv_m, sig_o, igfg, gate_bias):
    nc = SEQ // MLSTM_CHUNK
    L = MLSTM_CHUNK

    def rows(width, col):
        return pl.BlockSpec((L, width), lambda b, c: (b * nc + c, col))

    return pl.pallas_call(
        _mlstm_prompt_body,
        grid=(BATCH, nc),
        in_specs=[rows(QK_M_COLS, 0), rows(QK_M_COLS, 1), rows(V_M_COLS, 1), rows(V_M_COLS, 0),
                  rows(LANES, 0), pl.BlockSpec((1, LANES), lambda b, c: (0, 0))],
        out_specs=[rows(V_M_COLS, 0),
                   pl.BlockSpec((None, H_M, DK_M, DV_M), lambda b, c: (b, 0, 0, 0)),
                   pl.BlockSpec((None, H_M, DK_M), lambda b, c: (b, 0, 0)),
                   pl.BlockSpec((None, 1, H_M), lambda b, c: (b, 0, 0))],
        out_shape=[jax.ShapeDtypeStruct((M_ROWS, V_M_COLS), BF16),
                   jax.ShapeDtypeStruct((BATCH, H_M, DK_M, DV_M), F32),
                   jax.ShapeDtypeStruct((BATCH, H_M, DK_M), F32),
                   jax.ShapeDtypeStruct((BATCH, 1, H_M), F32)],
        compiler_params=_params("parallel", "arbitrary"),
        name="mlstm_prompt",
    )(qkv_m, qkv_m, qkv_m, sig_o, igfg, gate_bias)


def _mlstm_sample_body(q_ref, qt_ref, k_ref, kt_ref, v_ref, so_ref, ig_ref, fg_ref, m0_ref,
                       c0_ref, n0_ref, h_ref, c_ref, n_ref, m_ref):
    for h in range(H_M):
        ig = ig_ref[h:h + 1, 0:1]
        lf = _log_sigmoid(fg_ref[h:h + 1, 0:1])
        m_prev = m0_ref[h:h + 1, 0:1]
        m_inter = lf + m_prev
        m_t = jnp.maximum(m_inter, ig)
        w_inter = jnp.exp(m_inter - m_t)
        q_row = q_ref[h:h + 1, :]
        k_row = k_ref[h:h + 1, :] * (DK_M ** -0.5)
        q_col = qt_ref[:, h:h + 1]
        k_col = kt_ref[:, h:h + 1] * (DK_M ** -0.5)
        v_row = v_ref[h:h + 1, :]
        c_old = c0_ref[h]
        n_old = n0_ref[h:h + 1, :]
        a = jnp.sum(q_row * k_row, axis=1, keepdims=True) * jnp.exp(ig - m_t)
        num = a * v_row + w_inter * jnp.sum(q_col * c_old, axis=0, keepdims=True)
        den = a + w_inter * jnp.sum(q_row * n_old, axis=1, keepdims=True)
        hh = num / jnp.maximum(jnp.abs(den), jnp.exp(-m_t))
        h_ref[h:h + 1, :] = so_ref[h:h + 1, :] * hh
        w_end = jnp.exp(ig - m_t)
        decay = jnp.exp(m_inter - m_t)
        c_ref[h] = decay * c_old + (k_col * w_end) * v_row
        n_ref[h:h + 1, :] = decay * n_old + k_row * w_end
        m_ref[h:h + 1, :] = jnp.broadcast_to(m_t, (1, LANES))


def _mlstm_sample(q, k, v, so, ig, fg, c0, n0, m0):
    nb = q.shape[0]

    def lanes(a):
        return jnp.broadcast_to(a[:, :, None], (nb, H_M, LANES))

    def per_b(*shape):
        return pl.BlockSpec((None,) + shape, lambda b: (b,) + (0,) * len(shape))

    return pl.pallas_call(
        _mlstm_sample_body,
        grid=(nb,),
        in_specs=[per_b(H_M, DK_M), per_b(DK_M, H_M), per_b(H_M, DK_M), per_b(DK_M, H_M),
                  per_b(H_M, DV_M), per_b(H_M, DV_M), per_b(H_M, LANES), per_b(H_M, LANES),
                  per_b(H_M, LANES), per_b(H_M, DK_M, DV_M), per_b(H_M, DK_M)],
        out_specs=[per_b(H_M, DV_M), per_b(H_M, DK_M, DV_M), per_b(H_M, DK_M), per_b(H_M, LANES)],
        out_shape=[jax.ShapeDtypeStruct((nb, H_M, DV_M), F32),
                   jax.ShapeDtypeStruct((nb, H_M, DK_M, DV_M), F32),
                   jax.ShapeDtypeStruct((nb, H_M, DK_M), F32),
                   jax.ShapeDtypeStruct((nb, H_M, LANES), F32)],
        compiler_params=_params("parallel"),
        name="mlstm_sample",
    )(q, jnp.swapaxes(q, 1, 2), k, jnp.swapaxes(k, 1, 2), v, so, lanes(ig), lanes(fg), lanes(m0),
      c0, n0)


def _alibi_slopes():
    n = N_GROUPS * H_A
    e = np.arange(1, n + 1, dtype=np.float64) * (-ALIBI_MAX_EXP / n)
    return np.exp2(e).reshape(N_GROUPS, H_A)


def _softmax_mix(lses, outs):
    m = jnp.maximum(jnp.maximum(lses[0], lses[1]), lses[2])
    e = [jnp.exp(x - m) for x in lses]
    z = e[0] + e[1] + e[2]
    acc = (e[0] / z) * outs[0]
    for g in (1, 2):
        acc = acc + (e[g] / z) * outs[g]
    return acc


def _attn_prompt_body(slopes_ref, *refs):
    G = N_GROUPS
    q_refs, kc_refs, kp_refs = refs[0:G], refs[G:2 * G], refs[2 * G:3 * G]
    vc_refs, vp_refs = refs[3 * G:4 * G], refs[4 * G:5 * G]
    o_ref = refs[5 * G]
    o_scr, l_scr = refs[5 * G + 1:6 * G + 1], refs[6 * G + 1:7 * G + 1]
    L = Q_BLOCK
    head = pl.program_id(1)
    first_key = jnp.where(pl.program_id(2) == 0, L, 0)
    t_idx = lax.broadcasted_iota(jnp.int32, (L, 2 * L), 0)
    s_idx = lax.broadcasted_iota(jnp.int32, (L, 2 * L), 1)
    step = t_idx + L - s_idx
    in_window = jnp.logical_and(step >= 0, step <= L)
    in_window_first = jnp.logical_and(in_window, s_idx >= first_key)
    nt = (((1,), (1,)), ((), ()))

    def rows(start, size, dil):
        return pl.ds(start, size) if dil == 1 else pl.ds(start, size, stride=dil)

    for g in range(G):
        dil = DILATIONS[g]
        bias = step.astype(F32) * (slopes_ref[g, head] * dil)

        def sub_block(j, r, g=g, dil=dil, bias=bias):
            start = j * L * dil + r
            q = q_refs[g][rows(start, L, dil), :].astype(BF16)
            if j == 0:
                k2 = jnp.concatenate([kp_refs[g][rows(r, L, dil), :], kc_refs[g][rows(start, L, dil), :]], axis=0)
                v2 = jnp.concatenate([vp_refs[g][rows(r, L, dil), :], vc_refs[g][rows(start, L, dil), :]], axis=0)
                mask = in_window_first
            else:
                k2 = kc_refs[g][rows(start - L * dil, 2 * L, dil), :]
                v2 = vc_refs[g][rows(start - L * dil, 2 * L, dil), :]
                mask = in_window
            s = lax.dot_general(q, k2.astype(BF16), nt, preferred_element_type=F32)
            s = jnp.where(mask, s * (DH_A ** -0.5) - bias, NEG_INF)
            m = jnp.max(s, axis=1, keepdims=True)
            p = jnp.exp(s - m)
            l = jnp.sum(p, axis=1, keepdims=True)
            o = jnp.dot(p.astype(BF16), v2.astype(BF16), preferred_element_type=F32) / l
            o_scr[g][rows(start, L, dil), :] = o
            l_scr[g][rows(start, L, dil), :] = jnp.broadcast_to(m + jnp.log(l), (L, LANES))

        for j in range(ATT_SB // (L * dil)):
            if dil == 1:
                sub_block(j, 0)
            else:
                def residue(r, carry, j=j):
                    sub_block(j, r)
                    return carry
                lax.fori_loop(0, dil, residue, 0, unroll=4)

    chunk = 256
    for c in range(0, ATT_SB, chunk):
        lses = [l_scr[g][c:c + chunk, :] for g in range(G)]
        outs = [o_scr[g][c:c + chunk, :] for g in range(G)]
        o_ref[c:c + chunk, :] = _softmax_mix(lses, outs).astype(o_ref.dtype)


def _attn_prompt(qkv_a):
    ns = SEQ // ATT_SB
    L = Q_BLOCK
    heads_per_part = N_GROUPS * H_A

    def cur(part, g):
        return pl.BlockSpec((ATT_SB, DH_A), lambda b, h, i: (b * ns + i, part * heads_per_part + g * H_A + h))

    def prev(part, g):
        span = L * DILATIONS[g]
        f = ATT_SB // span
        return pl.BlockSpec((span, DH_A),
                            lambda b, h, i: (b * ns * f + jnp.maximum(i * f - 1, 0),
                                             part * heads_per_part + g * H_A + h))

    groups = range(N_GROUPS)
    in_specs = ([pl.BlockSpec(memory_space=pltpu.SMEM)]
                + [cur(0, g) for g in groups] + [cur(1, g) for g in groups] + [prev(1, g) for g in groups]
                + [cur(2, g) for g in groups] + [prev(2, g) for g in groups])
    slopes = jnp.asarray(_alibi_slopes(), F32)
    return pl.pallas_call(
        _attn_prompt_body,
        grid=(BATCH, H_A, ns),
        in_specs=in_specs,
        out_specs=pl.BlockSpec((ATT_SB, DH_A), lambda b, h, i: (b * ns + i, h)),
        out_shape=jax.ShapeDtypeStruct((M_ROWS, A_GROUP_COLS), BF16),
        scratch_shapes=[pltpu.VMEM((ATT_SB, DH_A), F32)] * (2 * N_GROUPS),
        compiler_params=_params("parallel", "parallel", "arbitrary"),
        name="attn_prompt",
    )(slopes, *([qkv_a] * (5 * N_GROUPS)))


def _attn_sample_body(slopes_ref, q_ref, kn_ref, vn_ref, c0_ref, c1_ref, c2_ref, o_ref):
    L = Q_BLOCK
    c_refs = (c0_ref, c1_ref, c2_ref)
    steps = (L - lax.broadcasted_iota(jnp.int32, (L, 1, 1), 0)).astype(F32)
    outs, lses = [], []
    for g in range(N_GROUPS):
        q = q_ref[g]
        slope = slopes_ref[g][:, 0:1]
        s = (jnp.sum(c_refs[g][:, 0] * q[None], axis=-1, keepdims=True) * (DH_A ** -0.5)
             - (slope * DILATIONS[g])[None] * steps)
        s_new = jnp.sum(q * kn_ref[g], axis=-1, keepdims=True) * (DH_A ** -0.5)
        m = jnp.maximum(jnp.max(s, axis=0), s_new)
        p = jnp.exp(s - m[None])
        p_new = jnp.exp(s_new - m)
        l = jnp.sum(p, axis=0) + p_new
        outs.append((jnp.sum(p * c_refs[g][:, 1], axis=0) + p_new * vn_ref[g]) / l)
        lses.append(m + jnp.log(l))
    o_ref[...] = _softmax_mix(lses, outs)


def _attn_sample(q, kn, vn, caches):
    nb = q.shape[0]
    small = pl.BlockSpec((None, N_GROUPS, H_A, DH_A), lambda b: (b, 0, 0, 0))
    cache_specs, cache_views = [], []
    for g, c in enumerate(caches):
        dil = DILATIONS[g]
        assert c.shape[1] == Q_BLOCK * dil
        cache_views.append(c.reshape(nb, Q_BLOCK, dil, 2, H_A, DH_A))
        cache_specs.append(pl.BlockSpec((None, Q_BLOCK, None, 2, H_A, DH_A), lambda b: (b, 0, 0, 0, 0, 0)))
    slopes = jnp.broadcast_to(jnp.asarray(_alibi_slopes(), F32)[:, :, None], (N_GROUPS, H_A, LANES))
    return pl.pallas_call(
        _attn_sample_body,
        grid=(nb,),
        in_specs=[pl.BlockSpec((N_GROUPS, H_A, LANES), lambda b: (0, 0, 0)), small, small, small] + cache_specs,
        out_specs=pl.BlockSpec((None, H_A, DH_A), lambda b: (b, 0, 0)),
        out_shape=jax.ShapeDtypeStruct((nb, H_A, DH_A), F32),
        compiler_params=_params("parallel"),
        name="attn_sample",
    )(slopes, q, kn, vn, *cache_views)


def _sigmoid_epilogue(prods, extras):
    return jax.nn.sigmoid(prods[0])


def _gate_epilogue(prods, extras):
    return jax.nn.sigmoid(prods[0] + extras[0])


def _plain_epilogue(prods, extras):
    return prods[0]


def _mix_epilogue(prods, extras):
    return extras[0].astype(F32) * prods[0] + extras[1].astype(F32) * prods[1]


def _residual_epilogue(prods, extras):
    return extras[0] + prods[0]


def _with_sample_rows(prompt_rows_array, sample_rows):
    nb, c = sample_rows.shape
    block = jnp.concatenate([sample_rows, jnp.zeros((SAMPLE_PAD - nb, c), sample_rows.dtype)], axis=0)
    return lax.dynamic_update_slice(prompt_rows_array, block.astype(prompt_rows_array.dtype), (N_PROMPT, 0))


def kernel(x_prompt, x_sample, state_mlstm_C, state_mlstm_n, state_mlstm_m, cache_win1_kv, cache_win2_kv, cache_win3_kv, g_ffn1, w_ffn1_in, w_ffn1_out, g_mix, w_in, b_gate, b_igate, b_fgate, w_mlstm_out, w_attn_out, w_out, g_ffn2, w_ffn2_in, w_ffn2_out, g_final):
    d = D_MODEL
    nb = DEC_BATCH
    xp, xs = x_prompt.reshape(N_PROMPT, d), x_sample.reshape(nb, d)

    w1_in, w1_out = w_ffn1_in[0].astype(BF16), w_ffn1_out[0].astype(BF16)
    w_in_nk = w_in[0].T
    gate_bias = jnp.pad(jnp.concatenate([b_igate[0], b_fgate[0]]), (0, LANES - 2 * H_M)).reshape(1, LANES)

    xn = _rmsnorm_inputs(xp, xs, g_ffn1[0])
    y, w2_in = _swiglu(xn, w1_in, w1_out, side=[(w_ffn2_in[0], 1024, 512)])
    x1, u = _add_rmsnorm_inputs(xp, xs, y, g_mix[0], 0.5)

    qkv_m, w_mo = _mm("proj_qkv_m", [u], [(0, w_in_nk, 0, True)], [], COL_O_M, BF16, _plain_epilogue,
                      side=[(w_mlstm_out[0], 512, 512)])
    sig_o, w_ao = _mm("proj_o_m", [u], [(0, w_in_nk, COL_O_M, True)], [], V_M_COLS, BF16, _sigmoid_epilogue,
                      side=[(w_attn_out[0], 256, 512)])
    igfg = _mm("proj_if", [u], [(0, w_in_nk, COL_IF, True)], [], LANES, F32, _plain_epilogue, tn=LANES)
    qkv_a, w2_out = _mm("proj_qkv_a", [u], [(0, w_in_nk, COL_Q_A, True)], [], 3 * A_COLS, F32, _plain_epilogue,
                        side=[(w_ffn2_out[0], 256, 2048)])
    gates, w_o = _mm("proj_gates", [u], [(0, w_in_nk, COL_GATE, True)],
                     [(b_gate[0].reshape(1, 2 * d), "row", 0)], 2 * d, BF16, _gate_epilogue,
                     side=[(w_out[0], 512, 512)])

    hm, c_p, n_p, m_p = _mlstm_prompt(qkv_m, sig_o, igfg, gate_bias)
    qkv_s = qkv_m[N_PROMPT:N_PROMPT + nb].astype(F32)
    if_s = igfg[N_PROMPT:N_PROMPT + nb] + gate_bias
    hm_s, c_s, n_s, m_s = _mlstm_sample(
        qkv_s[:, :QK_M_COLS].reshape(nb, H_M, DK_M),
        qkv_s[:, QK_M_COLS:2 * QK_M_COLS].reshape(nb, H_M, DK_M),
        qkv_s[:, 2 * QK_M_COLS:].reshape(nb, H_M, DV_M),
        sig_o[N_PROMPT:N_PROMPT + nb].astype(F32).reshape(nb, H_M, DV_M),
        if_s[:, :H_M], if_s[:, H_M:2 * H_M],
        state_mlstm_C[0], state_mlstm_n[0], state_mlstm_m[0])
    hm = _with_sample_rows(hm, hm_s.reshape(nb, V_M_COLS))

    ha = _attn_prompt(qkv_a)
    qkv_as = qkv_a[N_PROMPT:N_PROMPT + nb].reshape(nb, 3, N_GROUPS, H_A, DH_A)
    ha_s = _attn_sample(qkv_as[:, 0], qkv_as[:, 1], qkv_as[:, 2],
                        (cache_win1_kv[0], cache_win2_kv[0], cache_win3_kv[0]))
    ha = _with_sample_rows(ha, ha_s.reshape(nb, A_GROUP_COLS))

    z = _mm("mix", [hm, ha], [(0, w_mo, 0, False), (1, w_ao, 0, False)],
            [(gates, "tile", 0), (gates, "tile", d)], d, BF16, _mix_epilogue)
    x2 = _mm("proj_out", [z], [(0, w_o, 0, False)], [(x1, "tile", 0)], d, F32, _residual_epilogue)

    xn2 = _rmsnorm(x2, g_ffn2[0], BF16)
    y2 = _swiglu(xn2, w2_in, w2_out)
    out_p, out_s = _add_final_norm(x2, y2, g_final, 0.5)

    win_p, win_s = [], []
    for g, w in enumerate(WINDOWS):
        n_last = min(w, SEQ)
        k0, v0 = A_COLS + g * A_GROUP_COLS, 2 * A_COLS + g * A_GROUP_COLS
        per_seq = []
        for b in range(BATCH):
            last = qkv_a[(b + 1) * SEQ - n_last:(b + 1) * SEQ]
            per_seq.append(jnp.stack([last[:, k0:k0 + A_GROUP_COLS], last[:, v0:v0 + A_GROUP_COLS]], axis=1))
        win_p.append(jnp.stack(per_seq).reshape(1, BATCH, n_last, 2, H_A, DH_A))
        win_s.append(qkv_as[:, 1:3, g].reshape(1, nb, 1, 2, H_A, DH_A))
    return (out_p.reshape(BATCH, SEQ, d), out_s[:nb].reshape(nb, 1, d),
            c_p[None], n_p[None], m_p.reshape(1, BATCH, H_M),
            c_s[None], n_s[None], m_s[:, :, 0][None],
            win_p[0], win_p[1], win_p[2], win_s[0], win_s[1], win_s[2])
```

```python
import functools

import numpy as np
import jax
import jax.numpy as jnp
from jax import lax
from jax.experimental import pallas as pl
from jax.experimental.pallas import tpu as pltpu

F32 = jnp.float32
BF16 = jnp.bfloat16

D_MODEL = 4096
BATCH = 2
SEQ = 4096
DEC_BATCH = 32
H_M = 8
DK_M = 256
DV_M = 512
MLSTM_CHUNK = 128
WINDOWS = (128, 512, 2048)
DILATIONS = (1, 4, 16)
N_GROUPS = 3
H_A = 8
DH_A = 128
Q_BLOCK = 128
ALIBI_MAX_EXP = 8.0
D_FF = 11008
EPS = 1e-6
NEG_INF = -1e30

QK_M_COLS = H_M * DK_M
V_M_COLS = H_M * DV_M
A_GROUP_COLS = H_A * DH_A
A_COLS = N_GROUPS * A_GROUP_COLS
COL_O_M = 2 * QK_M_COLS + V_M_COLS
COL_IF = COL_O_M + V_M_COLS
COL_Q_A = COL_IF + 2 * H_M
COL_GATE = COL_Q_A + 3 * A_COLS
LANES = 128

N_PROMPT = BATCH * SEQ
SAMPLE_PAD = 64
M_ROWS = N_PROMPT + SAMPLE_PAD

V7X_VMEM_BYTES = 64 * 1024 * 1024
VMEM_LIMIT_BYTES = V7X_VMEM_BYTES - 8 * 1024 * 1024
FFN_VMEM_LIMIT_BYTES = V7X_VMEM_BYTES - 2 * 1024 * 1024

ROW_TILE = 192
OUT_ROW_TILE = 128
MM_TM = 1376
MM_TN = 512
FFN_TM = 1376
FFN_TF = 256
FFN_TN = 512
ATT_SB = 2048


def _params(*semantics, vmem_limit_bytes=VMEM_LIMIT_BYTES):
    return pltpu.CompilerParams(dimension_semantics=semantics, vmem_limit_bytes=vmem_limit_bytes)


def _rms(x, g):
    ms = jnp.mean(x * x, axis=-1, keepdims=True)
    return x * lax.rsqrt(ms + EPS) * g


def _rmsnorm_body(x_ref, g_ref, o_ref):
    o_ref[...] = _rms(x_ref[...], g_ref[...]).astype(o_ref.dtype)


def _rmsnorm(x, g, out_dtype):
    m, d = x.shape
    return pl.pallas_call(
        _rmsnorm_body,
        grid=(m // ROW_TILE,),
        in_specs=[pl.BlockSpec((ROW_TILE, d), lambda i: (i, 0)),
                  pl.BlockSpec((1, d), lambda i: (0, 0))],
        out_specs=pl.BlockSpec((ROW_TILE, d), lambda i: (i, 0)),
        out_shape=jax.ShapeDtypeStruct((m, d), out_dtype),
        compiler_params=_params("parallel"),
        name="rmsnorm",
    )(x, g.reshape(1, d))


def _input_tile(xp_ref, xs_ref):
    is_sample_tile = pl.program_id(0) == N_PROMPT // OUT_ROW_TILE
    pad = jnp.zeros((OUT_ROW_TILE - xs_ref.shape[0], xs_ref.shape[1]), F32)
    return jnp.where(is_sample_tile, jnp.concatenate([xs_ref[...], pad], axis=0), xp_ref[...])


def _input_specs(d, nb):
    n_p = N_PROMPT // OUT_ROW_TILE
    return [pl.BlockSpec((OUT_ROW_TILE, d), lambda i: (jnp.minimum(i, n_p - 1), 0)),
            pl.BlockSpec((nb, d), lambda i: (0, 0))]


def _rmsnorm_inputs_body(xp_ref, xs_ref, g_ref, o_ref):
    o_ref[...] = _rms(_input_tile(xp_ref, xs_ref), g_ref[...]).astype(o_ref.dtype)


def _rmsnorm_inputs(xp, xs, g):
    d = xp.shape[1]
    row = pl.BlockSpec((OUT_ROW_TILE, d), lambda i: (i, 0))
    return pl.pallas_call(
        _rmsnorm_inputs_body,
        grid=(N_PROMPT // OUT_ROW_TILE + 1,),
        in_specs=_input_specs(d, xs.shape[0]) + [pl.BlockSpec((1, d), lambda i: (0, 0))],
        out_specs=row,
        out_shape=jax.ShapeDtypeStruct((M_ROWS, d), BF16),
        compiler_params=_params("parallel"),
        name="rmsnorm_inputs",
    )(xp, xs, g.reshape(1, d))


def _add_rmsnorm_inputs_body(xp_ref, xs_ref, y_ref, g_ref, xo_ref, no_ref, *, scale):
    x = _input_tile(xp_ref, xs_ref) + scale * y_ref[...]
    xo_ref[...] = x
    no_ref[...] = _rms(x, g_ref[...]).astype(no_ref.dtype)


def _add_rmsnorm_inputs(xp, xs, y, g, scale):
    d = xp.shape[1]
    row = pl.BlockSpec((OUT_ROW_TILE, d), lambda i: (i, 0))
    return pl.pallas_call(
        functools.partial(_add_rmsnorm_inputs_body, scale=scale),
        grid=(N_PROMPT // OUT_ROW_TILE + 1,),
        in_specs=_input_specs(d, xs.shape[0]) + [row, pl.BlockSpec((1, d), lambda i: (0, 0))],
        out_specs=[row, row],
        out_shape=[jax.ShapeDtypeStruct((M_ROWS, d), F32), jax.ShapeDtypeStruct((M_ROWS, d), BF16)],
        compiler_params=_params("parallel"),
        name="add_rmsnorm_inputs",
    )(xp, xs, y, g.reshape(1, d))


def _add_final_norm_body(x_ref, y_ref, g_ref, op_ref, os_ref, *, scale):
    out = _rms(x_ref[...] + scale * y_ref[...], g_ref[...])
    is_sample_tile = pl.program_id(0) == N_PROMPT // OUT_ROW_TILE

    @pl.when(jnp.logical_not(is_sample_tile))
    def _():
        op_ref[...] = out

    @pl.when(is_sample_tile)
    def _():
        os_ref[...] = out[:SAMPLE_PAD]


def _add_final_norm(x, y, g, scale):
    m, d = x.shape
    n_p = N_PROMPT // OUT_ROW_TILE
    row = pl.BlockSpec((OUT_ROW_TILE, d), lambda i: (i, 0))
    return pl.pallas_call(
        functools.partial(_add_final_norm_body, scale=scale),
        grid=(n_p + 1,),
        in_specs=[row, row, pl.BlockSpec((1, d), lambda i: (0, 0))],
        out_specs=[pl.BlockSpec((OUT_ROW_TILE, d), lambda i: (jnp.minimum(i, n_p - 1), 0)),
                   pl.BlockSpec((SAMPLE_PAD, d), lambda i: (0, 0))],
        out_shape=[jax.ShapeDtypeStruct((N_PROMPT, d), F32), jax.ShapeDtypeStruct((SAMPLE_PAD, d), F32)],
        compiler_params=_params("arbitrary"),
        name="add_final_norm",
    )(x, y, g.reshape(1, d))


def _side_cast_specs(side, n_inner):
    in_specs, out_specs, out_shapes = [], [], []
    for src, rb, cb in side:
        r, c = src.shape
        n_cb = c // cb
        last = (r // rb) * n_cb - 1

        def index(i, j, n_cb=n_cb, last=last):
            blk = jnp.minimum(i * n_inner + j, last)
            return lax.div(blk, n_cb), lax.rem(blk, n_cb)

        in_specs.append(pl.BlockSpec((rb, cb), index))
        out_specs.append(pl.BlockSpec((rb, cb), index))
        out_shapes.append(jax.ShapeDtypeStruct((r, c), BF16))
    return in_specs, out_specs, out_shapes


def _side_cast(src_refs, dst_refs):
    for src_ref, dst_ref in zip(src_refs, dst_refs):
        dst_ref[...] = src_ref[...].astype(dst_ref.dtype)


def _mm_body(*refs, dots, n_a, n_extra, n_side, epilogue):
    a_refs = refs[:n_a]
    pos = n_a
    prods = []
    for ai, w_is_nk, shift in dots:
        if shift:
            w = jnp.concatenate([refs[pos][shift:, :].astype(BF16), refs[pos + 1][...].astype(BF16)], axis=0)
            pos += 2
        else:
            w = refs[pos][...].astype(BF16)
            pos += 1
        contract = (((1,), (1,)), ((), ())) if w_is_nk else (((1,), (0,)), ((), ()))
        prods.append(lax.dot_general(a_refs[ai][...], w, contract, preferred_element_type=F32))
    n_in = pos + n_extra
    e_refs = refs[pos:n_in]
    o_ref = refs[n_in + n_side]
    o_ref[...] = epilogue(prods, [e[...] for e in e_refs]).astype(o_ref.dtype)
    _side_cast(refs[n_in:n_in + n_side], refs[n_in + n_side + 1:])


def _mm(name, a_list, w_list, extras, n, out_dtype, epilogue, tm=MM_TM, tn=MM_TN, side=()):
    m = a_list[0].shape[0]
    in_specs = [pl.BlockSpec((tm, a.shape[1]), lambda i, j: (i, 0)) for a in a_list]
    operands = list(a_list)
    dots = []
    for ai, w, off, w_is_nk in w_list:
        shift = off % tn if w_is_nk else 0
        if w_is_nk:
            in_specs.append(pl.BlockSpec((tn, w.shape[1]), lambda i, j, o=off // tn: (j + o, 0)))
        else:
            in_specs.append(pl.BlockSpec((w.shape[0], tn), lambda i, j, o=off // tn: (0, j + o)))
        operands.append(w)
        if shift:
            in_specs.append(pl.BlockSpec((shift, w.shape[1]),
                                         lambda i, j, o=off // tn, r=tn // shift: ((j + o + 1) * r, 0)))
            operands.append(w)
        dots.append((ai, w_is_nk, shift))
    for arr, kind, off in extras:
        if kind == "tile":
            in_specs.append(pl.BlockSpec((tm, tn), lambda i, j, o=off // tn: (i, j + o)))
        else:
            in_specs.append(pl.BlockSpec((1, tn), lambda i, j, o=off // tn: (0, j + o)))
        operands.append(arr)
    side_in, side_out, side_shapes = _side_cast_specs(side, n // tn)
    body = functools.partial(_mm_body, dots=tuple(dots),
                             n_a=len(a_list), n_extra=len(extras), n_side=len(side), epilogue=epilogue)
    outs = pl.pallas_call(
        body,
        grid=(m // tm, n // tn),
        in_specs=in_specs + side_in,
        out_specs=[pl.BlockSpec((tm, tn), lambda i, j: (i, j))] + side_out,
        out_shape=[jax.ShapeDtypeStruct((m, n), out_dtype)] + side_shapes,
        compiler_params=_params("arbitrary", "arbitrary"),
        name=name,
    )(*operands, *[src for src, _, _ in side])
    return outs if side else outs[0]


def _ffn_body(x_ref, wg_ref, wu_ref, wo_ref, *rest):
    n_side = (len(rest) - 1) // 2
    o_ref = rest[n_side]
    _side_cast(rest[:n_side], rest[n_side + 1:])
    x = x_ref[...]
    g = jnp.dot(x, wg_ref[...].astype(BF16), preferred_element_type=F32)
    u = jnp.dot(x, wu_ref[...].astype(BF16), preferred_element_type=F32)
    h = (g * jax.nn.sigmoid(g) * u).astype(BF16)

    @pl.when(pl.program_id(1) == 0)
    def _():
        o_ref[...] = jnp.zeros_like(o_ref)

    for c in range(0, o_ref.shape[1], FFN_TN):
        o_ref[:, c:c + FFN_TN] += jnp.dot(h, wo_ref[:, c:c + FFN_TN].astype(BF16),
                                          preferred_element_type=F32)


def _swiglu(xn, w_in, w_out, side=()):
    m, d = xn.shape
    nf = D_FF // FFN_TF
    side_in, side_out, side_shapes = _side_cast_specs(side, nf)
    outs = pl.pallas_call(
        _ffn_body,
        grid=(m // FFN_TM, nf),
        in_specs=[pl.BlockSpec((FFN_TM, d), lambda i, f: (i, 0), pipeline_mode=pl.Buffered(1)),
                  pl.BlockSpec((d, FFN_TF), lambda i, f: (0, f)),
                  pl.BlockSpec((d, FFN_TF), lambda i, f: (0, f + nf)),
                  pl.BlockSpec((FFN_TF, d), lambda i, f: (f, 0))] + side_in,
        out_specs=[pl.BlockSpec((FFN_TM, d), lambda i, f: (i, 0), pipeline_mode=pl.Buffered(1))] + side_out,
        out_shape=[jax.ShapeDtypeStruct((m, d), F32)] + side_shapes,
        compiler_params=_params("arbitrary", "arbitrary", vmem_limit_bytes=FFN_VMEM_LIMIT_BYTES),
        name="swiglu",
    )(xn, w_in, w_in, w_out, *[src for src, _, _ in side])
    return outs if side else outs[0]


def _log_sigmoid(x):
    return jnp.minimum(x, 0.0) - jnp.log1p(jnp.exp(-jnp.abs(x)))


def _mlstm_prompt_body(q_ref, k_ref, v_ref, so_ref, if_ref, bias_ref,
                       h_ref, c_ref, n_ref, m_ref):
    L = MLSTM_CHUNK

    @pl.when(pl.program_id(1) == 0)
    def _():
        c_ref[...] = jnp.zeros_like(c_ref)
        n_ref[...] = jnp.zeros_like(n_ref)
        m_ref[...] = jnp.zeros_like(m_ref)

    x = if_ref[...] + bias_ref[...]
    xt = x.T
    t_idx = lax.broadcasted_iota(jnp.int32, (L, L), 0)
    s_idx = lax.broadcasted_iota(jnp.int32, (L, L), 1)
    causal = s_idx <= t_idx
    for h in range(H_M):
        ig_col = x[:, h:h + 1]
        ig_row = xt[h:h + 1, :]
        lf_col = _log_sigmoid(x[:, H_M + h:H_M + h + 1])
        lf_row = _log_sigmoid(xt[H_M + h:H_M + h + 1, :])
        b_col = jnp.sum(jnp.where(causal, lf_row, 0.0), axis=1, keepdims=True)
        b_row = jnp.sum(jnp.where(t_idx <= s_idx, lf_col, 0.0), axis=0, keepdims=True)
        m_prev = m_ref[:, h:h + 1]
        d = jnp.where(causal, b_col - b_row + ig_row, -jnp.inf)
        m_inter = b_col + m_prev
        m_t = jnp.maximum(m_inter, jnp.max(d, axis=1, keepdims=True))
        w_inter = jnp.exp(m_inter - m_t)

        qh = q_ref[:, h * DK_M:(h + 1) * DK_M]
        kf = k_ref[:, h * DK_M:(h + 1) * DK_M].astype(F32) * (DK_M ** -0.5)
        vh = v_ref[:, h * DV_M:(h + 1) * DV_M]
        s = lax.dot_general(qh, kf.astype(BF16), (((1,), (1,)), ((), ())),
                            preferred_element_type=F32)
        a = s * jnp.exp(d - m_t)
        c_old = c_ref[h]
        n_old = n_ref[h:h + 1, :]
        num = (jnp.dot(a.astype(BF16), vh, preferred_element_type=F32)
               + w_inter * jnp.dot(qh, c_old.astype(BF16), preferred_element_type=F32))
        den = (jnp.sum(a, axis=1, keepdims=True)
               + w_inter * jnp.sum(qh.astype(F32) * n_old, axis=1, keepdims=True))
        hh = num / jnp.maximum(jnp.abs(den), jnp.exp(-m_t))
        gate = so_ref[:, h * DV_M:(h + 1) * DV_M].astype(F32)
        h_ref[:, h * DV_M:(h + 1) * DV_M] = (gate * hh).astype(h_ref.dtype)

        m_new = m_t[L - 1:L, :]
        b_last = b_col[L - 1:L, :]
        w_end = jnp.exp(b_last - b_col + ig_col - m_new)
        decay = jnp.exp(b_last + m_prev - m_new)
        kw = kf * w_end
        dc = jnp.dot(kw.T.astype(BF16), vh, preferred_element_type=F32)
        c_ref[h] = decay * c_old + dc
        n_ref[h:h + 1, :] = decay * n_old + jnp.sum(kw, axis=0, keepdims=True)
        m_ref[:, h:h + 1] = m_new


def _mlstm_prompt(qkv_m, sig_o, igfg, gate_bias):
    nc = SEQ // MLSTM_CHUNK
    L = MLSTM_CHUNK

    def rows(width, col):
        return pl.BlockSpec((L, width), lambda b, c: (b * nc + c, col))

    return pl.pallas_call(
        _mlstm_prompt_body,
        grid=(BATCH, nc),
        in_specs=[rows(QK_M_COLS, 0), rows(QK_M_COLS, 1), rows(V_M_COLS, 1), rows(V_M_COLS, 0),
                  rows(LANES, 0), pl.BlockSpec((1, LANES), lambda b, c: (0, 0))],
        out_specs=[rows(V_M_COLS, 0),
                   pl.BlockSpec((None, H_M, DK_M, DV_M), lambda b, c: (b, 0, 0, 0)),
                   pl.BlockSpec((None, H_M, DK_M), lambda b, c: (b, 0, 0)),
                   pl.BlockSpec((None, 1, H_M), lambda b, c: (b, 0, 0))],
        out_shape=[jax.ShapeDtypeStruct((M_ROWS, V_M_COLS), BF16),
                   jax.ShapeDtypeStruct((BATCH, H_M, DK_M, DV_M), F32),
                   jax.ShapeDtypeStruct((BATCH, H_M, DK_M), F32),
                   jax.ShapeDtypeStruct((BATCH, 1, H_M), F32)],
        compiler_params=_params("parallel", "arbitrary"),
        name="mlstm_prompt",
    )(qkv_m, qkv_m, qkv_m, sig_o, igfg, gate_bias)


def _mlstm_sample_body(q_ref, qt_ref, k_ref, kt_ref, v_ref, so_ref, ig_ref, fg_ref, m0_ref,
                       c0_ref, n0_ref, h_ref, c_ref, n_ref, m_ref):
    for h in range(H_M):
        ig = ig_ref[h:h + 1, 0:1]
        lf = _log_sigmoid(fg_ref[h:h + 1, 0:1])
        m_prev = m0_ref[h:h + 1, 0:1]
        m_inter = lf + m_prev
        m_t = jnp.maximum(m_inter, ig)
        w_inter = jnp.exp(m_inter - m_t)
        q_row = q_ref[h:h + 1, :]
        k_row = k_ref[h:h + 1, :] * (DK_M ** -0.5)
        q_col = qt_ref[:, h:h + 1]
        k_col = kt_ref[:, h:h + 1] * (DK_M ** -0.5)
        v_row = v_ref[h:h + 1, :]
        c_old = c0_ref[h]
        n_old = n0_ref[h:h + 1, :]
        a = jnp.sum(q_row * k_row, axis=1, keepdims=True) * jnp.exp(ig - m_t)
        num = a * v_row + w_inter * jnp.sum(q_col * c_old, axis=0, keepdims=True)
        den = a + w_inter * jnp.sum(q_row * n_old, axis=1, keepdims=True)
        hh = num / jnp.maximum(jnp.abs(den), jnp.exp(-m_t))
        h_ref[h:h + 1, :] = so_ref[h:h + 1, :] * hh
        w_end = jnp.exp(ig - m_t)
        decay = jnp.exp(m_inter - m_t)
        c_ref[h] = decay * c_old + (k_col * w_end) * v_row
        n_ref[h:h + 1, :] = decay * n_old + k_row * w_end
        m_ref[h:h + 1, :] = jnp.broadcast_to(m_t, (1, LANES))


def _mlstm_sample(q, k, v, so, ig, fg, c0, n0, m0):
    nb = q.shape[0]

    def lanes(a):
        return jnp.broadcast_to(a[:, :, None], (nb, H_M, LANES))

    def per_b(*shape):
        return pl.BlockSpec((None,) + shape, lambda b: (b,) + (0,) * len(shape))

    return pl.pallas_call(
        _mlstm_sample_body,
        grid=(nb,),
        in_specs=[per_b(H_M, DK_M), per_b(DK_M, H_M), per_b(H_M, DK_M), per_b(DK_M, H_M),
                  per_b(H_M, DV_M), per_b(H_M, DV_M), per_b(H_M, LANES), per_b(H_M, LANES),
                  per_b(H_M, LANES), per_b(H_M, DK_M, DV_M), per_b(H_M, DK_M)],
        out_specs=[per_b(H_M, DV_M), per_b(H_M, DK_M, DV_M), per_b(H_M, DK_M), per_b(H_M, LANES)],
        out_shape=[jax.ShapeDtypeStruct((nb, H_M, DV_M), F32),
                   jax.ShapeDtypeStruct((nb, H_M, DK_M, DV_M), F32),
                   jax.ShapeDtypeStruct((nb, H_M, DK_M), F32),
                   jax.ShapeDtypeStruct((nb, H_M, LANES), F32)],
        compiler_params=_params("parallel"),
        name="mlstm_sample",
    )(q, jnp.swapaxes(q, 1, 2), k, jnp.swapaxes(k, 1, 2), v, so, lanes(ig), lanes(fg), lanes(m0),
      c0, n0)


def _alibi_slopes():
    n = N_GROUPS * H_A
    e = np.arange(1, n + 1, dtype=np.float64) * (-ALIBI_MAX_EXP / n)
    return np.exp2(e).reshape(N_GROUPS, H_A)


def _softmax_mix(lses, outs):
    m = jnp.maximum(jnp.maximum(lses[0], lses[1]), lses[2])
    e = [jnp.exp(x - m) for x in lses]
    z = e[0] + e[1] + e[2]
    acc = (e[0] / z) * outs[0]
    for g in (1, 2):
        acc = acc + (e[g] / z) * outs[g]
    return acc


def _attn_prompt_body(slopes_ref, *refs):
    G = N_GROUPS
    q_refs, kc_refs, kp_refs = refs[0:G], refs[G:2 * G], refs[2 * G:3 * G]
    vc_refs, vp_refs = refs[3 * G:4 * G], refs[4 * G:5 * G]
    o_ref = refs[5 * G]
    o_scr, l_scr = refs[5 * G + 1:6 * G + 1], refs[6 * G + 1:7 * G + 1]
    L = Q_BLOCK
    head = pl.program_id(1)
    first_key = jnp.where(pl.program_id(2) == 0, L, 0)
    t_idx = lax.broadcasted_iota(jnp.int32, (L, 2 * L), 0)
    s_idx = lax.broadcasted_iota(jnp.int32, (L, 2 * L), 1)
    step = t_idx + L - s_idx
    in_window = jnp.logical_and(step >= 0, step <= L)
    in_window_first = jnp.logical_and(in_window, s_idx >= first_key)
    nt = (((1,), (1,)), ((), ()))

    def rows(start, size, dil):
        return pl.ds(start, size) if dil == 1 else pl.ds(start, size, stride=dil)

    for g in range(G):
        dil = DILATIONS[g]
        bias = step.astype(F32) * (slopes_ref[g, head] * dil)

        def sub_block(j, r, g=g, dil=dil, bias=bias):
            start = j * L * dil + r
            q = q_refs[g][rows(start, L, dil), :].astype(BF16)
            if j == 0:
                k2 = jnp.concatenate([kp_refs[g][rows(r, L, dil), :], kc_refs[g][rows(start, L, dil), :]], axis=0)
                v2 = jnp.concatenate([vp_refs[g][rows(r, L, dil), :], vc_refs[g][rows(start, L, dil), :]], axis=0)
                mask = in_window_first
            else:
                k2 = kc_refs[g][rows(start - L * dil, 2 * L, dil), :]
                v2 = vc_refs[g][rows(start - L * dil, 2 * L, dil), :]
                mask = in_window
            s = lax.dot_general(q, k2.astype(BF16), nt, preferred_element_type=F32)
            s = jnp.where(mask, s * (DH_A ** -0.5) - bias, NEG_INF)
            m = jnp.max(s, axis=1, keepdims=True)
            p = jnp.exp(s - m)
            l = jnp.sum(p, axis=1, keepdims=True)
            o = jnp.dot(p.astype(BF16), v2.astype(BF16), preferred_element_type=F32) / l
            o_scr[g][rows(start, L, dil), :] = o
            l_scr[g][rows(start, L, dil), :] = jnp.broadcast_to(m + jnp.log(l), (L, LANES))

        for j in range(ATT_SB // (L * dil)):
            if dil == 1:
                sub_block(j, 0)
            else:
                def residue(r, carry, j=j):
                    sub_block(j, r)
                    return carry
                lax.fori_loop(0, dil, residue, 0, unroll=4)

    chunk = 256
    for c in range(0, ATT_SB, chunk):
        lses = [l_scr[g][c:c + chunk, :] for g in range(G)]
        outs = [o_scr[g][c:c + chunk, :] for g in range(G)]
        o_ref[c:c + chunk, :] = _softmax_mix(lses, outs).astype(o_ref.dtype)


def _attn_prompt(qkv_a):
    ns = SEQ // ATT_SB
    L = Q_BLOCK
    heads_per_part = N_GROUPS * H_A

    def cur(part, g):
        return pl.BlockSpec((ATT_SB, DH_A), lambda b, h, i: (b * ns + i, part * heads_per_part + g * H_A + h))

    def prev(part, g):
        span = L * DILATIONS[g]
        f = ATT_SB // span
        return pl.BlockSpec((span, DH_A),
                            lambda b, h, i: (b * ns * f + jnp.maximum(i * f - 1, 0),
                                             part * heads_per_part + g * H_A + h))

    groups = range(N_GROUPS)
    in_specs = ([pl.BlockSpec(memory_space=pltpu.SMEM)]
                + [cur(0, g) for g in groups] + [cur(1, g) for g in groups] + [prev(1, g) for g in groups]
                + [cur(2, g) for g in groups] + [prev(2, g) for g in groups])
    slopes = jnp.asarray(_alibi_slopes(), F32)
    return pl.pallas_call(
        _attn_prompt_body,
        grid=(BATCH, H_A, ns),
        in_specs=in_specs,
        out_specs=pl.BlockSpec((ATT_SB, DH_A), lambda b, h, i: (b * ns + i, h)),
        out_shape=jax.ShapeDtypeStruct((M_ROWS, A_GROUP_COLS), BF16),
        scratch_shapes=[pltpu.VMEM((ATT_SB, DH_A), F32)] * (2 * N_GROUPS),
        compiler_params=_params("parallel", "parallel", "arbitrary"),
        name="attn_prompt",
    )(slopes, *([qkv_a] * (5 * N_GROUPS)))


def _attn_sample_body(slopes_ref, q_ref, kn_ref, vn_ref, c0_ref, c1_ref, c2_ref, o_ref):
    L = Q_BLOCK
    c_refs = (c0_ref, c1_ref, c2_ref)
    steps = (L - lax.broadcasted_iota(jnp.int32, (L, 1, 1), 0)).astype(F32)
    outs, lses = [], []
    for g in range(N_GROUPS):
        q = q_ref[g]
        slope = slopes_ref[g][:, 0:1]
        s = (jnp.sum(c_refs[g][:, 0] * q[None], axis=-1, keepdims=True) * (DH_A ** -0.5)
             - (slope * DILATIONS[g])[None] * steps)
        s_new = jnp.sum(q * kn_ref[g], axis=-1, keepdims=True) * (DH_A ** -0.5)
        m = jnp.maximum(jnp.max(s, axis=0), s_new)
        p = jnp.exp(s - m[None])
        p_new = jnp.exp(s_new - m)
        l = jnp.sum(p, axis=0) + p_new
        outs.append((jnp.sum(p * c_refs[g][:, 1], axis=0) + p_new * vn_ref[g]) / l)
        lses.append(m + jnp.log(l))
    o_ref[...] = _softmax_mix(lses, outs)


def _attn_sample(q, kn, vn, caches):
    nb = q.shape[0]
    small = pl.BlockSpec((None, N_GROUPS, H_A, DH_A), lambda b: (b, 0, 0, 0))
    cache_specs, cache_views = [], []
    for g, c in enumerate(caches):
        dil = DILATIONS[g]
        assert c.shape[1] == Q_BLOCK * dil
        cache_views.append(c.reshape(nb, Q_BLOCK, dil, 2, H_A, DH_A))
        cache_specs.append(pl.BlockSpec((None, Q_BLOCK, None, 2, H_A, DH_A), lambda b: (b, 0, 0, 0, 0, 0)))
    slopes = jnp.broadcast_to(jnp.asarray(_alibi_slopes(), F32)[:, :, None], (N_GROUPS, H_A, LANES))
    return pl.pallas_call(
        _attn_sample_body,
        grid=(nb,),
        in_specs=[pl.BlockSpec((N_GROUPS, H_A, LANES), lambda b: (0, 0, 0)), small, small, small] + cache_specs,
        out_specs=pl.BlockSpec((None, H_A, DH_A), lambda b: (b, 0, 0)),
        out_shape=jax.ShapeDtypeStruct((nb, H_A, DH_A), F32),
        compiler_params=_params("parallel"),
        name="attn_sample",
    )(slopes, q, kn, vn, *cache_views)


def _sigmoid_epilogue(prods, extras):
    return jax.nn.sigmoid(prods[0])


def _gate_epilogue(prods, extras):
    return jax.nn.sigmoid(prods[0] + extras[0])


def _plain_epilogue(prods, extras):
    return prods[0]


def _mix_epilogue(prods, extras):
    return extras[0].astype(F32) * prods[0] + extras[1].astype(F32) * prods[1]


def _residual_epilogue(prods, extras):
    return extras[0] + prods[0]


def _with_sample_rows(prompt_rows_array, sample_rows):
    nb, c = sample_rows.shape
    block = jnp.concatenate([sample_rows, jnp.zeros((SAMPLE_PAD - nb, c), sample_rows.dtype)], axis=0)
    return lax.dynamic_update_slice(prompt_rows_array, block.astype(prompt_rows_array.dtype), (N_PROMPT, 0))


def kernel(x_prompt, x_sample, state_mlstm_C, state_mlstm_n, state_mlstm_m, cache_win1_kv, cache_win2_kv, cache_win3_kv, g_ffn1, w_ffn1_in, w_ffn1_out, g_mix, w_in, b_gate, b_igate, b_fgate, w_mlstm_out, w_attn_out, w_out, g_ffn2, w_ffn2_in, w_ffn2_out, g_final):
    d = D_MODEL
    nb = DEC_BATCH
    xp, xs = x_prompt.reshape(N_PROMPT, d), x_sample.reshape(nb, d)

    w_in_nk = w_in[0].T
    gate_bias = jnp.pad(jnp.concatenate([b_igate[0], b_fgate[0]]), (0, LANES - 2 * H_M)).reshape(1, LANES)

    xn = _rmsnorm_inputs(xp, xs, g_ffn1[0])
    y = _swiglu(xn, w_ffn1_in[0], w_ffn1_out[0])
    x1, u = _add_rmsnorm_inputs(xp, xs, y, g_mix[0], 0.5)

    qkv_m, w_mo = _mm("proj_qkv_m", [u], [(0, w_in_nk, 0, True)], [], COL_O_M, BF16, _plain_epilogue,
                      side=[(w_mlstm_out[0], 512, 512)])
    sig_o, w_ao = _mm("proj_o_m", [u], [(0, w_in_nk, COL_O_M, True)], [], V_M_COLS, BF16, _sigmoid_epilogue,
                      side=[(w_attn_out[0], 256, 512)])
    igfg = _mm("proj_if", [u], [(0, w_in_nk, COL_IF, True)], [], LANES, F32, _plain_epilogue, tn=LANES)
    qkv_a = _mm("proj_qkv_a", [u], [(0, w_in_nk, COL_Q_A, True)], [], 3 * A_COLS, F32, _plain_epilogue)
    gates, w_o = _mm("proj_gates", [u], [(0, w_in_nk, COL_GATE, True)],
                     [(b_gate[0].reshape(1, 2 * d), "row", 0)], 2 * d, BF16, _gate_epilogue,
                     side=[(w_out[0], 512, 512)])

    hm, c_p, n_p, m_p = _mlstm_prompt(qkv_m, sig_o, igfg, gate_bias)
    qkv_s = qkv_m[N_PROMPT:N_PROMPT + nb].astype(F32)
    if_s = igfg[N_PROMPT:N_PROMPT + nb] + gate_bias
    hm_s, c_s, n_s, m_s = _mlstm_sample(
        qkv_s[:, :QK_M_COLS].reshape(nb, H_M, DK_M),
        qkv_s[:, QK_M_COLS:2 * QK_M_COLS].reshape(nb, H_M, DK_M),
        qkv_s[:, 2 * QK_M_COLS:].reshape(nb, H_M, DV_M),
        sig_o[N_PROMPT:N_PROMPT + nb].astype(F32).reshape(nb, H_M, DV_M),
        if_s[:, :H_M], if_s[:, H_M:2 * H_M],
        state_mlstm_C[0], state_mlstm_n[0], state_mlstm_m[0])
    hm = _with_sample_rows(hm, hm_s.reshape(nb, V_M_COLS))

    ha = _attn_prompt(qkv_a)
    qkv_as = qkv_a[N_PROMPT:N_PROMPT + nb].reshape(nb, 3, N_GROUPS, H_A, DH_A)
    ha_s = _attn_sample(qkv_as[:, 0], qkv_as[:, 1], qkv_as[:, 2],
                        (cache_win1_kv[0], cache_win2_kv[0], cache_win3_kv[0]))
    ha = _with_sample_rows(ha, ha_s.reshape(nb, A_GROUP_COLS))

    z = _mm("mix", [hm, ha], [(0, w_mo, 0, False), (1, w_ao, 0, False)],
            [(gates, "tile", 0), (gates, "tile", d)], d, BF16, _mix_epilogue)
    x2 = _mm("proj_out", [z], [(0, w_o, 0, False)], [(x1, "tile", 0)], d, F32, _residual_epilogue)

    xn2 = _rmsnorm(x2, g_ffn2[0], BF16)
    y2 = _swiglu(xn2, w_ffn2_in[0], w_ffn2_out[0])
    out_p, out_s = _add_final_norm(x2, y2, g_final, 0.5)

    win_p, win_s = [], []
    for g, w in enumerate(WINDOWS):
        n_last = min(w, SEQ)
        k0, v0 = A_COLS + g * A_GROUP_COLS, 2 * A_COLS + g * A_GROUP_COLS
        per_seq = []
        for b in range(BATCH):
            last = qkv_a[(b + 1) * SEQ - n_last:(b + 1) * SEQ]
            per_seq.append(jnp.stack([last[:, k0:k0 + A_GROUP_COLS], last[:, v0:v0 + A_GROUP_COLS]], axis=1))
        win_p.append(jnp.stack(per_seq).reshape(1, BATCH, n_last, 2, H_A, DH_A))
        win_s.append(qkv_as[:, 1:3, g].reshape(1, nb, 1, 2, H_A, DH_A))
    return (out_p.reshape(BATCH, SEQ, d), out_s[:nb].reshape(nb, 1, d),
            c_p[None], n_p[None], m_p.reshape(1, BATCH, H_M),
            c_s[None], n_s[None], m_s[:, :, 0][None],
            win_p[0], win_p[1], win_p[2], win_s[0], win_s[1], win_s[2])
```

```python
import functools

import numpy as np
import jax
import jax.numpy as jnp
from jax import lax
from jax.experimental import pallas as pl
from jax.experimental.pallas import tpu as pltpu

F32 = jnp.float32
BF16 = jnp.bfloat16

D_MODEL = 4096
BATCH = 2
SEQ = 4096
DEC_BATCH = 32
H_M = 8
DK_M = 256
DV_M = 512
MLSTM_CHUNK = 128
WINDOWS = (128, 512, 2048)
DILATIONS = (1, 4, 16)
N_GROUPS = 3
H_A = 8
DH_A = 128
Q_BLOCK = 128
ALIBI_MAX_EXP = 8.0
D_FF = 11008
EPS = 1e-6
NEG_INF = -1e30

QK_M_COLS = H_M * DK_M
V_M_COLS = H_M * DV_M
A_GROUP_COLS = H_A * DH_A
A_COLS = N_GROUPS * A_GROUP_COLS
COL_O_M = 2 * QK_M_COLS + V_M_COLS
COL_IF = COL_O_M + V_M_COLS
COL_Q_A = COL_IF + 2 * H_M
COL_GATE = COL_Q_A + 3 * A_COLS
LANES = 128

N_PROMPT = BATCH * SEQ
SAMPLE_PAD = 64
M_ROWS = N_PROMPT + SAMPLE_PAD

V7X_VMEM_BYTES = 64 * 1024 * 1024
VMEM_LIMIT_BYTES = V7X_VMEM_BYTES - 8 * 1024 * 1024
FFN_VMEM_LIMIT_BYTES = V7X_VMEM_BYTES - 2 * 1024 * 1024

ROW_TILE = 192
OUT_ROW_TILE = 128
MM_TM = 1376
MM_TN = 512
FFN_TM = 1376
FFN_TF = 256
FFN_TN = 512
ATT_SB = 2048


def _params(*semantics, vmem_limit_bytes=VMEM_LIMIT_BYTES):
    return pltpu.CompilerParams(dimension_semantics=semantics, vmem_limit_bytes=vmem_limit_bytes)


def _rms(x, g):
    ms = jnp.mean(x * x, axis=-1, keepdims=True)
    return x * lax.rsqrt(ms + EPS) * g


def _rmsnorm_body(x_ref, g_ref, o_ref):
    o_ref[...] = _rms(x_ref[...], g_ref[...]).astype(o_ref.dtype)


def _rmsnorm(x, g, out_dtype):
    m, d = x.shape
    return pl.pallas_call(
        _rmsnorm_body,
        grid=(m // ROW_TILE,),
        in_specs=[pl.BlockSpec((ROW_TILE, d), lambda i: (i, 0)),
                  pl.BlockSpec((1, d), lambda i: (0, 0))],
        out_specs=pl.BlockSpec((ROW_TILE, d), lambda i: (i, 0)),
        out_shape=jax.ShapeDtypeStruct((m, d), out_dtype),
        compiler_params=_params("parallel"),
        name="rmsnorm",
    )(x, g.reshape(1, d))


def _input_tile(xp_ref, xs_ref):
    is_sample_tile = pl.program_id(0) == N_PROMPT // OUT_ROW_TILE
    pad = jnp.zeros((OUT_ROW_TILE - xs_ref.shape[0], xs_ref.shape[1]), F32)
    return jnp.where(is_sample_tile, jnp.concatenate([xs_ref[...], pad], axis=0), xp_ref[...])


def _input_specs(d, nb):
    n_p = N_PROMPT // OUT_ROW_TILE
    return [pl.BlockSpec((OUT_ROW_TILE, d), lambda i: (jnp.minimum(i, n_p - 1), 0)),
            pl.BlockSpec((nb, d), lambda i: (0, 0))]


def _rmsnorm_inputs_body(xp_ref, xs_ref, g_ref, o_ref):
    o_ref[...] = _rms(_input_tile(xp_ref, xs_ref), g_ref[...]).astype(o_ref.dtype)


def _rmsnorm_inputs(xp, xs, g):
    d = xp.shape[1]
    row = pl.BlockSpec((OUT_ROW_TILE, d), lambda i: (i, 0))
    return pl.pallas_call(
        _rmsnorm_inputs_body,
        grid=(N_PROMPT // OUT_ROW_TILE + 1,),
        in_specs=_input_specs(d, xs.shape[0]) + [pl.BlockSpec((1, d), lambda i: (0, 0))],
        out_specs=row,
        out_shape=jax.ShapeDtypeStruct((M_ROWS, d), BF16),
        compiler_params=_params("parallel"),
        name="rmsnorm_inputs",
    )(xp, xs, g.reshape(1, d))


def _add_rmsnorm_inputs_body(xp_ref, xs_ref, y_ref, g_ref, xo_ref, no_ref, *, scale):
    x = _input_tile(xp_ref, xs_ref) + scale * y_ref[...]
    xo_ref[...] = x
    no_ref[...] = _rms(x, g_ref[...]).astype(no_ref.dtype)


def _add_rmsnorm_inputs(xp, xs, y, g, scale):
    d = xp.shape[1]
    row = pl.BlockSpec((OUT_ROW_TILE, d), lambda i: (i, 0))
    return pl.pallas_call(
        functools.partial(_add_rmsnorm_inputs_body, scale=scale),
        grid=(N_PROMPT // OUT_ROW_TILE + 1,),
        in_specs=_input_specs(d, xs.shape[0]) + [row, pl.BlockSpec((1, d), lambda i: (0, 0))],
        out_specs=[row, row],
        out_shape=[jax.ShapeDtypeStruct((M_ROWS, d), F32), jax.ShapeDtypeStruct((M_ROWS, d), BF16)],
        compiler_params=_params("parallel"),
        name="add_rmsnorm_inputs",
    )(xp, xs, y, g.reshape(1, d))


def _add_final_norm_body(x_ref, y_ref, g_ref, op_ref, os_ref, *, scale):
    out = _rms(x_ref[...] + scale * y_ref[...], g_ref[...])
    is_sample_tile = pl.program_id(0) == N_PROMPT // OUT_ROW_TILE

    @pl.when(jnp.logical_not(is_sample_tile))
    def _():
        op_ref[...] = out

    @pl.when(is_sample_tile)
    def _():
        os_ref[...] = out[:SAMPLE_PAD]


def _add_final_norm(x, y, g, scale):
    m, d = x.shape
    n_p = N_PROMPT // OUT_ROW_TILE
    row = pl.BlockSpec((OUT_ROW_TILE, d), lambda i: (i, 0))
    return pl.pallas_call(
        functools.partial(_add_final_norm_body, scale=scale),
        grid=(n_p + 1,),
        in_specs=[row, row, pl.BlockSpec((1, d), lambda i: (0, 0))],
        out_specs=[pl.BlockSpec((OUT_ROW_TILE, d), lambda i: (jnp.minimum(i, n_p - 1), 0)),
                   pl.BlockSpec((SAMPLE_PAD, d), lambda i: (0, 0))],
        out_shape=[jax.ShapeDtypeStruct((N_PROMPT, d), F32), jax.ShapeDtypeStruct((SAMPLE_PAD, d), F32)],
        compiler_params=_params("arbitrary"),
        name="add_final_norm",
    )(x, y, g.reshape(1, d))


def _side_cast_specs(side, n_inner):
    in_specs, out_specs, out_shapes = [], [], []
    for src, rb, cb in side:
        r, c = src.shape
        n_cb = c // cb
        last = (r // rb) * n_cb - 1

        def index(i, j, n_cb=n_cb, last=last):
            blk = jnp.minimum(i * n_inner + j, last)
            return lax.div(blk, n_cb), lax.rem(blk, n_cb)

        in_specs.append(pl.BlockSpec((rb, cb), index))
        out_specs.append(pl.BlockSpec((rb, cb), index))
        out_shapes.append(jax.ShapeDtypeStruct((r, c), BF16))
    return in_specs, out_specs, out_shapes


def _side_cast(src_refs, dst_refs):
    for src_ref, dst_ref in zip(src_refs, dst_refs):
        dst_ref[...] = src_ref[...].astype(dst_ref.dtype)


def _mm_body(*refs, dots, n_a, n_extra, n_side, epilogue):
    a_refs = refs[:n_a]
    pos = n_a
    weights = []
    for ai, w_is_nk, shift in dots:
        if shift:
            w = jnp.concatenate([refs[pos][shift:, :].astype(BF16), refs[pos + 1][...].astype(BF16)], axis=0)
            pos += 2
        else:
            w = refs[pos][...].astype(BF16)
            pos += 1
        weights.append(w)
    n_in = pos + n_extra
    e_refs = refs[pos:n_in]
    o_ref = refs[n_in + n_side]
    prods = []
    for (ai, w_is_nk, _), w in zip(dots, weights):
        contract = (((1,), (1,)), ((), ())) if w_is_nk else (((1,), (0,)), ((), ()))
        prods.append(lax.dot_general(a_refs[ai][...], w, contract, preferred_element_type=F32))
    o_ref[...] = epilogue(prods, [e[...] for e in e_refs]).astype(o_ref.dtype)
    _side_cast(refs[n_in:n_in + n_side], refs[n_in + n_side + 1:])


def _mm(name, a_list, w_list, extras, n, out_dtype, epilogue, tm=MM_TM, tn=MM_TN, side=()):
    m = a_list[0].shape[0]
    in_specs = [pl.BlockSpec((tm, a.shape[1]), lambda i, j: (i, 0)) for a in a_list]
    operands = list(a_list)
    dots = []
    for ai, w, off, w_is_nk in w_list:
        shift = off % tn if w_is_nk else 0
        if w_is_nk:
            in_specs.append(pl.BlockSpec((tn, w.shape[1]), lambda i, j, o=off // tn: (j + o, 0)))
        else:
            in_specs.append(pl.BlockSpec((w.shape[0], tn), lambda i, j, o=off // tn: (0, j + o)))
        operands.append(w)
        if shift:
            in_specs.append(pl.BlockSpec((shift, w.shape[1]),
                                         lambda i, j, o=off // tn, r=tn // shift: ((j + o + 1) * r, 0)))
            operands.append(w)
        dots.append((ai, w_is_nk, shift))
    for arr, kind, off in extras:
        if kind == "tile":
            in_specs.append(pl.BlockSpec((tm, tn), lambda i, j, o=off // tn: (i, j + o)))
        else:
            in_specs.append(pl.BlockSpec((1, tn), lambda i, j, o=off // tn: (0, j + o)))
        operands.append(arr)
    side_in, side_out, side_shapes = _side_cast_specs(side, n // tn)
    body = functools.partial(_mm_body, dots=tuple(dots),
                             n_a=len(a_list), n_extra=len(extras), n_side=len(side), epilogue=epilogue)
    outs = pl.pallas_call(
        body,
        grid=(m // tm, n // tn),
        in_specs=in_specs + side_in,
        out_specs=[pl.BlockSpec((tm, tn), lambda i, j: (i, j))] + side_out,
        out_shape=[jax.ShapeDtypeStruct((m, n), out_dtype)] + side_shapes,
        compiler_params=_params("arbitrary", "arbitrary"),
        name=name,
    )(*operands, *[src for src, _, _ in side])
    return outs if side else outs[0]


def _ffn_body(x_ref, wg_ref, wu_ref, wo_ref, *rest):
    n_side = (len(rest) - 1) // 2
    o_ref = rest[n_side]
    _side_cast(rest[:n_side], rest[n_side + 1:])
    x = x_ref[...]
    g = jnp.dot(x, wg_ref[...].astype(BF16), preferred_element_type=F32)
    u = jnp.dot(x, wu_ref[...].astype(BF16), preferred_element_type=F32)
    h = (g * jax.nn.sigmoid(g) * u).astype(BF16)

    @pl.when(pl.program_id(1) == 0)
    def _():
        o_ref[...] = jnp.zeros_like(o_ref)

    for c in range(0, o_ref.shape[1], FFN_TN):
        o_ref[:, c:c + FFN_TN] += jnp.dot(h, wo_ref[:, c:c + FFN_TN].astype(BF16),
                                          preferred_element_type=F32)


def _swiglu(xn, w_in, w_out, side=()):
    m, d = xn.shape
    nf = D_FF // FFN_TF
    side_in, side_out, side_shapes = _side_cast_specs(side, nf)
    outs = pl.pallas_call(
        _ffn_body,
        grid=(m // FFN_TM, nf),
        in_specs=[pl.BlockSpec((FFN_TM, d), lambda i, f: (i, 0), pipeline_mode=pl.Buffered(1)),
                  pl.BlockSpec((d, FFN_TF), lambda i, f: (0, f)),
                  pl.BlockSpec((d, FFN_TF), lambda i, f: (0, f + nf)),
                  pl.BlockSpec((FFN_TF, d), lambda i, f: (f, 0))] + side_in,
        out_specs=[pl.BlockSpec((FFN_TM, d), lambda i, f: (i, 0), pipeline_mode=pl.Buffered(1))] + side_out,
        out_shape=[jax.ShapeDtypeStruct((m, d), F32)] + side_shapes,
        compiler_params=_params("arbitrary", "arbitrary", vmem_limit_bytes=FFN_VMEM_LIMIT_BYTES),
        name="swiglu",
    )(xn, w_in, w_in, w_out, *[src for src, _, _ in side])
    return outs if side else outs[0]


def _log_sigmoid(x):
    return jnp.minimum(x, 0.0) - jnp.log1p(jnp.exp(-jnp.abs(x)))


def _mlstm_prompt_body(q_ref, k_ref, v_ref, so_ref, if_ref, bias_ref,
                       h_ref, c_ref, n_ref, m_ref):
    L = MLSTM_CHUNK

    @pl.when(pl.program_id(1) == 0)
    def _():
        c_ref[...] = jnp.zeros_like(c_ref)
        n_ref[...] = jnp.zeros_like(n_ref)
        m_ref[...] = jnp.zeros_like(m_ref)

    x = if_ref[...] + bias_ref[...]
    xt = x.T
    t_idx = lax.broadcasted_iota(jnp.int32, (L, L), 0)
    s_idx = lax.broadcasted_iota(jnp.int32, (L, L), 1)
    causal = s_idx <= t_idx
    for h in range(H_M):
        ig_col = x[:, h:h + 1]
        ig_row = xt[h:h + 1, :]
        lf_col = _log_sigmoid(x[:, H_M + h:H_M + h + 1])
        lf_row = _log_sigmoid(xt[H_M + h:H_M + h + 1, :])
        b_col = jnp.sum(jnp.where(causal, lf_row, 0.0), axis=1, keepdims=True)
        b_row = jnp.sum(jnp.where(t_idx <= s_idx, lf_col, 0.0), axis=0, keepdims=True)
        m_prev = m_ref[:, h:h + 1]
        d = jnp.where(causal, b_col - b_row + ig_row, -jnp.inf)
        m_inter = b_col + m_prev
        m_t = jnp.maximum(m_inter, jnp.max(d, axis=1, keepdims=True))
        w_inter = jnp.exp(m_inter - m_t)

        qh = q_ref[:, h * DK_M:(h + 1) * DK_M]
        kf = k_ref[:, h * DK_M:(h + 1) * DK_M].astype(F32) * (DK_M ** -0.5)
        vh = v_ref[:, h * DV_M:(h + 1) * DV_M]
        s = lax.dot_general(qh, kf.astype(BF16), (((1,), (1,)), ((), ())),
                            preferred_element_type=F32)
        a = s * jnp.exp(d - m_t)
        c_old = c_ref[h]
        n_old = n_ref[h:h + 1, :]
        num = (jnp.dot(a.astype(BF16), vh, preferred_element_type=F32)
               + w_inter * jnp.dot(qh, c_old.astype(BF16), preferred_element_type=F32))
        den = (jnp.sum(a, axis=1, keepdims=True)
               + w_inter * jnp.sum(qh.astype(F32) * n_old, axis=1, keepdims=True))
        hh = num / jnp.maximum(jnp.abs(den), jnp.exp(-m_t))
        gate = so_ref[:, h * DV_M:(h + 1) * DV_M].astype(F32)
        h_ref[:, h * DV_M:(h + 1) * DV_M] = (gate * hh).astype(h_ref.dtype)

        m_new = m_t[L - 1:L, :]
        b_last = b_col[L - 1:L, :]
        w_end = jnp.exp(b_last - b_col + ig_col - m_new)
        decay = jnp.exp(b_last + m_prev - m_new)
        kw = kf * w_end
        dc = jnp.dot(kw.T.astype(BF16), vh, preferred_element_type=F32)
        c_ref[h] = decay * c_old + dc
        n_ref[h:h + 1, :] = decay * n_old + jnp.sum(kw, axis=0, keepdims=True)
        m_ref[:, h:h + 1] = m_new


def _mlstm_prompt(qkv_m, sig_o, igfg, gate_bias):
    nc = SEQ // MLSTM_CHUNK
    L = MLSTM_CHUNK

    def rows(width, col):
        return pl.BlockSpec((L, width), lambda b, c: (b * nc + c, col))

    return pl.pallas_call(
        _mlstm_prompt_body,
        grid=(BATCH, nc),
        in_specs=[rows(QK_M_COLS, 0), rows(QK_M_COLS, 1), rows(V_M_COLS, 1), rows(V_M_COLS, 0),
                  rows(LANES, 0), pl.BlockSpec((1, LANES), lambda b, c: (0, 0))],
        out_specs=[rows(V_M_COLS, 0),
                   pl.BlockSpec((None, H_M, DK_M, DV_M), lambda b, c: (b, 0, 0, 0)),
                   pl.BlockSpec((None, H_M, DK_M), lambda b, c: (b, 0, 0)),
                   pl.BlockSpec((None, 1, H_M), lambda b, c: (b, 0, 0))],
        out_shape=[jax.ShapeDtypeStruct((M_ROWS, V_M_COLS), BF16),
                   jax.ShapeDtypeStruct((BATCH, H_M, DK_M, DV_M), F32),
                   jax.ShapeDtypeStruct((BATCH, H_M, DK_M), F32),
                   jax.ShapeDtypeStruct((BATCH, 1, H_M), F32)],
        compiler_params=_params("parallel", "arbitrary"),
        name="mlstm_prompt",
    )(qkv_m, qkv_m, qkv_m, sig_o, igfg, gate_bias)


def _mlstm_sample_body(q_ref, qt_ref, k_ref, kt_ref, v_ref, so_ref, ig_ref, fg_ref, m0_ref,
                       c0_ref, n0_ref, h_ref, c_ref, n_ref, m_ref):
    for h in range(H_M):
        ig = ig_ref[h:h + 1, 0:1]
        lf = _log_sigmoid(fg_ref[h:h + 1, 0:1])
        m_prev = m0_ref[h:h + 1, 0:1]
        m_inter = lf + m_prev
        m_t = jnp.maximum(m_inter, ig)
        w_inter = jnp.exp(m_inter - m_t)
        q_row = q_ref[h:h + 1, :]
        k_row = k_ref[h:h + 1, :] * (DK_M ** -0.5)
        q_col = qt_ref[:, h:h + 1]
        k_col = kt_ref[:, h:h + 1] * (DK_M ** -0.5)
        v_row = v_ref[h:h + 1, :]
        c_old = c0_ref[h]
        n_old = n0_ref[h:h + 1, :]
        a = jnp.sum(q_row * k_row, axis=1, keepdims=True) * jnp.exp(ig - m_t)
        num = a * v_row + w_inter * jnp.sum(q_col * c_old, axis=0, keepdims=True)
        den = a + w_inter * jnp.sum(q_row * n_old, axis=1, keepdims=True)
        hh = num / jnp.maximum(jnp.abs(den), jnp.exp(-m_t))
        h_ref[h:h + 1, :] = so_ref[h:h + 1, :] * hh
        w_end = jnp.exp(ig - m_t)
        decay = jnp.exp(m_inter - m_t)
        c_ref[h] = decay * c_old + (k_col * w_end) * v_row
        n_ref[h:h + 1, :] = decay * n_old + k_row * w_end
        m_ref[h:h + 1, :] = jnp.broadcast_to(m_t, (1, LANES))


def _mlstm_sample(q, k, v, so, ig, fg, c0, n0, m0):
    nb = q.shape[0]

    def lanes(a):
        return jnp.broadcast_to(a[:, :, None], (nb, H_M, LANES))

    def per_b(*shape):
        return pl.BlockSpec((None,) + shape, lambda b: (b,) + (0,) * len(shape))

    return pl.pallas_call(
        _mlstm_sample_body,
        grid=(nb,),
        in_specs=[per_b(H_M, DK_M), per_b(DK_M, H_M), per_b(H_M, DK_M), per_b(DK_M, H_M),
                  per_b(H_M, DV_M), per_b(H_M, DV_M), per_b(H_M, LANES), per_b(H_M, LANES),
                  per_b(H_M, LANES), per_b(H_M, DK_M, DV_M), per_b(H_M, DK_M)],
        out_specs=[per_b(H_M, DV_M), per_b(H_M, DK_M, DV_M), per_b(H_M, DK_M), per_b(H_M, LANES)],
        out_shape=[jax.ShapeDtypeStruct((nb, H_M, DV_M), F32),
                   jax.ShapeDtypeStruct((nb, H_M, DK_M, DV_M), F32),
                   jax.ShapeDtypeStruct((nb, H_M, DK_M), F32),
                   jax.ShapeDtypeStruct((nb, H_M, LANES), F32)],
        compiler_params=_params("parallel"),
        name="mlstm_sample",
    )(q, jnp.swapaxes(q, 1, 2), k, jnp.swapaxes(k, 1, 2), v, so, lanes(ig), lanes(fg), lanes(m0),
      c0, n0)


def _alibi_slopes():
    n = N_GROUPS * H_A
    e = np.arange(1, n + 1, dtype=np.float64) * (-ALIBI_MAX_EXP / n)
    return np.exp2(e).reshape(N_GROUPS, H_A)


def _softmax_mix(lses, outs):
    m = jnp.maximum(jnp.maximum(lses[0], lses[1]), lses[2])
    e = [jnp.exp(x - m) for x in lses]
    z = e[0] + e[1] + e[2]
    acc = (e[0] / z) * outs[0]
    for g in (1, 2):
        acc = acc + (e[g] / z) * outs[g]
    return acc


def _attn_prompt_body(slopes_ref, *refs):
    G = N_GROUPS
    q_refs, kc_refs, kp_refs = refs[0:G], refs[G:2 * G], refs[2 * G:3 * G]
    vc_refs, vp_refs = refs[3 * G:4 * G], refs[4 * G:5 * G]
    o_ref = refs[5 * G]
    o_scr, l_scr = refs[5 * G + 1:6 * G + 1], refs[6 * G + 1:7 * G + 1]
    L = Q_BLOCK
    head = pl.program_id(1)
    first_key = jnp.where(pl.program_id(2) == 0, L, 0)
    t_idx = lax.broadcasted_iota(jnp.int32, (L, 2 * L), 0)
    s_idx = lax.broadcasted_iota(jnp.int32, (L, 2 * L), 1)
    step = t_idx + L - s_idx
    in_window = jnp.logical_and(step >= 0, step <= L)
    in_window_first = jnp.logical_and(in_window, s_idx >= first_key)
    nt = (((1,), (1,)), ((), ()))

    def rows(start, size, dil):
        return pl.ds(start, size) if dil == 1 else pl.ds(start, size, stride=dil)

    for g in range(G):
        dil = DILATIONS[g]
        bias = step.astype(F32) * (slopes_ref[g, head] * dil)

        def sub_block(j, r, g=g, dil=dil, bias=bias):
            start = j * L * dil + r
            q = q_refs[g][rows(start, L, dil), :].astype(BF16)
            if j == 0:
                k2 = jnp.concatenate([kp_refs[g][rows(r, L, dil), :], kc_refs[g][rows(start, L, dil), :]], axis=0)
                v2 = jnp.concatenate([vp_refs[g][rows(r, L, dil), :], vc_refs[g][rows(start, L, dil), :]], axis=0)
                mask = in_window_first
            else:
                k2 = kc_refs[g][rows(start - L * dil, 2 * L, dil), :]
                v2 = vc_refs[g][rows(start - L * dil, 2 * L, dil), :]
                mask = in_window
            s = lax.dot_general(q, k2.astype(BF16), nt, preferred_element_type=F32)
            s = jnp.where(mask, s * (DH_A ** -0.5) - bias, NEG_INF)
            m = jnp.max(s, axis=1, keepdims=True)
            p = jnp.exp(s - m)
            l = jnp.sum(p, axis=1, keepdims=True)
            o = jnp.dot(p.astype(BF16), v2.astype(BF16), preferred_element_type=F32) / l
            o_scr[g][rows(start, L, dil), :] = o
            l_scr[g][rows(start, L, dil), :] = jnp.broadcast_to(m + jnp.log(l), (L, LANES))

        for j in range(ATT_SB // (L * dil)):
            if dil == 1:
                sub_block(j, 0)
            else:
                def residue(r, carry, j=j):
                    sub_block(j, r)
                    return carry
                lax.fori_loop(0, dil, residue, 0, unroll=16)

    chunk = 256
    for c in range(0, ATT_SB, chunk):
        lses = [l_scr[g][c:c + chunk, :] for g in range(G)]
        outs = [o_scr[g][c:c + chunk, :] for g in range(G)]
        o_ref[c:c + chunk, :] = _softmax_mix(lses, outs).astype(o_ref.dtype)


def _attn_prompt(qkv_a):
    ns = SEQ // ATT_SB
    L = Q_BLOCK
    heads_per_part = N_GROUPS * H_A

    def cur(part, g):
        return pl.BlockSpec((ATT_SB, DH_A), lambda b, h, i: (b * ns + i, part * heads_per_part + g * H_A + h))

    def prev(part, g):
        span = L * DILATIONS[g]
        f = ATT_SB // span
        return pl.BlockSpec((span, DH_A),
                            lambda b, h, i: (b * ns * f + jnp.maximum(i * f - 1, 0),
                                             part * heads_per_part + g * H_A + h))

    groups = range(N_GROUPS)
    in_specs = ([pl.BlockSpec(memory_space=pltpu.SMEM)]
                + [cur(0, g) for g in groups] + [cur(1, g) for g in groups] + [prev(1, g) for g in groups]
                + [cur(2, g) for g in groups] + [prev(2, g) for g in groups])
    slopes = jnp.asarray(_alibi_slopes(), F32)
    return pl.pallas_call(
        _attn_prompt_body,
        grid=(BATCH, H_A, ns),
        in_specs=in_specs,
        out_specs=pl.BlockSpec((ATT_SB, DH_A), lambda b, h, i: (b * ns + i, h)),
        out_shape=jax.ShapeDtypeStruct((M_ROWS, A_GROUP_COLS), BF16),
        scratch_shapes=[pltpu.VMEM((ATT_SB, DH_A), F32)] * (2 * N_GROUPS),
        compiler_params=_params("parallel", "parallel", "arbitrary"),
        name="attn_prompt",
    )(slopes, *([qkv_a] * (5 * N_GROUPS)))


def _attn_sample_body(slopes_ref, q_ref, kn_ref, vn_ref, c0_ref, c1_ref, c2_ref, o_ref):
    L = Q_BLOCK
    c_refs = (c0_ref, c1_ref, c2_ref)
    steps = (L - lax.broadcasted_iota(jnp.int32, (L, 1, 1), 0)).astype(F32)
    outs, lses = [], []
    for g in range(N_GROUPS):
        q = q_ref[g]
        slope = slopes_ref[g][:, 0:1]
        s = (jnp.sum(c_refs[g][:, 0] * q[None], axis=-1, keepdims=True) * (DH_A ** -0.5)
             - (slope * DILATIONS[g])[None] * steps)
        s_new = jnp.sum(q * kn_ref[g], axis=-1, keepdims=True) * (DH_A ** -0.5)
        m = jnp.maximum(jnp.max(s, axis=0), s_new)
        p = jnp.exp(s - m[None])
        p_new = jnp.exp(s_new - m)
        l = jnp.sum(p, axis=0) + p_new
        outs.append((jnp.sum(p * c_refs[g][:, 1], axis=0) + p_new * vn_ref[g]) / l)
        lses.append(m + jnp.log(l))
    o_ref[...] = _softmax_mix(lses, outs)


def _attn_sample(q, kn, vn, caches):
    nb = q.shape[0]
    small = pl.BlockSpec((None, N_GROUPS, H_A, DH_A), lambda b: (b, 0, 0, 0))
    cache_specs, cache_views = [], []
    for g, c in enumerate(caches):
        dil = DILATIONS[g]
        assert c.shape[1] == Q_BLOCK * dil
        cache_views.append(c.reshape(nb, Q_BLOCK, dil, 2, H_A, DH_A))
        cache_specs.append(pl.BlockSpec((None, Q_BLOCK, None, 2, H_A, DH_A), lambda b: (b, 0, 0, 0, 0, 0)))
    slopes = jnp.broadcast_to(jnp.asarray(_alibi_slopes(), F32)[:, :, None], (N_GROUPS, H_A, LANES))
    return pl.pallas_call(
        _attn_sample_body,
        grid=(nb,),
        in_specs=[pl.BlockSpec((N_GROUPS, H_A, LANES), lambda b: (0, 0, 0)), small, small, small] + cache_specs,
        out_specs=pl.BlockSpec((None, H_A, DH_A), lambda b: (b, 0, 0)),
        out_shape=jax.ShapeDtypeStruct((nb, H_A, DH_A), F32),
        compiler_params=_params("parallel"),
        name="attn_sample",
    )(slopes, q, kn, vn, *cache_views)


def _logistic(x):
    return 0.5 * jnp.tanh(0.5 * x) + 0.5


def _sigmoid_epilogue(prods, extras):
    return _logistic(prods[0])


def _gate_epilogue(prods, extras):
    return _logistic(prods[0] + extras[0])


def _plain_epilogue(prods, extras):
    return prods[0]


def _mix_epilogue(prods, extras):
    return extras[0].astype(F32) * prods[0] + extras[1].astype(F32) * prods[1]


def _residual_epilogue(prods, extras):
    return extras[0] + prods[0]


def _with_sample_rows(prompt_rows_array, sample_rows):
    nb, c = sample_rows.shape
    block = jnp.concatenate([sample_rows, jnp.zeros((SAMPLE_PAD - nb, c), sample_rows.dtype)], axis=0)
    return lax.dynamic_update_slice(prompt_rows_array, block.astype(prompt_rows_array.dtype), (N_PROMPT, 0))


def kernel(x_prompt, x_sample, state_mlstm_C, state_mlstm_n, state_mlstm_m, cache_win1_kv, cache_win2_kv, cache_win3_kv, g_ffn1, w_ffn1_in, w_ffn1_out, g_mix, w_in, b_gate, b_igate, b_fgate, w_mlstm_out, w_attn_out, w_out, g_ffn2, w_ffn2_in, w_ffn2_out, g_final):
    d = D_MODEL
    nb = DEC_BATCH
    xp, xs = x_prompt.reshape(N_PROMPT, d), x_sample.reshape(nb, d)

    w_in_nk = w_in[0].T
    gate_bias = jnp.pad(jnp.concatenate([b_igate[0], b_fgate[0]]), (0, LANES - 2 * H_M)).reshape(1, LANES)

    xn = _rmsnorm_inputs(xp, xs, g_ffn1[0])
    y = _swiglu(xn, w_ffn1_in[0], w_ffn1_out[0])
    x1, u = _add_rmsnorm_inputs(xp, xs, y, g_mix[0], 0.5)

    qkv_m, w_mo = _mm("proj_qkv_m", [u], [(0, w_in_nk, 0, True)], [], COL_O_M, BF16, _plain_epilogue,
                      side=[(w_mlstm_out[0], 512, 512)])
    sig_o, w_ao = _mm("proj_o_m", [u], [(0, w_in_nk, COL_O_M, True)], [], V_M_COLS, BF16, _sigmoid_epilogue,
                      side=[(w_attn_out[0], 256, 512)])
    igfg = _mm("proj_if", [u], [(0, w_in_nk, COL_IF, True)], [], LANES, F32, _plain_epilogue, tn=LANES)
    qkv_a = _mm("proj_qkv_a", [u], [(0, w_in_nk, COL_Q_A, True)], [], 3 * A_COLS, F32, _plain_epilogue)
    gates, w_o = _mm("proj_gates", [u], [(0, w_in_nk, COL_GATE, True)],
                     [(b_gate[0].reshape(1, 2 * d), "row", 0)], 2 * d, BF16, _gate_epilogue,
                     side=[(w_out[0], 512, 512)])

    hm, c_p, n_p, m_p = _mlstm_prompt(qkv_m, sig_o, igfg, gate_bias)
    qkv_s = qkv_m[N_PROMPT:N_PROMPT + nb].astype(F32)
    if_s = igfg[N_PROMPT:N_PROMPT + nb] + gate_bias
    hm_s, c_s, n_s, m_s = _mlstm_sample(
        qkv_s[:, :QK_M_COLS].reshape(nb, H_M, DK_M),
        qkv_s[:, QK_M_COLS:2 * QK_M_COLS].reshape(nb, H_M, DK_M),
        qkv_s[:, 2 * QK_M_COLS:].reshape(nb, H_M, DV_M),
        sig_o[N_PROMPT:N_PROMPT + nb].astype(F32).reshape(nb, H_M, DV_M),
        if_s[:, :H_M], if_s[:, H_M:2 * H_M],
        state_mlstm_C[0], state_mlstm_n[0], state_mlstm_m[0])
    hm = _with_sample_rows(hm, hm_s.reshape(nb, V_M_COLS))

    ha = _attn_prompt(qkv_a)
    qkv_as = qkv_a[N_PROMPT:N_PROMPT + nb].reshape(nb, 3, N_GROUPS, H_A, DH_A)
    ha_s = _attn_sample(qkv_as[:, 0], qkv_as[:, 1], qkv_as[:, 2],
                        (cache_win1_kv[0], cache_win2_kv[0], cache_win3_kv[0]))
    ha = _with_sample_rows(ha, ha_s.reshape(nb, A_GROUP_COLS))

    z = _mm("mix", [hm, ha], [(0, w_mo, 0, False), (1, w_ao, 0, False)],
            [(gates, "tile", 0), (gates, "tile", d)], d, BF16, _mix_epilogue)
    x2 = _mm("proj_out", [z], [(0, w_o, 0, False)], [(x1, "tile", 0)], d, F32, _residual_epilogue)

    xn2 = _rmsnorm(x2, g_ffn2[0], BF16)
    y2 = _swiglu(xn2, w_ffn2_in[0], w_ffn2_out[0])
    out_p, out_s = _add_final_norm(x2, y2, g_final, 0.5)

    win_p, win_s = [], []
    for g, w in enumerate(WINDOWS):
        n_last = min(w, SEQ)
        k0, v0 = A_COLS + g * A_GROUP_COLS, 2 * A_COLS + g * A_GROUP_COLS
        per_seq = []
        for b in range(BATCH):
            last = qkv_a[(b + 1) * SEQ - n_last:(b + 1) * SEQ]
            per_seq.append(jnp.stack([last[:, k0:k0 + A_GROUP_COLS], last[:, v0:v0 + A_GROUP_COLS]], axis=1))
        win_p.append(jnp.stack(per_seq).reshape(1, BATCH, n_last, 2, H_A, DH_A))
        win_s.append(qkv_as[:, 1:3, g].reshape(1, nb, 1, 2, H_A, DH_A))
    return (out_p.reshape(BATCH, SEQ, d), out_s[:nb].reshape(nb, 1, d),
            c_p[None], n_p[None], m_p.reshape(1, BATCH, H_M),
            c_s[None], n_s[None], m_s[:, :, 0][None],
            win_p[0], win_p[1], win_p[2], win_s[0], win_s[1], win_s[2])
```

```python
import functools

import numpy as np
import jax
import jax.numpy as jnp
from jax import lax
from jax.experimental import pallas as pl
from jax.experimental.pallas import tpu as pltpu

F32 = jnp.float32
BF16 = jnp.bfloat16

D_MODEL = 4096
BATCH = 2
SEQ = 4096
DEC_BATCH = 32
H_M = 8
DK_M = 256
DV_M = 512
MLSTM_CHUNK = 128
WINDOWS = (128, 512, 2048)
DILATIONS = (1, 4, 16)
N_GROUPS = 3
H_A = 8
DH_A = 128
Q_BLOCK = 128
ALIBI_MAX_EXP = 8.0
D_FF = 11008
EPS = 1e-6
NEG_INF = -1e30

QK_M_COLS = H_M * DK_M
V_M_COLS = H_M * DV_M
A_GROUP_COLS = H_A * DH_A
A_COLS = N_GROUPS * A_GROUP_COLS
COL_O_M = 2 * QK_M_COLS + V_M_COLS
COL_IF = COL_O_M + V_M_COLS
COL_Q_A = COL_IF + 2 * H_M
COL_GATE = COL_Q_A + 3 * A_COLS
LANES = 128
SUBLANES = 8

N_PROMPT = BATCH * SEQ
SAMPLE_PAD = 64
M_ROWS = N_PROMPT + SAMPLE_PAD

V7X_VMEM_BYTES = 64 * 1024 * 1024
VMEM_LIMIT_BYTES = V7X_VMEM_BYTES - 8 * 1024 * 1024
FFN_VMEM_LIMIT_BYTES = V7X_VMEM_BYTES - 2 * 1024 * 1024

ROW_TILE = 192
OUT_ROW_TILE = 128
MM_TM = 1376
MM_TN = 512
FFN_TM = 1376
FFN_TF = 256
FFN_TN = 512
ATT_SB = 2048


def _params(*semantics, vmem_limit_bytes=VMEM_LIMIT_BYTES):
    return pltpu.CompilerParams(dimension_semantics=semantics, vmem_limit_bytes=vmem_limit_bytes)


def _rms(x, g):
    ms = jnp.mean(x * x, axis=-1, keepdims=True)
    return x * lax.rsqrt(ms + EPS) * g


def _rmsnorm_body(x_ref, g_ref, o_ref):
    o_ref[...] = _rms(x_ref[...], g_ref[...]).astype(o_ref.dtype)


def _rmsnorm(x, g, out_dtype):
    m, d = x.shape
    return pl.pallas_call(
        _rmsnorm_body,
        grid=(m // ROW_TILE,),
        in_specs=[pl.BlockSpec((ROW_TILE, d), lambda i: (i, 0)),
                  pl.BlockSpec((1, d), lambda i: (0, 0))],
        out_specs=pl.BlockSpec((ROW_TILE, d), lambda i: (i, 0)),
        out_shape=jax.ShapeDtypeStruct((m, d), out_dtype),
        compiler_params=_params("parallel"),
        name="rmsnorm",
    )(x, g.reshape(1, d))


def _input_tile(xp_ref, xs_ref):
    is_sample_tile = pl.program_id(0) == N_PROMPT // OUT_ROW_TILE
    pad = jnp.zeros((OUT_ROW_TILE - xs_ref.shape[0], xs_ref.shape[1]), F32)
    return jnp.where(is_sample_tile, jnp.concatenate([xs_ref[...], pad], axis=0), xp_ref[...])


def _input_specs(d, nb):
    n_p = N_PROMPT // OUT_ROW_TILE
    return [pl.BlockSpec((OUT_ROW_TILE, d), lambda i: (jnp.minimum(i, n_p - 1), 0)),
            pl.BlockSpec((nb, d), lambda i: (0, 0))]


def _rmsnorm_inputs_body(xp_ref, xs_ref, g_ref, o_ref):
    o_ref[...] = _rms(_input_tile(xp_ref, xs_ref), g_ref[...]).astype(o_ref.dtype)


def _rmsnorm_inputs(xp, xs, g):
    d = xp.shape[1]
    row = pl.BlockSpec((OUT_ROW_TILE, d), lambda i: (i, 0))
    return pl.pallas_call(
        _rmsnorm_inputs_body,
        grid=(N_PROMPT // OUT_ROW_TILE + 1,),
        in_specs=_input_specs(d, xs.shape[0]) + [pl.BlockSpec((1, d), lambda i: (0, 0))],
        out_specs=row,
        out_shape=jax.ShapeDtypeStruct((M_ROWS, d), BF16),
        compiler_params=_params("parallel"),
        name="rmsnorm_inputs",
    )(xp, xs, g.reshape(1, d))


def _add_rmsnorm_inputs_body(xp_ref, xs_ref, y_ref, g_ref, xo_ref, no_ref, *, scale):
    x = _input_tile(xp_ref, xs_ref) + scale * y_ref[...]
    xo_ref[...] = x
    no_ref[...] = _rms(x, g_ref[...]).astype(no_ref.dtype)


def _add_rmsnorm_inputs(xp, xs, y, g, scale):
    d = xp.shape[1]
    row = pl.BlockSpec((OUT_ROW_TILE, d), lambda i: (i, 0))
    return pl.pallas_call(
        functools.partial(_add_rmsnorm_inputs_body, scale=scale),
        grid=(N_PROMPT // OUT_ROW_TILE + 1,),
        in_specs=_input_specs(d, xs.shape[0]) + [row, pl.BlockSpec((1, d), lambda i: (0, 0))],
        out_specs=[row, row],
        out_shape=[jax.ShapeDtypeStruct((M_ROWS, d), F32), jax.ShapeDtypeStruct((M_ROWS, d), BF16)],
        compiler_params=_params("parallel"),
        name="add_rmsnorm_inputs",
    )(xp, xs, y, g.reshape(1, d))


def _add_final_norm_body(x_ref, y_ref, g_ref, op_ref, os_ref, *, scale):
    out = _rms(x_ref[...] + scale * y_ref[...], g_ref[...])
    is_sample_tile = pl.program_id(0) == N_PROMPT // OUT_ROW_TILE

    @pl.when(jnp.logical_not(is_sample_tile))
    def _():
        op_ref[...] = out

    @pl.when(is_sample_tile)
    def _():
        os_ref[...] = out[:SAMPLE_PAD]


def _add_final_norm(x, y, g, scale):
    m, d = x.shape
    n_p = N_PROMPT // OUT_ROW_TILE
    row = pl.BlockSpec((OUT_ROW_TILE, d), lambda i: (i, 0))
    return pl.pallas_call(
        functools.partial(_add_final_norm_body, scale=scale),
        grid=(n_p + 1,),
        in_specs=[row, row, pl.BlockSpec((1, d), lambda i: (0, 0))],
        out_specs=[pl.BlockSpec((OUT_ROW_TILE, d), lambda i: (jnp.minimum(i, n_p - 1), 0)),
                   pl.BlockSpec((SAMPLE_PAD, d), lambda i: (0, 0))],
        out_shape=[jax.ShapeDtypeStruct((N_PROMPT, d), F32), jax.ShapeDtypeStruct((SAMPLE_PAD, d), F32)],
        compiler_params=_params("arbitrary"),
        name="add_final_norm",
    )(x, y, g.reshape(1, d))


def _side_cast_specs(side, n_inner):
    in_specs, out_specs, out_shapes = [], [], []
    for src, rb, cb in side:
        r, c = src.shape
        n_cb = c // cb
        last = (r // rb) * n_cb - 1

        def index(i, j, n_cb=n_cb, last=last):
            blk = jnp.minimum(i * n_inner + j, last)
            return lax.div(blk, n_cb), lax.rem(blk, n_cb)

        in_specs.append(pl.BlockSpec((rb, cb), index))
        out_specs.append(pl.BlockSpec((rb, cb), index))
        out_shapes.append(jax.ShapeDtypeStruct((r, c), BF16))
    return in_specs, out_specs, out_shapes


def _side_cast(src_refs, dst_refs):
    for src_ref, dst_ref in zip(src_refs, dst_refs):
        dst_ref[...] = src_ref[...].astype(dst_ref.dtype)


def _mm_body(*refs, dots, n_a, n_extra, n_side, epilogue):
    a_refs = refs[:n_a]
    pos = n_a
    weights = []
    for ai, w_is_nk, shift in dots:
        if shift:
            w = jnp.concatenate([refs[pos][shift:, :].astype(BF16), refs[pos + 1][...].astype(BF16)], axis=0)
            pos += 2
        else:
            w = refs[pos][...].astype(BF16)
            pos += 1
        weights.append(w)
    n_in = pos + n_extra
    e_refs = refs[pos:n_in]
    o_ref = refs[n_in + n_side]
    prods = []
    for (ai, w_is_nk, _), w in zip(dots, weights):
        contract = (((1,), (1,)), ((), ())) if w_is_nk else (((1,), (0,)), ((), ()))
        prods.append(lax.dot_general(a_refs[ai][...], w, contract, preferred_element_type=F32))
    o_ref[...] = epilogue(prods, [e[...] for e in e_refs]).astype(o_ref.dtype)
    _side_cast(refs[n_in:n_in + n_side], refs[n_in + n_side + 1:])


def _mm(name, a_list, w_list, extras, n, out_dtype, epilogue, tm=MM_TM, tn=MM_TN, side=()):
    m = a_list[0].shape[0]
    in_specs = [pl.BlockSpec((tm, a.shape[1]), lambda i, j: (i, 0)) for a in a_list]
    operands = list(a_list)
    dots = []
    for ai, w, off, w_is_nk in w_list:
        shift = off % tn if w_is_nk else 0
        if w_is_nk:
            in_specs.append(pl.BlockSpec((tn, w.shape[1]), lambda i, j, o=off // tn: (j + o, 0)))
        else:
            in_specs.append(pl.BlockSpec((w.shape[0], tn), lambda i, j, o=off // tn: (0, j + o)))
        operands.append(w)
        if shift:
            in_specs.append(pl.BlockSpec((shift, w.shape[1]),
                                         lambda i, j, o=off // tn, r=tn // shift: ((j + o + 1) * r, 0)))
            operands.append(w)
        dots.append((ai, w_is_nk, shift))
    for arr, kind, off in extras:
        if kind == "tile":
            in_specs.append(pl.BlockSpec((tm, tn), lambda i, j, o=off // tn: (i, j + o)))
        else:
            in_specs.append(pl.BlockSpec((1, tn), lambda i, j, o=off // tn: (0, j + o)))
        operands.append(arr)
    side_in, side_out, side_shapes = _side_cast_specs(side, n // tn)
    body = functools.partial(_mm_body, dots=tuple(dots),
                             n_a=len(a_list), n_extra=len(extras), n_side=len(side), epilogue=epilogue)
    outs = pl.pallas_call(
        body,
        grid=(m // tm, n // tn),
        in_specs=in_specs + side_in,
        out_specs=[pl.BlockSpec((tm, tn), lambda i, j: (i, j))] + side_out,
        out_shape=[jax.ShapeDtypeStruct((m, n), out_dtype)] + side_shapes,
        compiler_params=_params("arbitrary", "arbitrary"),
        name=name,
    )(*operands, *[src for src, _, _ in side])
    return outs if side else outs[0]


def _ffn_body(x_ref, wg_ref, wu_ref, wo_ref, *rest):
    n_side = (len(rest) - 1) // 2
    o_ref = rest[n_side]
    _side_cast(rest[:n_side], rest[n_side + 1:])
    x = x_ref[...]
    g = jnp.dot(x, wg_ref[...].astype(BF16), preferred_element_type=F32)
    u = jnp.dot(x, wu_ref[...].astype(BF16), preferred_element_type=F32)
    h = (g * jax.nn.sigmoid(g) * u).astype(BF16)

    @pl.when(pl.program_id(1) == 0)
    def _():
        o_ref[...] = jnp.zeros_like(o_ref)

    for c in range(0, o_ref.shape[1], FFN_TN):
        o_ref[:, c:c + FFN_TN] += jnp.dot(h, wo_ref[:, c:c + FFN_TN].astype(BF16),
                                          preferred_element_type=F32)


def _swiglu(xn, w_in, w_out, side=()):
    m, d = xn.shape
    nf = D_FF // FFN_TF
    side_in, side_out, side_shapes = _side_cast_specs(side, nf)
    outs = pl.pallas_call(
        _ffn_body,
        grid=(m // FFN_TM, nf),
        in_specs=[pl.BlockSpec((FFN_TM, d), lambda i, f: (i, 0), pipeline_mode=pl.Buffered(1)),
                  pl.BlockSpec((d, FFN_TF), lambda i, f: (0, f)),
                  pl.BlockSpec((d, FFN_TF), lambda i, f: (0, f + nf)),
                  pl.BlockSpec((FFN_TF, d), lambda i, f: (f, 0))] + side_in,
        out_specs=[pl.BlockSpec((FFN_TM, d), lambda i, f: (i, 0), pipeline_mode=pl.Buffered(1))] + side_out,
        out_shape=[jax.ShapeDtypeStruct((m, d), F32)] + side_shapes,
        compiler_params=_params("arbitrary", "arbitrary", vmem_limit_bytes=FFN_VMEM_LIMIT_BYTES),
        name="swiglu",
    )(xn, w_in, w_in, w_out, *[src for src, _, _ in side])
    return outs if side else outs[0]


def _log_sigmoid(x):
    return jnp.minimum(x, 0.0) - jnp.log1p(jnp.exp(-jnp.abs(x)))


def _mlstm_prompt_body(q_ref, k_ref, v_ref, so_ref, if_ref, bias_ref,
                       h_ref, c_ref, n_ref, m_ref):
    L = MLSTM_CHUNK

    @pl.when(pl.program_id(1) == 0)
    def _():
        c_ref[...] = jnp.zeros_like(c_ref)
        n_ref[...] = jnp.zeros_like(n_ref)
        m_ref[...] = jnp.zeros_like(m_ref)

    x = if_ref[...] + bias_ref[...]
    xt = x.T
    t_idx = lax.broadcasted_iota(jnp.int32, (L, L), 0)
    s_idx = lax.broadcasted_iota(jnp.int32, (L, L), 1)
    causal = s_idx <= t_idx
    b_cols = jnp.dot(causal.astype(F32), _log_sigmoid(x), precision=lax.Precision.HIGHEST,
                     preferred_element_type=F32)
    b_rows = b_cols.T
    for h in range(H_M):
        ig_col = x[:, h:h + 1]
        ig_row = xt[h:h + 1, :]
        b_col = b_cols[:, H_M + h:H_M + h + 1]
        b_row = b_rows[H_M + h:H_M + h + 1, :]
        m_prev = m_ref[h, 0:1, 0:1]
        d = jnp.where(causal, b_col - b_row + ig_row, -jnp.inf)
        m_inter = b_col + m_prev
        m_t = jnp.maximum(m_inter, jnp.max(d, axis=1, keepdims=True))
        w_inter = jnp.exp(m_inter - m_t)
        m_new = m_t[L - 1:L, :]
        b_last = b_col[L - 1:L, :]
        w_end = jnp.exp(b_last - b_col + ig_col - m_new)
        decay = jnp.exp(b_last + m_prev - m_new)

        qh = q_ref[:, h * DK_M:(h + 1) * DK_M]
        kf = k_ref[:, h * DK_M:(h + 1) * DK_M].astype(F32) * (DK_M ** -0.5)
        s = lax.dot_general(qh, kf.astype(BF16), (((1,), (1,)), ((), ())),
                            preferred_element_type=F32)
        c_old = c_ref[h]
        qc = jnp.dot(qh, c_old.astype(BF16), preferred_element_type=F32)
        n_tile = n_ref[h]
        n_old = n_tile[0:1, :]
        qn = lax.dot_general(qh, n_tile.astype(BF16), (((1,), (1,)), ((), ())),
                             preferred_element_type=F32)[:, 0:1]
        kw = kf * w_end
        a = s * jnp.exp(d - m_t)
        vh = v_ref[:, h * DV_M:(h + 1) * DV_M]
        num = jnp.dot(a.astype(BF16), vh, preferred_element_type=F32) + w_inter * qc
        den = jnp.sum(a, axis=1, keepdims=True) + w_inter * qn
        hh = num / jnp.maximum(jnp.abs(den), jnp.exp(-m_t))
        gate = so_ref[:, h * DV_M:(h + 1) * DV_M].astype(F32)
        h_ref[:, h * DV_M:(h + 1) * DV_M] = (gate * hh).astype(h_ref.dtype)

        dc = jnp.dot(kw.T.astype(BF16), vh, preferred_element_type=F32)
        c_ref[h] = decay * c_old + dc
        n_new = decay * n_old + jnp.sum(kw, axis=0, keepdims=True)
        n_ref[h] = jnp.broadcast_to(n_new, n_ref.shape[1:])
        m_ref[h] = jnp.broadcast_to(m_new, m_ref.shape[1:])


def _mlstm_prompt(qkv_m, sig_o, igfg, gate_bias):
    nc = SEQ // MLSTM_CHUNK
    L = MLSTM_CHUNK

    def rows(width, col):
        return pl.BlockSpec((L, width), lambda b, c: (b * nc + c, col))

    hm, c, n, m = pl.pallas_call(
        _mlstm_prompt_body,
        grid=(BATCH, nc),
        in_specs=[rows(QK_M_COLS, 0), rows(QK_M_COLS, 1), rows(V_M_COLS, 1), rows(V_M_COLS, 0),
                  rows(LANES, 0), pl.BlockSpec((1, LANES), lambda b, c: (0, 0))],
        out_specs=[rows(V_M_COLS, 0),
                   pl.BlockSpec((None, H_M, DK_M, DV_M), lambda b, c: (b, 0, 0, 0)),
                   pl.BlockSpec((None, H_M, SUBLANES, DK_M), lambda b, c: (b, 0, 0, 0)),
                   pl.BlockSpec((None, H_M, SUBLANES, LANES), lambda b, c: (b, 0, 0, 0))],
        out_shape=[jax.ShapeDtypeStruct((M_ROWS, V_M_COLS), BF16),
                   jax.ShapeDtypeStruct((BATCH, H_M, DK_M, DV_M), F32),
                   jax.ShapeDtypeStruct((BATCH, H_M, SUBLANES, DK_M), F32),
                   jax.ShapeDtypeStruct((BATCH, H_M, SUBLANES, LANES), F32)],
        compiler_params=_params("parallel", "arbitrary"),
        name="mlstm_prompt",
    )(qkv_m, qkv_m, qkv_m, sig_o, igfg, gate_bias)
    return hm, c, n[:, :, 0, :], m[:, :, 0, 0]


def _mlstm_sample_body(q_ref, qt_ref, k_ref, kt_ref, v_ref, so_ref, ig_ref, fg_ref, m0_ref,
                       c0_ref, n0_ref, h_ref, c_ref, n_ref, m_ref):
    for h in range(H_M):
        ig = ig_ref[h:h + 1, 0:1]
        lf = _log_sigmoid(fg_ref[h:h + 1, 0:1])
        m_prev = m0_ref[h:h + 1, 0:1]
        m_inter = lf + m_prev
        m_t = jnp.maximum(m_inter, ig)
        w_inter = jnp.exp(m_inter - m_t)
        q_row = q_ref[h:h + 1, :]
        k_row = k_ref[h:h + 1, :] * (DK_M ** -0.5)
        q_col = qt_ref[:, h:h + 1]
        k_col = kt_ref[:, h:h + 1] * (DK_M ** -0.5)
        v_row = v_ref[h:h + 1, :]
        c_old = c0_ref[h]
        n_old = n0_ref[h:h + 1, :]
        a = jnp.sum(q_row * k_row, axis=1, keepdims=True) * jnp.exp(ig - m_t)
        num = a * v_row + w_inter * jnp.sum(q_col * c_old, axis=0, keepdims=True)
        den = a + w_inter * jnp.sum(q_row * n_old, axis=1, keepdims=True)
        hh = num / jnp.maximum(jnp.abs(den), jnp.exp(-m_t))
        h_ref[h:h + 1, :] = so_ref[h:h + 1, :] * hh
        w_end = jnp.exp(ig - m_t)
        decay = jnp.exp(m_inter - m_t)
        c_ref[h] = decay * c_old + (k_col * w_end) * v_row
        n_ref[h:h + 1, :] = decay * n_old + k_row * w_end
        m_ref[h:h + 1, :] = jnp.broadcast_to(m_t, (1, LANES))


def _mlstm_sample(q, k, v, so, ig, fg, c0, n0, m0):
    nb = q.shape[0]

    def lanes(a):
        return jnp.broadcast_to(a[:, :, None], (nb, H_M, LANES))

    def per_b(*shape):
        return pl.BlockSpec((None,) + shape, lambda b: (b,) + (0,) * len(shape))

    return pl.pallas_call(
        _mlstm_sample_body,
        grid=(nb,),
        in_specs=[per_b(H_M, DK_M), per_b(DK_M, H_M), per_b(H_M, DK_M), per_b(DK_M, H_M),
                  per_b(H_M, DV_M), per_b(H_M, DV_M), per_b(H_M, LANES), per_b(H_M, LANES),
                  per_b(H_M, LANES), per_b(H_M, DK_M, DV_M), per_b(H_M, DK_M)],
        out_specs=[per_b(H_M, DV_M), per_b(H_M, DK_M, DV_M), per_b(H_M, DK_M), per_b(H_M, LANES)],
        out_shape=[jax.ShapeDtypeStruct((nb, H_M, DV_M), F32),
                   jax.ShapeDtypeStruct((nb, H_M, DK_M, DV_M), F32),
                   jax.ShapeDtypeStruct((nb, H_M, DK_M), F32),
                   jax.ShapeDtypeStruct((nb, H_M, LANES), F32)],
        compiler_params=_params("parallel"),
        name="mlstm_sample",
    )(q, jnp.swapaxes(q, 1, 2), k, jnp.swapaxes(k, 1, 2), v, so, lanes(ig), lanes(fg), lanes(m0),
      c0, n0)


def _alibi_slopes():
    n = N_GROUPS * H_A
    e = np.arange(1, n + 1, dtype=np.float64) * (-ALIBI_MAX_EXP / n)
    return np.exp2(e).reshape(N_GROUPS, H_A)


def _softmax_mix(lses, outs):
    m = jnp.maximum(jnp.maximum(lses[0], lses[1]), lses[2])
    e = [jnp.exp(x - m) for x in lses]
    z = e[0] + e[1] + e[2]
    acc = (e[0] / z) * outs[0]
    for g in (1, 2):
        acc = acc + (e[g] / z) * outs[g]
    return acc


def _attn_prompt_body(slopes_ref, *refs):
    G = N_GROUPS
    q_refs, kc_refs, kp_refs = refs[0:G], refs[G:2 * G], refs[2 * G:3 * G]
    vc_refs, vp_refs = refs[3 * G:4 * G], refs[4 * G:5 * G]
    o_ref = refs[5 * G]
    o_scr, l_scr = refs[5 * G + 1:6 * G + 1], refs[6 * G + 1:7 * G + 1]
    L = Q_BLOCK
    head = pl.program_id(1)
    first_key = jnp.where(pl.program_id(2) == 0, L, 0)
    t_idx = lax.broadcasted_iota(jnp.int32, (L, 2 * L), 0)
    s_idx = lax.broadcasted_iota(jnp.int32, (L, 2 * L), 1)
    step = t_idx + L - s_idx
    in_window = jnp.logical_and(step >= 0, step <= L)
    in_window_first = jnp.logical_and(in_window, s_idx >= first_key)
    nt = (((1,), (1,)), ((), ()))

    def rows(start, size, dil):
        return pl.ds(start, size) if dil == 1 else pl.ds(start, size, stride=dil)

    for g in range(G):
        dil = DILATIONS[g]
        bias = step.astype(F32) * (slopes_ref[g, head] * dil)

        def sub_block(j, r, g=g, dil=dil, bias=bias):
            start = j * L * dil + r
            q = q_refs[g][rows(start, L, dil), :].astype(BF16)
            if j == 0:
                k2 = jnp.concatenate([kp_refs[g][rows(r, L, dil), :], kc_refs[g][rows(start, L, dil), :]], axis=0)
                v2 = jnp.concatenate([vp_refs[g][rows(r, L, dil), :], vc_refs[g][rows(start, L, dil), :]], axis=0)
                mask = in_window_first
            else:
                k2 = kc_refs[g][rows(start - L * dil, 2 * L, dil), :]
                v2 = vc_refs[g][rows(start - L * dil, 2 * L, dil), :]
                mask = in_window
            s = lax.dot_general(q, k2.astype(BF16), nt, preferred_element_type=F32)
            s = jnp.where(mask, s * (DH_A ** -0.5) - bias, NEG_INF)
            m = jnp.max(s, axis=1, keepdims=True)
            p = jnp.exp(s - m)
            l = jnp.sum(p, axis=1, keepdims=True)
            o = jnp.dot(p.astype(BF16), v2.astype(BF16), preferred_element_type=F32) / l
            o_scr[g][rows(start, L, dil), :] = o
            l_scr[g][rows(start, L, dil), :] = jnp.broadcast_to(m + jnp.log(l), (L, LANES))

        for j in range(ATT_SB // (L * dil)):
            if dil == 1:
                sub_block(j, 0)
            else:
                def residue(r, carry, j=j):
                    sub_block(j, r)
                    return carry
                lax.fori_loop(0, dil, residue, 0, unroll=16)

    chunk = 256
    for c in range(0, ATT_SB, chunk):
        lses = [l_scr[g][c:c + chunk, :] for g in range(G)]
        outs = [o_scr[g][c:c + chunk, :] for g in range(G)]
        o_ref[c:c + chunk, :] = _softmax_mix(lses, outs).astype(o_ref.dtype)


def _attn_prompt(qkv_a):
    ns = SEQ // ATT_SB
    L = Q_BLOCK
    heads_per_part = N_GROUPS * H_A

    def cur(part, g):
        return pl.BlockSpec((ATT_SB, DH_A), lambda b, h, i: (b * ns + i, part * heads_per_part + g * H_A + h))

    def prev(part, g):
        span = L * DILATIONS[g]
        f = ATT_SB // span
        return pl.BlockSpec((span, DH_A),
                            lambda b, h, i: (b * ns * f + jnp.maximum(i * f - 1, 0),
                                             part * heads_per_part + g * H_A + h))

    groups = range(N_GROUPS)
    in_specs = ([pl.BlockSpec(memory_space=pltpu.SMEM)]
                + [cur(0, g) for g in groups] + [cur(1, g) for g in groups] + [prev(1, g) for g in groups]
                + [cur(2, g) for g in groups] + [prev(2, g) for g in groups])
    slopes = jnp.asarray(_alibi_slopes(), F32)
    return pl.pallas_call(
        _attn_prompt_body,
        grid=(BATCH, H_A, ns),
        in_specs=in_specs,
        out_specs=pl.BlockSpec((ATT_SB, DH_A), lambda b, h, i: (b * ns + i, h)),
        out_shape=jax.ShapeDtypeStruct((M_ROWS, A_GROUP_COLS), BF16),
        scratch_shapes=[pltpu.VMEM((ATT_SB, DH_A), F32)] * (2 * N_GROUPS),
        compiler_params=_params("parallel", "parallel", "arbitrary"),
        name="attn_prompt",
    )(slopes, *([qkv_a] * (5 * N_GROUPS)))


def _attn_sample_body(slopes_ref, q_ref, kn_ref, vn_ref, c0_ref, c1_ref, c2_ref, o_ref):
    L = Q_BLOCK
    c_refs = (c0_ref, c1_ref, c2_ref)
    steps = (L - lax.broadcasted_iota(jnp.int32, (L, 1, 1), 0)).astype(F32)
    outs, lses = [], []
    for g in range(N_GROUPS):
        q = q_ref[g]
        slope = slopes_ref[g][:, 0:1]
        s = (jnp.sum(c_refs[g][:, 0] * q[None], axis=-1, keepdims=True) * (DH_A ** -0.5)
             - (slope * DILATIONS[g])[None] * steps)
        s_new = jnp.sum(q * kn_ref[g], axis=-1, keepdims=True) * (DH_A ** -0.5)
        m = jnp.maximum(jnp.max(s, axis=0), s_new)
        p = jnp.exp(s - m[None])
        p_new = jnp.exp(s_new - m)
        l = jnp.sum(p, axis=0) + p_new
        outs.append((jnp.sum(p * c_refs[g][:, 1], axis=0) + p_new * vn_ref[g]) / l)
        lses.append(m + jnp.log(l))
    o_ref[...] = _softmax_mix(lses, outs)


def _attn_sample(q, kn, vn, caches):
    nb = q.shape[0]
    small = pl.BlockSpec((None, N_GROUPS, H_A, DH_A), lambda b: (b, 0, 0, 0))
    cache_specs, cache_views = [], []
    for g, c in enumerate(caches):
        dil = DILATIONS[g]
        assert c.shape[1] == Q_BLOCK * dil
        cache_views.append(c.reshape(nb, Q_BLOCK, dil, 2, H_A, DH_A))
        cache_specs.append(pl.BlockSpec((None, Q_BLOCK, None, 2, H_A, DH_A), lambda b: (b, 0, 0, 0, 0, 0)))
    slopes = jnp.broadcast_to(jnp.asarray(_alibi_slopes(), F32)[:, :, None], (N_GROUPS, H_A, LANES))
    return pl.pallas_call(
        _attn_sample_body,
        grid=(nb,),
        in_specs=[pl.BlockSpec((N_GROUPS, H_A, LANES), lambda b: (0, 0, 0)), small, small, small] + cache_specs,
        out_specs=pl.BlockSpec((None, H_A, DH_A), lambda b: (b, 0, 0)),
        out_shape=jax.ShapeDtypeStruct((nb, H_A, DH_A), F32),
        compiler_params=_params("parallel"),
        name="attn_sample",
    )(slopes, q, kn, vn, *cache_views)


def _logistic(x):
    return 0.5 * jnp.tanh(0.5 * x) + 0.5


def _sigmoid_epilogue(prods, extras):
    return _logistic(prods[0])


def _gate_epilogue(prods, extras):
    return _logistic(prods[0] + extras[0])


def _plain_epilogue(prods, extras):
    return prods[0]


def _mix_epilogue(prods, extras):
    return extras[0].astype(F32) * prods[0] + extras[1].astype(F32) * prods[1]


def _residual_epilogue(prods, extras):
    return extras[0] + prods[0]


def _with_sample_rows(prompt_rows_array, sample_rows):
    nb, c = sample_rows.shape
    block = jnp.concatenate([sample_rows, jnp.zeros((SAMPLE_PAD - nb, c), sample_rows.dtype)], axis=0)
    return lax.dynamic_update_slice(prompt_rows_array, block.astype(prompt_rows_array.dtype), (N_PROMPT, 0))


def kernel(x_prompt, x_sample, state_mlstm_C, state_mlstm_n, state_mlstm_m, cache_win1_kv, cache_win2_kv, cache_win3_kv, g_ffn1, w_ffn1_in, w_ffn1_out, g_mix, w_in, b_gate, b_igate, b_fgate, w_mlstm_out, w_attn_out, w_out, g_ffn2, w_ffn2_in, w_ffn2_out, g_final):
    d = D_MODEL
    nb = DEC_BATCH
    xp, xs = x_prompt.reshape(N_PROMPT, d), x_sample.reshape(nb, d)

    w_in_nk = w_in[0].T
    gate_bias = jnp.pad(jnp.concatenate([b_igate[0], b_fgate[0]]), (0, LANES - 2 * H_M)).reshape(1, LANES)

    xn = _rmsnorm_inputs(xp, xs, g_ffn1[0])
    y = _swiglu(xn, w_ffn1_in[0], w_ffn1_out[0])
    x1, u = _add_rmsnorm_inputs(xp, xs, y, g_mix[0], 0.5)

    qkv_m, w_mo = _mm("proj_qkv_m", [u], [(0, w_in_nk, 0, True)], [], COL_O_M, BF16, _plain_epilogue,
                      side=[(w_mlstm_out[0], 512, 512)])
    sig_o, w_ao = _mm("proj_o_m", [u], [(0, w_in_nk, COL_O_M, True)], [], V_M_COLS, BF16, _sigmoid_epilogue,
                      side=[(w_attn_out[0], 256, 512)])
    igfg = _mm("proj_if", [u], [(0, w_in_nk, COL_IF, True)], [], LANES, F32, _plain_epilogue, tn=LANES)
    qkv_a = _mm("proj_qkv_a", [u], [(0, w_in_nk, COL_Q_A, True)], [], 3 * A_COLS, F32, _plain_epilogue)
    gates, w_o = _mm("proj_gates", [u], [(0, w_in_nk, COL_GATE, True)],
                     [(b_gate[0].reshape(1, 2 * d), "row", 0)], 2 * d, BF16, _gate_epilogue,
                     side=[(w_out[0], 512, 512)])

    hm, c_p, n_p, m_p = _mlstm_prompt(qkv_m, sig_o, igfg, gate_bias)
    qkv_s = qkv_m[N_PROMPT:N_PROMPT + nb].astype(F32)
    if_s = igfg[N_PROMPT:N_PROMPT + nb] + gate_bias
    hm_s, c_s, n_s, m_s = _mlstm_sample(
        qkv_s[:, :QK_M_COLS].reshape(nb, H_M, DK_M),
        qkv_s[:, QK_M_COLS:2 * QK_M_COLS].reshape(nb, H_M, DK_M),
        qkv_s[:, 2 * QK_M_COLS:].reshape(nb, H_M, DV_M),
        sig_o[N_PROMPT:N_PROMPT + nb].astype(F32).reshape(nb, H_M, DV_M),
        if_s[:, :H_M], if_s[:, H_M:2 * H_M],
        state_mlstm_C[0], state_mlstm_n[0], state_mlstm_m[0])
    hm = _with_sample_rows(hm, hm_s.reshape(nb, V_M_COLS))

    ha = _attn_prompt(qkv_a)
    qkv_as = qkv_a[N_PROMPT:N_PROMPT + nb].reshape(nb, 3, N_GROUPS, H_A, DH_A)
    ha_s = _attn_sample(qkv_as[:, 0], qkv_as[:, 1], qkv_as[:, 2],
                        (cache_win1_kv[0], cache_win2_kv[0], cache_win3_kv[0]))
    ha = _with_sample_rows(ha, ha_s.reshape(nb, A_GROUP_COLS))

    z = _mm("mix", [hm, ha], [(0, w_mo, 0, False), (1, w_ao, 0, False)],
            [(gates, "tile", 0), (gates, "tile", d)], d, BF16, _mix_epilogue)
    x2 = _mm("proj_out", [z], [(0, w_o, 0, False)], [(x1, "tile", 0)], d, F32, _residual_epilogue)

    xn2 = _rmsnorm(x2, g_ffn2[0], BF16)
    y2 = _swiglu(xn2, w_ffn2_in[0], w_ffn2_out[0])
    out_p, out_s = _add_final_norm(x2, y2, g_final, 0.5)

    win_p, win_s = [], []
    for g, w in enumerate(WINDOWS):
        n_last = min(w, SEQ)
        k0, v0 = A_COLS + g * A_GROUP_COLS, 2 * A_COLS + g * A_GROUP_COLS
        per_seq = []
        for b in range(BATCH):
            last = qkv_a[(b + 1) * SEQ - n_last:(b + 1) * SEQ]
            per_seq.append(jnp.stack([last[:, k0:k0 + A_GROUP_COLS], last[:, v0:v0 + A_GROUP_COLS]], axis=1))
        win_p.append(jnp.stack(per_seq).reshape(1, BATCH, n_last, 2, H_A, DH_A))
        win_s.append(qkv_as[:, 1:3, g].reshape(1, nb, 1, 2, H_A, DH_A))
    return (out_p.reshape(BATCH, SEQ, d), out_s[:nb].reshape(nb, 1, d),
            c_p[None], n_p[None], m_p[None],
            c_s[None], n_s[None], m_s[:, :, 0][None],
            win_p[0], win_p[1], win_p[2], win_s[0], win_s[1], win_s[2])
```

```python
import functools

import numpy as np
import jax
import jax.numpy as jnp
from jax import lax
from jax.experimental import pallas as pl
from jax.experimental.pallas import tpu as pltpu

F32 = jnp.float32
BF16 = jnp.bfloat16

D_MODEL = 4096
BATCH = 2
SEQ = 4096
DEC_BATCH = 32
H_M = 8
DK_M = 256
DV_M = 512
MLSTM_CHUNK = 128
WINDOWS = (128, 512, 2048)
DILATIONS = (1, 4, 16)
N_GROUPS = 3
H_A = 8
DH_A = 128
Q_BLOCK = 128
ALIBI_MAX_EXP = 8.0
D_FF = 11008
EPS = 1e-6
NEG_INF = -1e30

QK_M_COLS = H_M * DK_M
V_M_COLS = H_M * DV_M
A_GROUP_COLS = H_A * DH_A
A_COLS = N_GROUPS * A_GROUP_COLS
COL_O_M = 2 * QK_M_COLS + V_M_COLS
COL_IF = COL_O_M + V_M_COLS
COL_Q_A = COL_IF + 2 * H_M
COL_GATE = COL_Q_A + 3 * A_COLS
LANES = 128
SUBLANES = 8

N_PROMPT = BATCH * SEQ
SAMPLE_PAD = 64
M_ROWS = N_PROMPT + SAMPLE_PAD

V7X_VMEM_BYTES = 64 * 1024 * 1024
VMEM_LIMIT_BYTES = V7X_VMEM_BYTES - 8 * 1024 * 1024
FFN_VMEM_LIMIT_BYTES = V7X_VMEM_BYTES - 2 * 1024 * 1024

ROW_TILE = 192
OUT_ROW_TILE = 256
MM_TM = 1376
MM_TN = 512
FFN_TM = 1376
FFN_TF = 256
FFN_TN = 512
ATT_SB = 2048


def _params(*semantics, vmem_limit_bytes=VMEM_LIMIT_BYTES):
    return pltpu.CompilerParams(dimension_semantics=semantics, vmem_limit_bytes=vmem_limit_bytes)


def _rms(x, g):
    ms = jnp.mean(x * x, axis=-1, keepdims=True)
    return x * lax.rsqrt(ms + EPS) * g


def _rmsnorm_body(x_ref, g_ref, o_ref):
    o_ref[...] = _rms(x_ref[...], g_ref[...]).astype(o_ref.dtype)


def _rmsnorm(x, g, out_dtype):
    m, d = x.shape
    return pl.pallas_call(
        _rmsnorm_body,
        grid=(m // ROW_TILE,),
        in_specs=[pl.BlockSpec((ROW_TILE, d), lambda i: (i, 0)),
                  pl.BlockSpec((1, d), lambda i: (0, 0))],
        out_specs=pl.BlockSpec((ROW_TILE, d), lambda i: (i, 0)),
        out_shape=jax.ShapeDtypeStruct((m, d), out_dtype),
        compiler_params=_params("parallel"),
        name="rmsnorm",
    )(x, g.reshape(1, d))


def _input_tile(xp_ref, xs_ref):
    is_sample_tile = pl.program_id(0) == N_PROMPT // OUT_ROW_TILE
    pad = jnp.zeros((OUT_ROW_TILE - xs_ref.shape[0], xs_ref.shape[1]), F32)
    return jnp.where(is_sample_tile, jnp.concatenate([xs_ref[...], pad], axis=0), xp_ref[...])


def _input_specs(d, nb):
    n_p = N_PROMPT // OUT_ROW_TILE
    return [pl.BlockSpec((OUT_ROW_TILE, d), lambda i: (jnp.minimum(i, n_p - 1), 0)),
            pl.BlockSpec((nb, d), lambda i: (0, 0))]


def _rmsnorm_inputs_body(xp_ref, xs_ref, g_ref, o_ref):
    o_ref[...] = _rms(_input_tile(xp_ref, xs_ref), g_ref[...]).astype(o_ref.dtype)


def _rmsnorm_inputs(xp, xs, g):
    d = xp.shape[1]
    row = pl.BlockSpec((OUT_ROW_TILE, d), lambda i: (i, 0))
    return pl.pallas_call(
        _rmsnorm_inputs_body,
        grid=(N_PROMPT // OUT_ROW_TILE + 1,),
        in_specs=_input_specs(d, xs.shape[0]) + [pl.BlockSpec((1, d), lambda i: (0, 0))],
        out_specs=row,
        out_shape=jax.ShapeDtypeStruct((M_ROWS, d), BF16),
        compiler_params=_params("parallel"),
        name="rmsnorm_inputs",
    )(xp, xs, g.reshape(1, d))


def _add_rmsnorm_inputs_body(xp_ref, xs_ref, y_ref, g_ref, xo_ref, no_ref, *, scale):
    x = _input_tile(xp_ref, xs_ref) + scale * y_ref[...]
    xo_ref[...] = x
    no_ref[...] = _rms(x, g_ref[...]).astype(no_ref.dtype)


def _add_rmsnorm_inputs(xp, xs, y, g, scale):
    d = xp.shape[1]
    row = pl.BlockSpec((OUT_ROW_TILE, d), lambda i: (i, 0))
    return pl.pallas_call(
        functools.partial(_add_rmsnorm_inputs_body, scale=scale),
        grid=(N_PROMPT // OUT_ROW_TILE + 1,),
        in_specs=_input_specs(d, xs.shape[0]) + [row, pl.BlockSpec((1, d), lambda i: (0, 0))],
        out_specs=[row, row],
        out_shape=[jax.ShapeDtypeStruct((M_ROWS, d), F32), jax.ShapeDtypeStruct((M_ROWS, d), BF16)],
        compiler_params=_params("parallel"),
        name="add_rmsnorm_inputs",
    )(xp, xs, y, g.reshape(1, d))


def _add_final_norm_body(x_ref, y_ref, g_ref, op_ref, os_ref, *, scale):
    out = _rms(x_ref[...] + scale * y_ref[...], g_ref[...])
    is_sample_tile = pl.program_id(0) == N_PROMPT // OUT_ROW_TILE

    @pl.when(jnp.logical_not(is_sample_tile))
    def _():
        op_ref[...] = out

    @pl.when(is_sample_tile)
    def _():
        os_ref[...] = out[:SAMPLE_PAD]


def _add_final_norm(x, y, g, scale):
    m, d = x.shape
    n_p = N_PROMPT // OUT_ROW_TILE
    row = pl.BlockSpec((OUT_ROW_TILE, d), lambda i: (i, 0))
    return pl.pallas_call(
        functools.partial(_add_final_norm_body, scale=scale),
        grid=(n_p + 1,),
        in_specs=[row, row, pl.BlockSpec((1, d), lambda i: (0, 0))],
        out_specs=[pl.BlockSpec((OUT_ROW_TILE, d), lambda i: (jnp.minimum(i, n_p - 1), 0)),
                   pl.BlockSpec((SAMPLE_PAD, d), lambda i: (0, 0))],
        out_shape=[jax.ShapeDtypeStruct((N_PROMPT, d), F32), jax.ShapeDtypeStruct((SAMPLE_PAD, d), F32)],
        compiler_params=_params("arbitrary"),
        name="add_final_norm",
    )(x, y, g.reshape(1, d))


def _side_cast_specs(side, n_inner):
    in_specs, out_specs, out_shapes = [], [], []
    for src, rb, cb in side:
        r, c = src.shape
        n_cb = c // cb
        last = (r // rb) * n_cb - 1

        def index(i, j, n_cb=n_cb, last=last):
            blk = jnp.minimum(i * n_inner + j, last)
            return lax.div(blk, n_cb), lax.rem(blk, n_cb)

        in_specs.append(pl.BlockSpec((rb, cb), index))
        out_specs.append(pl.BlockSpec((rb, cb), index))
        out_shapes.append(jax.ShapeDtypeStruct((r, c), BF16))
    return in_specs, out_specs, out_shapes


def _side_cast(src_refs, dst_refs):
    for src_ref, dst_ref in zip(src_refs, dst_refs):
        dst_ref[...] = src_ref[...].astype(dst_ref.dtype)


def _mm_body(*refs, dots, n_a, n_extra, n_side, epilogue):
    a_refs = refs[:n_a]
    pos = n_a
    weights = []
    for ai, w_is_nk, shift in dots:
        if shift:
            w = jnp.concatenate([refs[pos][shift:, :].astype(BF16), refs[pos + 1][...].astype(BF16)], axis=0)
            pos += 2
        else:
            w = refs[pos][...].astype(BF16)
            pos += 1
        weights.append(w)
    n_in = pos + n_extra
    e_refs = refs[pos:n_in]
    o_ref = refs[n_in + n_side]
    prods = []
    for (ai, w_is_nk, _), w in zip(dots, weights):
        contract = (((1,), (1,)), ((), ())) if w_is_nk else (((1,), (0,)), ((), ()))
        prods.append(lax.dot_general(a_refs[ai][...], w, contract, preferred_element_type=F32))
    o_ref[...] = epilogue(prods, [e[...] for e in e_refs]).astype(o_ref.dtype)
    _side_cast(refs[n_in:n_in + n_side], refs[n_in + n_side + 1:])


def _mm(name, a_list, w_list, extras, n, out_dtype, epilogue, tm=MM_TM, tn=MM_TN, side=()):
    m = a_list[0].shape[0]
    in_specs = [pl.BlockSpec((tm, a.shape[1]), lambda i, j: (i, 0)) for a in a_list]
    operands = list(a_list)
    dots = []
    for ai, w, off, w_is_nk in w_list:
        shift = off % tn if w_is_nk else 0
        if w_is_nk:
            in_specs.append(pl.BlockSpec((tn, w.shape[1]), lambda i, j, o=off // tn: (j + o, 0)))
        else:
            in_specs.append(pl.BlockSpec((w.shape[0], tn), lambda i, j, o=off // tn: (0, j + o)))
        operands.append(w)
        if shift:
            in_specs.append(pl.BlockSpec((shift, w.shape[1]),
                                         lambda i, j, o=off // tn, r=tn // shift: ((j + o + 1) * r, 0)))
            operands.append(w)
        dots.append((ai, w_is_nk, shift))
    for arr, kind, off in extras:
        if kind == "tile":
            in_specs.append(pl.BlockSpec((tm, tn), lambda i, j, o=off // tn: (i, j + o)))
        else:
            in_specs.append(pl.BlockSpec((1, tn), lambda i, j, o=off // tn: (0, j + o)))
        operands.append(arr)
    side_in, side_out, side_shapes = _side_cast_specs(side, n // tn)
    body = functools.partial(_mm_body, dots=tuple(dots),
                             n_a=len(a_list), n_extra=len(extras), n_side=len(side), epilogue=epilogue)
    outs = pl.pallas_call(
        body,
        grid=(m // tm, n // tn),
        in_specs=in_specs + side_in,
        out_specs=[pl.BlockSpec((tm, tn), lambda i, j: (i, j))] + side_out,
        out_shape=[jax.ShapeDtypeStruct((m, n), out_dtype)] + side_shapes,
        compiler_params=_params("arbitrary", "arbitrary"),
        name=name,
    )(*operands, *[src for src, _, _ in side])
    return outs if side else outs[0]


def _ffn_body(x_ref, wg_ref, wu_ref, wo_ref, o_ref):
    x = x_ref[...]
    g = jnp.dot(x, wg_ref[...].astype(BF16), preferred_element_type=F32)
    u = jnp.dot(x, wu_ref[...].astype(BF16), preferred_element_type=F32)
    h = (g * jax.nn.sigmoid(g) * u).astype(BF16)

    @pl.when(pl.program_id(1) == 0)
    def _():
        o_ref[...] = jnp.zeros_like(o_ref)

    for c in range(0, o_ref.shape[1], FFN_TN):
        o_ref[:, c:c + FFN_TN] += jnp.dot(h, wo_ref[:, c:c + FFN_TN].astype(BF16),
                                          preferred_element_type=F32)


def _swiglu(xn, w_in, w_out):
    m, d = xn.shape
    nf = D_FF // FFN_TF
    return pl.pallas_call(
        _ffn_body,
        grid=(m // FFN_TM, nf),
        in_specs=[pl.BlockSpec((FFN_TM, d), lambda i, f: (i, 0), pipeline_mode=pl.Buffered(1)),
                  pl.BlockSpec((d, FFN_TF), lambda i, f: (0, f)),
                  pl.BlockSpec((d, FFN_TF), lambda i, f: (0, f + nf)),
                  pl.BlockSpec((FFN_TF, d), lambda i, f: (f, 0))],
        out_specs=pl.BlockSpec((FFN_TM, d), lambda i, f: (i, 0), pipeline_mode=pl.Buffered(1)),
        out_shape=jax.ShapeDtypeStruct((m, d), F32),
        compiler_params=_params("parallel", "arbitrary", vmem_limit_bytes=FFN_VMEM_LIMIT_BYTES),
        name="swiglu",
    )(xn, w_in, w_in, w_out)


def _log_sigmoid(x):
    return jnp.minimum(x, 0.0) - jnp.log1p(jnp.exp(-jnp.abs(x)))


def _mlstm_prompt_body(q_ref, k_ref, v_ref, so_ref, if_ref, bias_ref,
                       h_ref, c_ref, n_ref, m_ref):
    L = MLSTM_CHUNK

    @pl.when(pl.program_id(1) == 0)
    def _():
        c_ref[...] = jnp.zeros_like(c_ref)
        n_ref[...] = jnp.zeros_like(n_ref)
        m_ref[...] = jnp.zeros_like(m_ref)

    x = if_ref[...] + bias_ref[...]
    xt = x.T
    t_idx = lax.broadcasted_iota(jnp.int32, (L, L), 0)
    s_idx = lax.broadcasted_iota(jnp.int32, (L, L), 1)
    causal = s_idx <= t_idx
    b_cols = jnp.dot(causal.astype(F32), _log_sigmoid(x), precision=lax.Precision.HIGHEST,
                     preferred_element_type=F32)
    b_rows = b_cols.T
    for h in range(H_M):
        ig_col = x[:, h:h + 1]
        ig_row = xt[h:h + 1, :]
        b_col = b_cols[:, H_M + h:H_M + h + 1]
        b_row = b_rows[H_M + h:H_M + h + 1, :]
        m_prev = m_ref[h, 0:1, 0:1]
        d = jnp.where(causal, b_col - b_row + ig_row, -jnp.inf)
        m_inter = b_col + m_prev
        m_t = jnp.maximum(m_inter, jnp.max(d, axis=1, keepdims=True))
        w_inter = jnp.exp(m_inter - m_t)
        m_new = m_t[L - 1:L, :]
        b_last = b_col[L - 1:L, :]
        w_end = jnp.exp(b_last - b_col + ig_col - m_new)
        decay = jnp.exp(b_last + m_prev - m_new)

        qh = q_ref[:, h * DK_M:(h + 1) * DK_M]
        kf = k_ref[:, h * DK_M:(h + 1) * DK_M].astype(F32) * (DK_M ** -0.5)
        s = lax.dot_general(qh, kf.astype(BF16), (((1,), (1,)), ((), ())),
                            preferred_element_type=F32)
        c_old = c_ref[h]
        qc = jnp.dot(qh, c_old.astype(BF16), preferred_element_type=F32)
        n_tile = n_ref[h]
        n_old = n_tile[0:1, :]
        qn = lax.dot_general(qh, n_tile.astype(BF16), (((1,), (1,)), ((), ())),
                             preferred_element_type=F32)[:, 0:1]
        kw = kf * w_end
        a = s * jnp.exp(d - m_t)
        vh = v_ref[:, h * DV_M:(h + 1) * DV_M]
        num = jnp.dot(a.astype(BF16), vh, preferred_element_type=F32) + w_inter * qc
        den = jnp.sum(a, axis=1, keepdims=True) + w_inter * qn
        hh = num / jnp.maximum(jnp.abs(den), jnp.exp(-m_t))
        gate = so_ref[:, h * DV_M:(h + 1) * DV_M].astype(F32)
        h_ref[:, h * DV_M:(h + 1) * DV_M] = (gate * hh).astype(h_ref.dtype)

        dc = jnp.dot(kw.T.astype(BF16), vh, preferred_element_type=F32)
        c_ref[h] = decay * c_old + dc
        n_new = decay * n_old + jnp.sum(kw, axis=0, keepdims=True)
        n_ref[h] = jnp.broadcast_to(n_new, n_ref.shape[1:])
        m_ref[h] = jnp.broadcast_to(m_new, m_ref.shape[1:])


def _mlstm_prompt(qkv_m, sig_o, igfg, gate_bias):
    nc = SEQ // MLSTM_CHUNK
    L = MLSTM_CHUNK

    def rows(width, col):
        return pl.BlockSpec((L, width), lambda b, c: (b * nc + c, col))

    hm, c, n, m = pl.pallas_call(
        _mlstm_prompt_body,
        grid=(BATCH, nc),
        in_specs=[rows(QK_M_COLS, 0), rows(QK_M_COLS, 1), rows(V_M_COLS, 1), rows(V_M_COLS, 0),
                  rows(LANES, 0), pl.BlockSpec((1, LANES), lambda b, c: (0, 0))],
        out_specs=[rows(V_M_COLS, 0),
                   pl.BlockSpec((None, H_M, DK_M, DV_M), lambda b, c: (b, 0, 0, 0)),
                   pl.BlockSpec((None, H_M, SUBLANES, DK_M), lambda b, c: (b, 0, 0, 0)),
                   pl.BlockSpec((None, H_M, SUBLANES, LANES), lambda b, c: (b, 0, 0, 0))],
        out_shape=[jax.ShapeDtypeStruct((M_ROWS, V_M_COLS), BF16),
                   jax.ShapeDtypeStruct((BATCH, H_M, DK_M, DV_M), F32),
                   jax.ShapeDtypeStruct((BATCH, H_M, SUBLANES, DK_M), F32),
                   jax.ShapeDtypeStruct((BATCH, H_M, SUBLANES, LANES), F32)],
        compiler_params=_params("parallel", "arbitrary"),
        name="mlstm_prompt",
    )(qkv_m, qkv_m, qkv_m, sig_o, igfg, gate_bias)
    return hm, c, n[:, :, 0, :], m[:, :, 0, 0]


def _mlstm_sample_body(q_ref, qt_ref, k_ref, kt_ref, v_ref, so_ref, ig_ref, fg_ref, m0_ref,
                       c0_ref, n0_ref, h_ref, c_ref, n_ref, m_ref):
    for h in range(H_M):
        ig = ig_ref[h:h + 1, 0:1]
        lf = _log_sigmoid(fg_ref[h:h + 1, 0:1])
        m_prev = m0_ref[h:h + 1, 0:1]
        m_inter = lf + m_prev
        m_t = jnp.maximum(m_inter, ig)
        w_inter = jnp.exp(m_inter - m_t)
        q_row = q_ref[h:h + 1, :]
        k_row = k_ref[h:h + 1, :] * (DK_M ** -0.5)
        q_col = qt_ref[:, h:h + 1]
        k_col = kt_ref[:, h:h + 1] * (DK_M ** -0.5)
        v_row = v_ref[h:h + 1, :]
        c_old = c0_ref[h]
        n_old = n0_ref[h:h + 1, :]
        a = jnp.sum(q_row * k_row, axis=1, keepdims=True) * jnp.exp(ig - m_t)
        num = a * v_row + w_inter * jnp.sum(q_col * c_old, axis=0, keepdims=True)
        den = a + w_inter * jnp.sum(q_row * n_old, axis=1, keepdims=True)
        hh = num / jnp.maximum(jnp.abs(den), jnp.exp(-m_t))
        h_ref[h:h + 1, :] = so_ref[h:h + 1, :] * hh
        w_end = jnp.exp(ig - m_t)
        decay = jnp.exp(m_inter - m_t)
        c_ref[h] = decay * c_old + (k_col * w_end) * v_row
        n_ref[h:h + 1, :] = decay * n_old + k_row * w_end
        m_ref[h:h + 1, :] = jnp.broadcast_to(m_t, (1, LANES))


def _mlstm_sample(q, k, v, so, ig, fg, c0, n0, m0):
    nb = q.shape[0]

    def lanes(a):
        return jnp.broadcast_to(a[:, :, None], (nb, H_M, LANES))

    def per_b(*shape):
        return pl.BlockSpec((None,) + shape, lambda b: (b,) + (0,) * len(shape))

    return pl.pallas_call(
        _mlstm_sample_body,
        grid=(nb,),
        in_specs=[per_b(H_M, DK_M), per_b(DK_M, H_M), per_b(H_M, DK_M), per_b(DK_M, H_M),
                  per_b(H_M, DV_M), per_b(H_M, DV_M), per_b(H_M, LANES), per_b(H_M, LANES),
                  per_b(H_M, LANES), per_b(H_M, DK_M, DV_M), per_b(H_M, DK_M)],
        out_specs=[per_b(H_M, DV_M), per_b(H_M, DK_M, DV_M), per_b(H_M, DK_M), per_b(H_M, LANES)],
        out_shape=[jax.ShapeDtypeStruct((nb, H_M, DV_M), F32),
                   jax.ShapeDtypeStruct((nb, H_M, DK_M, DV_M), F32),
                   jax.ShapeDtypeStruct((nb, H_M, DK_M), F32),
                   jax.ShapeDtypeStruct((nb, H_M, LANES), F32)],
        compiler_params=_params("parallel"),
        name="mlstm_sample",
    )(q, jnp.swapaxes(q, 1, 2), k, jnp.swapaxes(k, 1, 2), v, so, lanes(ig), lanes(fg), lanes(m0),
      c0, n0)


def _alibi_slopes():
    n = N_GROUPS * H_A
    e = np.arange(1, n + 1, dtype=np.float64) * (-ALIBI_MAX_EXP / n)
    return np.exp2(e).reshape(N_GROUPS, H_A)


def _softmax_mix(lses, outs):
    m = jnp.maximum(jnp.maximum(lses[0], lses[1]), lses[2])
    e = [jnp.exp(x - m) for x in lses]
    z = e[0] + e[1] + e[2]
    acc = (e[0] / z) * outs[0]
    for g in (1, 2):
        acc = acc + (e[g] / z) * outs[g]
    return acc


def _attn_prompt_body(slopes_ref, *refs):
    G = N_GROUPS
    q_refs, kc_refs, kp_refs = refs[0:G], refs[G:2 * G], refs[2 * G:3 * G]
    vc_refs, vp_refs = refs[3 * G:4 * G], refs[4 * G:5 * G]
    o_ref = refs[5 * G]
    o_scr, l_scr = refs[5 * G + 1:6 * G + 1], refs[6 * G + 1:7 * G + 1]
    L = Q_BLOCK
    head = pl.program_id(1)
    first_key = jnp.where(pl.program_id(2) == 0, L, 0)
    t_idx = lax.broadcasted_iota(jnp.int32, (L, 2 * L), 0)
    s_idx = lax.broadcasted_iota(jnp.int32, (L, 2 * L), 1)
    step = t_idx + L - s_idx
    in_window = jnp.logical_and(step >= 0, step <= L)
    in_window_first = jnp.logical_and(in_window, s_idx >= first_key)
    nt = (((1,), (1,)), ((), ()))

    def rows(start, size, dil):
        return pl.ds(start, size) if dil == 1 else pl.ds(start, size, stride=dil)

    for g in range(G):
        dil = DILATIONS[g]
        alibi = step.astype(F32) * (slopes_ref[g, head] * dil)
        bias_in = jnp.where(in_window, alibi, -NEG_INF)
        bias_first = jnp.where(in_window_first, alibi, -NEG_INF)

        def sub_block(j, r, g=g, dil=dil, bias_in=bias_in, bias_first=bias_first):
            start = j * L * dil + r
            q = q_refs[g][rows(start, L, dil), :].astype(BF16)
            if j == 0:
                k2 = jnp.concatenate([kp_refs[g][rows(r, L, dil), :], kc_refs[g][rows(start, L, dil), :]], axis=0)
                v2 = jnp.concatenate([vp_refs[g][rows(r, L, dil), :], vc_refs[g][rows(start, L, dil), :]], axis=0)
                bias = bias_first
            else:
                k2 = kc_refs[g][rows(start - L * dil, 2 * L, dil), :]
                v2 = vc_refs[g][rows(start - L * dil, 2 * L, dil), :]
                bias = bias_in
            s = lax.dot_general(q, k2.astype(BF16), nt, preferred_element_type=F32)
            s = s * (DH_A ** -0.5) - bias
            m = jnp.max(s, axis=1, keepdims=True)
            p = jnp.exp(s - m)
            l = jnp.sum(p, axis=1, keepdims=True)
            o = jnp.dot(p.astype(BF16), v2.astype(BF16), preferred_element_type=F32) / l
            o_scr[g][rows(start, L, dil), :] = o
            l_scr[g][rows(start, L, dil), :] = jnp.broadcast_to(m + jnp.log(l), (L, LANES))

        for j in range(ATT_SB // (L * dil)):
            if dil == 1:
                sub_block(j, 0)
            else:
                def residue(r, carry, j=j):
                    sub_block(j, r)
                    return carry
                lax.fori_loop(0, dil, residue, 0, unroll=16)

    chunk = 256
    for c in range(0, ATT_SB, chunk):
        lses = [l_scr[g][c:c + chunk, :] for g in range(G)]
        outs = [o_scr[g][c:c + chunk, :] for g in range(G)]
        o_ref[c:c + chunk, :] = _softmax_mix(lses, outs).astype(o_ref.dtype)


def _attn_prompt(qkv_a):
    ns = SEQ // ATT_SB
    L = Q_BLOCK
    heads_per_part = N_GROUPS * H_A

    def cur(part, g):
        return pl.BlockSpec((ATT_SB, DH_A), lambda b, h, i: (b * ns + i, part * heads_per_part + g * H_A + h))

    def prev(part, g):
        span = L * DILATIONS[g]
        f = ATT_SB // span
        return pl.BlockSpec((span, DH_A),
                            lambda b, h, i: (b * ns * f + jnp.maximum(i * f - 1, 0),
                                             part * heads_per_part + g * H_A + h))

    groups = range(N_GROUPS)
    in_specs = ([pl.BlockSpec(memory_space=pltpu.SMEM)]
                + [cur(0, g) for g in groups] + [cur(1, g) for g in groups] + [prev(1, g) for g in groups]
                + [cur(2, g) for g in groups] + [prev(2, g) for g in groups])
    slopes = jnp.asarray(_alibi_slopes(), F32)
    return pl.pallas_call(
        _attn_prompt_body,
        grid=(BATCH, H_A, ns),
        in_specs=in_specs,
        out_specs=pl.BlockSpec((ATT_SB, DH_A), lambda b, h, i: (b * ns + i, h)),
        out_shape=jax.ShapeDtypeStruct((M_ROWS, A_GROUP_COLS), BF16),
        scratch_shapes=[pltpu.VMEM((ATT_SB, DH_A), F32)] * (2 * N_GROUPS),
        compiler_params=_params("parallel", "parallel", "arbitrary"),
        name="attn_prompt",
    )(slopes, *([qkv_a] * (5 * N_GROUPS)))


def _attn_sample_body(slopes_ref, q_ref, kn_ref, vn_ref, c0_ref, c1_ref, c2_ref, o_ref):
    L = Q_BLOCK
    c_refs = (c0_ref, c1_ref, c2_ref)
    steps = (L - lax.broadcasted_iota(jnp.int32, (L, 1, 1), 0)).astype(F32)
    outs, lses = [], []
    for g in range(N_GROUPS):
        q = q_ref[g]
        slope = slopes_ref[g][:, 0:1]
        s = (jnp.sum(c_refs[g][:, 0] * q[None], axis=-1, keepdims=True) * (DH_A ** -0.5)
             - (slope * DILATIONS[g])[None] * steps)
        s_new = jnp.sum(q * kn_ref[g], axis=-1, keepdims=True) * (DH_A ** -0.5)
        m = jnp.maximum(jnp.max(s, axis=0), s_new)
        p = jnp.exp(s - m[None])
        p_new = jnp.exp(s_new - m)
        l = jnp.sum(p, axis=0) + p_new
        outs.append((jnp.sum(p * c_refs[g][:, 1], axis=0) + p_new * vn_ref[g]) / l)
        lses.append(m + jnp.log(l))
    o_ref[...] = _softmax_mix(lses, outs)


def _attn_sample(q, kn, vn, caches):
    nb = q.shape[0]
    small = pl.BlockSpec((None, N_GROUPS, H_A, DH_A), lambda b: (b, 0, 0, 0))
    cache_specs, cache_views = [], []
    for g, c in enumerate(caches):
        dil = DILATIONS[g]
        assert c.shape[1] == Q_BLOCK * dil
        cache_views.append(c.reshape(nb, Q_BLOCK, dil, 2, H_A, DH_A))
        cache_specs.append(pl.BlockSpec((None, Q_BLOCK, None, 2, H_A, DH_A), lambda b: (b, 0, 0, 0, 0, 0)))
    slopes = jnp.broadcast_to(jnp.asarray(_alibi_slopes(), F32)[:, :, None], (N_GROUPS, H_A, LANES))
    return pl.pallas_call(
        _attn_sample_body,
        grid=(nb,),
        in_specs=[pl.BlockSpec((N_GROUPS, H_A, LANES), lambda b: (0, 0, 0)), small, small, small] + cache_specs,
        out_specs=pl.BlockSpec((None, H_A, DH_A), lambda b: (b, 0, 0)),
        out_shape=jax.ShapeDtypeStruct((nb, H_A, DH_A), F32),
        compiler_params=_params("parallel"),
        name="attn_sample",
    )(slopes, q, kn, vn, *cache_views)


def _logistic(x):
    return 0.5 * jnp.tanh(0.5 * x) + 0.5


def _sigmoid_epilogue(prods, extras):
    return _logistic(prods[0])


def _gate_epilogue(prods, extras):
    return _logistic(prods[0] + extras[0])


def _plain_epilogue(prods, extras):
    return prods[0]


def _mix_epilogue(prods, extras):
    return extras[0].astype(F32) * prods[0] + extras[1].astype(F32) * prods[1]


def _residual_epilogue(prods, extras):
    return extras[0] + prods[0]


def _with_sample_rows(prompt_rows_array, sample_rows):
    nb, c = sample_rows.shape
    block = jnp.concatenate([sample_rows, jnp.zeros((SAMPLE_PAD - nb, c), sample_rows.dtype)], axis=0)
    return lax.dynamic_update_slice(prompt_rows_array, block.astype(prompt_rows_array.dtype), (N_PROMPT, 0))


def kernel(x_prompt, x_sample, state_mlstm_C, state_mlstm_n, state_mlstm_m, cache_win1_kv, cache_win2_kv, cache_win3_kv, g_ffn1, w_ffn1_in, w_ffn1_out, g_mix, w_in, b_gate, b_igate, b_fgate, w_mlstm_out, w_attn_out, w_out, g_ffn2, w_ffn2_in, w_ffn2_out, g_final):
    d = D_MODEL
    nb = DEC_BATCH
    xp, xs = x_prompt.reshape(N_PROMPT, d), x_sample.reshape(nb, d)

    w_in_nk = w_in[0].T
    gate_bias = jnp.pad(jnp.concatenate([b_igate[0], b_fgate[0]]), (0, LANES - 2 * H_M)).reshape(1, LANES)

    xn = _rmsnorm_inputs(xp, xs, g_ffn1[0])
    y = _swiglu(xn, w_ffn1_in[0], w_ffn1_out[0])
    x1, u = _add_rmsnorm_inputs(xp, xs, y, g_mix[0], 0.5)

    qkv_m, w_mo = _mm("proj_qkv_m", [u], [(0, w_in_nk, 0, True)], [], COL_O_M, BF16, _plain_epilogue,
                      side=[(w_mlstm_out[0], 512, 512)])
    sig_o, w_ao = _mm("proj_o_m", [u], [(0, w_in_nk, COL_O_M, True)], [], V_M_COLS, BF16, _sigmoid_epilogue,
                      side=[(w_attn_out[0], 256, 512)])
    igfg = _mm("proj_if", [u], [(0, w_in_nk, COL_IF, True)], [], LANES, F32, _plain_epilogue, tn=LANES)
    qkv_a = _mm("proj_qkv_a", [u], [(0, w_in_nk, COL_Q_A, True)], [], 3 * A_COLS, F32, _plain_epilogue)
    gates, w_o = _mm("proj_gates", [u], [(0, w_in_nk, COL_GATE, True)],
                     [(b_gate[0].reshape(1, 2 * d), "row", 0)], 2 * d, BF16, _gate_epilogue,
                     side=[(w_out[0], 512, 512)])

    hm, c_p, n_p, m_p = _mlstm_prompt(qkv_m, sig_o, igfg, gate_bias)
    qkv_s = qkv_m[N_PROMPT:N_PROMPT + nb].astype(F32)
    if_s = igfg[N_PROMPT:N_PROMPT + nb] + gate_bias
    hm_s, c_s, n_s, m_s = _mlstm_sample(
        qkv_s[:, :QK_M_COLS].reshape(nb, H_M, DK_M),
        qkv_s[:, QK_M_COLS:2 * QK_M_COLS].reshape(nb, H_M, DK_M),
        qkv_s[:, 2 * QK_M_COLS:].reshape(nb, H_M, DV_M),
        sig_o[N_PROMPT:N_PROMPT + nb].astype(F32).reshape(nb, H_M, DV_M),
        if_s[:, :H_M], if_s[:, H_M:2 * H_M],
        state_mlstm_C[0], state_mlstm_n[0], state_mlstm_m[0])
    hm = _with_sample_rows(hm, hm_s.reshape(nb, V_M_COLS))

    ha = _attn_prompt(qkv_a)
    qkv_as = qkv_a[N_PROMPT:N_PROMPT + nb].reshape(nb, 3, N_GROUPS, H_A, DH_A)
    ha_s = _attn_sample(qkv_as[:, 0], qkv_as[:, 1], qkv_as[:, 2],
                        (cache_win1_kv[0], cache_win2_kv[0], cache_win3_kv[0]))
    ha = _with_sample_rows(ha, ha_s.reshape(nb, A_GROUP_COLS))

    z = _mm("mix", [hm, ha], [(0, w_mo, 0, False), (1, w_ao, 0, False)],
            [(gates, "tile", 0), (gates, "tile", d)], d, BF16, _mix_epilogue)
    x2 = _mm("proj_out", [z], [(0, w_o, 0, False)], [(x1, "tile", 0)], d, F32, _residual_epilogue)

    xn2 = _rmsnorm(x2, g_ffn2[0], BF16)
    y2 = _swiglu(xn2, w_ffn2_in[0], w_ffn2_out[0])
    out_p, out_s = _add_final_norm(x2, y2, g_final, 0.5)

    win_p, win_s = [], []
    for g, w in enumerate(WINDOWS):
        n_last = min(w, SEQ)
        k0, v0 = A_COLS + g * A_GROUP_COLS, 2 * A_COLS + g * A_GROUP_COLS
        per_seq = []
        for b in range(BATCH):
            last = qkv_a[(b + 1) * SEQ - n_last:(b + 1) * SEQ]
            per_seq.append(jnp.stack([last[:, k0:k0 + A_GROUP_COLS], last[:, v0:v0 + A_GROUP_COLS]], axis=1))
        win_p.append(jnp.stack(per_seq).reshape(1, BATCH, n_last, 2, H_A, DH_A))
        win_s.append(qkv_as[:, 1:3, g].reshape(1, nb, 1, 2, H_A, DH_A))
    return (out_p.reshape(BATCH, SEQ, d), out_s[:nb].reshape(nb, 1, d),
            c_p[None], n_p[None], m_p[None],
            c_s[None], n_s[None], m_s[:, :, 0][None],
            win_p[0], win_p[1], win_p[2], win_s[0], win_s[1], win_s[2])
```

```python
import functools

import numpy as np
import jax
import jax.numpy as jnp
from jax import lax
from jax.experimental import pallas as pl
from jax.experimental.pallas import tpu as pltpu

F32 = jnp.float32
BF16 = jnp.bfloat16

D_MODEL = 4096
BATCH = 2
SEQ = 4096
DEC_BATCH = 32
H_M = 8
DK_M = 256
DV_M = 512
MLSTM_CHUNK = 128
WINDOWS = (128, 512, 2048)
DILATIONS = (1, 4, 16)
N_GROUPS = 3
H_A = 8
DH_A = 128
Q_BLOCK = 128
ALIBI_MAX_EXP = 8.0
D_FF = 11008
EPS = 1e-6
NEG_INF = -1e30

QK_M_COLS = H_M * DK_M
V_M_COLS = H_M * DV_M
A_GROUP_COLS = H_A * DH_A
A_COLS = N_GROUPS * A_GROUP_COLS
COL_O_M = 2 * QK_M_COLS + V_M_COLS
COL_IF = COL_O_M + V_M_COLS
COL_Q_A = COL_IF + 2 * H_M
COL_GATE = COL_Q_A + 3 * A_COLS
LANES = 128
SUBLANES = 8

N_PROMPT = BATCH * SEQ
SAMPLE_PAD = 64
M_ROWS = N_PROMPT + SAMPLE_PAD

V7X_VMEM_BYTES = 64 * 1024 * 1024
VMEM_LIMIT_BYTES = V7X_VMEM_BYTES - 8 * 1024 * 1024
FFN_VMEM_LIMIT_BYTES = V7X_VMEM_BYTES - 2 * 1024 * 1024

ROW_TILE = 192
OUT_ROW_TILE = 256
MM_TM = 1376
MM_TN = 512
FFN_TM = 1376
FFN_TF = 256
FFN_TN = 512
ATT_SB = 2048


def _params(*semantics, vmem_limit_bytes=VMEM_LIMIT_BYTES):
    return pltpu.CompilerParams(dimension_semantics=semantics, vmem_limit_bytes=vmem_limit_bytes)


def _rms(x, g):
    ms = jnp.mean(x * x, axis=-1, keepdims=True)
    return x * lax.rsqrt(ms + EPS) * g


def _rmsnorm_body(x_ref, g_ref, o_ref):
    o_ref[...] = _rms(x_ref[...], g_ref[...]).astype(o_ref.dtype)


def _rmsnorm(x, g, out_dtype):
    m, d = x.shape
    return pl.pallas_call(
        _rmsnorm_body,
        grid=(m // ROW_TILE,),
        in_specs=[pl.BlockSpec((ROW_TILE, d), lambda i: (i, 0)),
                  pl.BlockSpec((1, d), lambda i: (0, 0))],
        out_specs=pl.BlockSpec((ROW_TILE, d), lambda i: (i, 0)),
        out_shape=jax.ShapeDtypeStruct((m, d), out_dtype),
        compiler_params=_params("parallel"),
        name="rmsnorm",
    )(x, g.reshape(1, d))


def _input_tile(xp_ref, xs_ref):
    is_sample_tile = pl.program_id(0) == N_PROMPT // OUT_ROW_TILE
    pad = jnp.zeros((OUT_ROW_TILE - xs_ref.shape[0], xs_ref.shape[1]), F32)
    return jnp.where(is_sample_tile, jnp.concatenate([xs_ref[...], pad], axis=0), xp_ref[...])


def _input_specs(d, nb):
    n_p = N_PROMPT // OUT_ROW_TILE
    return [pl.BlockSpec((OUT_ROW_TILE, d), lambda i: (jnp.minimum(i, n_p - 1), 0)),
            pl.BlockSpec((nb, d), lambda i: (0, 0))]


def _rmsnorm_inputs_body(xp_ref, xs_ref, g_ref, o_ref):
    o_ref[...] = _rms(_input_tile(xp_ref, xs_ref), g_ref[...]).astype(o_ref.dtype)


def _rmsnorm_inputs(xp, xs, g):
    d = xp.shape[1]
    row = pl.BlockSpec((OUT_ROW_TILE, d), lambda i: (i, 0))
    return pl.pallas_call(
        _rmsnorm_inputs_body,
        grid=(N_PROMPT // OUT_ROW_TILE + 1,),
        in_specs=_input_specs(d, xs.shape[0]) + [pl.BlockSpec((1, d), lambda i: (0, 0))],
        out_specs=row,
        out_shape=jax.ShapeDtypeStruct((M_ROWS, d), BF16),
        compiler_params=_params("parallel"),
        name="rmsnorm_inputs",
    )(xp, xs, g.reshape(1, d))


def _add_rmsnorm_inputs_body(xp_ref, xs_ref, y_ref, g_ref, xo_ref, no_ref, *, scale):
    x = _input_tile(xp_ref, xs_ref) + scale * y_ref[...]
    xo_ref[...] = x
    no_ref[...] = _rms(x, g_ref[...]).astype(no_ref.dtype)


def _add_rmsnorm_inputs(xp, xs, y, g, scale):
    d = xp.shape[1]
    row = pl.BlockSpec((OUT_ROW_TILE, d), lambda i: (i, 0))
    return pl.pallas_call(
        functools.partial(_add_rmsnorm_inputs_body, scale=scale),
        grid=(N_PROMPT // OUT_ROW_TILE + 1,),
        in_specs=_input_specs(d, xs.shape[0]) + [row, pl.BlockSpec((1, d), lambda i: (0, 0))],
        out_specs=[row, row],
        out_shape=[jax.ShapeDtypeStruct((M_ROWS, d), F32), jax.ShapeDtypeStruct((M_ROWS, d), BF16)],
        compiler_params=_params("parallel"),
        name="add_rmsnorm_inputs",
    )(xp, xs, y, g.reshape(1, d))


def _add_final_norm_body(x_ref, y_ref, g_ref, op_ref, os_ref, *, scale):
    out = _rms(x_ref[...] + scale * y_ref[...], g_ref[...])
    is_sample_tile = pl.program_id(0) == N_PROMPT // OUT_ROW_TILE

    @pl.when(jnp.logical_not(is_sample_tile))
    def _():
        op_ref[...] = out

    @pl.when(is_sample_tile)
    def _():
        os_ref[...] = out[:SAMPLE_PAD]


def _add_final_norm(x, y, g, scale):
    m, d = x.shape
    n_p = N_PROMPT // OUT_ROW_TILE
    row = pl.BlockSpec((OUT_ROW_TILE, d), lambda i: (i, 0))
    return pl.pallas_call(
        functools.partial(_add_final_norm_body, scale=scale),
        grid=(n_p + 1,),
        in_specs=[row, row, pl.BlockSpec((1, d), lambda i: (0, 0))],
        out_specs=[pl.BlockSpec((OUT_ROW_TILE, d), lambda i: (jnp.minimum(i, n_p - 1), 0)),
                   pl.BlockSpec((SAMPLE_PAD, d), lambda i: (0, 0))],
        out_shape=[jax.ShapeDtypeStruct((N_PROMPT, d), F32), jax.ShapeDtypeStruct((SAMPLE_PAD, d), F32)],
        compiler_params=_params("arbitrary"),
        name="add_final_norm",
    )(x, y, g.reshape(1, d))


def _side_cast_specs(side, n_inner):
    in_specs, out_specs, out_shapes = [], [], []
    for src, rb, cb in side:
        r, c = src.shape
        n_cb = c // cb
        last = (r // rb) * n_cb - 1

        def index(i, j, n_cb=n_cb, last=last):
            blk = jnp.minimum(i * n_inner + j, last)
            return lax.div(blk, n_cb), lax.rem(blk, n_cb)

        in_specs.append(pl.BlockSpec((rb, cb), index))
        out_specs.append(pl.BlockSpec((rb, cb), index))
        out_shapes.append(jax.ShapeDtypeStruct((r, c), BF16))
    return in_specs, out_specs, out_shapes


def _side_cast(src_refs, dst_refs):
    for src_ref, dst_ref in zip(src_refs, dst_refs):
        dst_ref[...] = src_ref[...].astype(dst_ref.dtype)


def _mm_body(*refs, dots, n_a, n_extra, n_side, epilogue):
    a_refs = refs[:n_a]
    pos = n_a
    weights = []
    for ai, w_is_nk, shift in dots:
        if shift:
            w = jnp.concatenate([refs[pos][shift:, :].astype(BF16), refs[pos + 1][...].astype(BF16)], axis=0)
            pos += 2
        else:
            w = refs[pos][...].astype(BF16)
            pos += 1
        weights.append(w)
    n_in = pos + n_extra
    e_refs = refs[pos:n_in]
    o_ref = refs[n_in + n_side]
    prods = []
    for (ai, w_is_nk, _), w in zip(dots, weights):
        contract = (((1,), (1,)), ((), ())) if w_is_nk else (((1,), (0,)), ((), ()))
        prods.append(lax.dot_general(a_refs[ai][...], w, contract, preferred_element_type=F32))
    o_ref[...] = epilogue(prods, [e[...] for e in e_refs]).astype(o_ref.dtype)
    _side_cast(refs[n_in:n_in + n_side], refs[n_in + n_side + 1:])


def _mm(name, a_list, w_list, extras, n, out_dtype, epilogue, tm=MM_TM, tn=MM_TN, side=()):
    m = a_list[0].shape[0]
    in_specs = [pl.BlockSpec((tm, a.shape[1]), lambda i, j: (i, 0)) for a in a_list]
    operands = list(a_list)
    dots = []
    for ai, w, off, w_is_nk in w_list:
        shift = off % tn if w_is_nk else 0
        if w_is_nk:
            in_specs.append(pl.BlockSpec((tn, w.shape[1]), lambda i, j, o=off // tn: (j + o, 0)))
        else:
            in_specs.append(pl.BlockSpec((w.shape[0], tn), lambda i, j, o=off // tn: (0, j + o)))
        operands.append(w)
        if shift:
            in_specs.append(pl.BlockSpec((shift, w.shape[1]),
                                         lambda i, j, o=off // tn, r=tn // shift: ((j + o + 1) * r, 0)))
            operands.append(w)
        dots.append((ai, w_is_nk, shift))
    for arr, kind, off in extras:
        if kind == "tile":
            in_specs.append(pl.BlockSpec((tm, tn), lambda i, j, o=off // tn: (i, j + o)))
        else:
            in_specs.append(pl.BlockSpec((1, tn), lambda i, j, o=off // tn: (0, j + o)))
        operands.append(arr)
    side_in, side_out, side_shapes = _side_cast_specs(side, n // tn)
    body = functools.partial(_mm_body, dots=tuple(dots),
                             n_a=len(a_list), n_extra=len(extras), n_side=len(side), epilogue=epilogue)
    outs = pl.pallas_call(
        body,
        grid=(m // tm, n // tn),
        in_specs=in_specs + side_in,
        out_specs=[pl.BlockSpec((tm, tn), lambda i, j: (i, j))] + side_out,
        out_shape=[jax.ShapeDtypeStruct((m, n), out_dtype)] + side_shapes,
        compiler_params=_params("arbitrary", "arbitrary"),
        name=name,
    )(*operands, *[src for src, _, _ in side])
    return outs if side else outs[0]


def _ffn_body(x_ref, wg_ref, wu_ref, wo_ref, o_ref):
    x = x_ref[...]
    g = jnp.dot(x, wg_ref[...].astype(BF16), preferred_element_type=F32)
    u = jnp.dot(x, wu_ref[...].astype(BF16), preferred_element_type=F32)
    h = (g * jax.nn.sigmoid(g) * u).astype(BF16)

    @pl.when(pl.program_id(1) == 0)
    def _():
        o_ref[...] = jnp.zeros_like(o_ref)

    for c in range(0, o_ref.shape[1], FFN_TN):
        o_ref[:, c:c + FFN_TN] += jnp.dot(h, wo_ref[:, c:c + FFN_TN].astype(BF16),
                                          preferred_element_type=F32)


def _swiglu(xn, w_in, w_out):
    m, d = xn.shape
    nf = D_FF // FFN_TF
    return pl.pallas_call(
        _ffn_body,
        grid=(m // FFN_TM, nf),
        in_specs=[pl.BlockSpec((FFN_TM, d), lambda i, f: (i, 0), pipeline_mode=pl.Buffered(1)),
                  pl.BlockSpec((d, FFN_TF), lambda i, f: (0, f)),
                  pl.BlockSpec((d, FFN_TF), lambda i, f: (0, f + nf)),
                  pl.BlockSpec((FFN_TF, d), lambda i, f: (f, 0))],
        out_specs=pl.BlockSpec((FFN_TM, d), lambda i, f: (i, 0), pipeline_mode=pl.Buffered(1)),
        out_shape=jax.ShapeDtypeStruct((m, d), F32),
        compiler_params=_params("parallel", "arbitrary", vmem_limit_bytes=FFN_VMEM_LIMIT_BYTES),
        name="swiglu",
    )(xn, w_in, w_in, w_out)


def _log_sigmoid(x):
    return jnp.minimum(x, 0.0) - jnp.log1p(jnp.exp(-jnp.abs(x)))


def _sample_rows_tile(o_ref, s_ref):
    o_ref[:SAMPLE_PAD, :] = s_ref[...]
    o_ref[SAMPLE_PAD:, :] = jnp.zeros((o_ref.shape[0] - SAMPLE_PAD, o_ref.shape[1]), o_ref.dtype)


def _mlstm_prompt_body(q_ref, k_ref, v_ref, so_ref, if_ref, bias_ref, hs_ref,
                       h_ref, c_ref, n_ref, m_ref):
    nc = SEQ // MLSTM_CHUNK
    step = pl.program_id(0)

    @pl.when(step < BATCH * nc)
    def _():
        _mlstm_chunk(lax.rem(step, nc) == 0, q_ref, k_ref, v_ref, so_ref, if_ref, bias_ref,
                     h_ref, c_ref, n_ref, m_ref)

    @pl.when(step == BATCH * nc)
    def _():
        _sample_rows_tile(h_ref, hs_ref)


def _mlstm_chunk(is_first_chunk, q_ref, k_ref, v_ref, so_ref, if_ref, bias_ref, h_ref, c_ref, n_ref, m_ref):
    L = MLSTM_CHUNK

    @pl.when(is_first_chunk)
    def _():
        c_ref[...] = jnp.zeros_like(c_ref)
        n_ref[...] = jnp.zeros_like(n_ref)
        m_ref[...] = jnp.zeros_like(m_ref)

    x = if_ref[...] + bias_ref[...]
    xt = x.T
    t_idx = lax.broadcasted_iota(jnp.int32, (L, L), 0)
    s_idx = lax.broadcasted_iota(jnp.int32, (L, L), 1)
    causal = s_idx <= t_idx
    b_cols = jnp.dot(causal.astype(F32), _log_sigmoid(x), precision=lax.Precision.HIGHEST,
                     preferred_element_type=F32)
    b_rows = b_cols.T
    for h in range(H_M):
        ig_col = x[:, h:h + 1]
        ig_row = xt[h:h + 1, :]
        b_col = b_cols[:, H_M + h:H_M + h + 1]
        b_row = b_rows[H_M + h:H_M + h + 1, :]
        m_prev = m_ref[h, 0:1, 0:1]
        d = jnp.where(causal, b_col - b_row + ig_row, -jnp.inf)
        m_inter = b_col + m_prev
        m_t = jnp.maximum(m_inter, jnp.max(d, axis=1, keepdims=True))
        w_inter = jnp.exp(m_inter - m_t)
        m_new = m_t[L - 1:L, :]
        b_last = b_col[L - 1:L, :]
        w_end = jnp.exp(b_last - b_col + ig_col - m_new)
        decay = jnp.exp(b_last + m_prev - m_new)

        qh = q_ref[:, h * DK_M:(h + 1) * DK_M]
        kf = k_ref[:, h * DK_M:(h + 1) * DK_M].astype(F32) * (DK_M ** -0.5)
        s = lax.dot_general(qh, kf.astype(BF16), (((1,), (1,)), ((), ())),
                            preferred_element_type=F32)
        c_old = c_ref[h]
        qc = jnp.dot(qh, c_old.astype(BF16), preferred_element_type=F32)
        n_tile = n_ref[h]
        n_old = n_tile[0:1, :]
        qn = lax.dot_general(qh, n_tile.astype(BF16), (((1,), (1,)), ((), ())),
                             preferred_element_type=F32)[:, 0:1]
        kw = kf * w_end
        a = s * jnp.exp(d - m_t)
        vh = v_ref[:, h * DV_M:(h + 1) * DV_M]
        num = jnp.dot(a.astype(BF16), vh, preferred_element_type=F32) + w_inter * qc
        den = jnp.sum(a, axis=1, keepdims=True) + w_inter * qn
        hh = num / jnp.maximum(jnp.abs(den), jnp.exp(-m_t))
        gate = so_ref[:, h * DV_M:(h + 1) * DV_M].astype(F32)
        h_ref[:, h * DV_M:(h + 1) * DV_M] = (gate * hh).astype(h_ref.dtype)

        dc = jnp.dot(kw.T.astype(BF16), vh, preferred_element_type=F32)
        c_ref[h] = decay * c_old + dc
        n_new = decay * n_old + jnp.sum(kw, axis=0, keepdims=True)
        n_ref[h] = jnp.broadcast_to(n_new, n_ref.shape[1:])
        m_ref[h] = jnp.broadcast_to(m_new, m_ref.shape[1:])


def _mlstm_prompt(qkv_m, sig_o, igfg, gate_bias, hm_sample):
    nc = SEQ // MLSTM_CHUNK
    L = MLSTM_CHUNK
    last = BATCH * nc - 1

    def rows(width, col):
        return pl.BlockSpec((L, width), lambda s: (jnp.minimum(s, last), col))

    def state(*shape):
        return pl.BlockSpec((None,) + shape, lambda s: (jnp.minimum(lax.div(s, nc), BATCH - 1),) + (0,) * len(shape))

    hm, c, n, m = pl.pallas_call(
        _mlstm_prompt_body,
        grid=(BATCH * nc + 1,),
        in_specs=[rows(QK_M_COLS, 0), rows(QK_M_COLS, 1), rows(V_M_COLS, 1), rows(V_M_COLS, 0),
                  rows(LANES, 0), pl.BlockSpec((1, LANES), lambda s: (0, 0)),
                  pl.BlockSpec((SAMPLE_PAD, V_M_COLS), lambda s: (0, 0))],
        out_specs=[pl.BlockSpec((L, V_M_COLS), lambda s: (s, 0)),
                   state(H_M, DK_M, DV_M), state(H_M, SUBLANES, DK_M), state(H_M, SUBLANES, LANES)],
        out_shape=[jax.ShapeDtypeStruct((M_ROWS, V_M_COLS), BF16),
                   jax.ShapeDtypeStruct((BATCH, H_M, DK_M, DV_M), F32),
                   jax.ShapeDtypeStruct((BATCH, H_M, SUBLANES, DK_M), F32),
                   jax.ShapeDtypeStruct((BATCH, H_M, SUBLANES, LANES), F32)],
        compiler_params=_params("arbitrary"),
        name="mlstm_prompt",
    )(qkv_m, qkv_m, qkv_m, sig_o, igfg, gate_bias, hm_sample)
    return hm, c, n[:, :, 0, :], m[:, :, 0, 0]


def _mlstm_sample_body(q_ref, qt_ref, k_ref, kt_ref, v_ref, so_ref, ig_ref, fg_ref, m0_ref,
                       c0_ref, n0_ref, h_ref, c_ref, n_ref, m_ref):
    for h in range(H_M):
        ig = ig_ref[h:h + 1, 0:1]
        lf = _log_sigmoid(fg_ref[h:h + 1, 0:1])
        m_prev = m0_ref[h:h + 1, 0:1]
        m_inter = lf + m_prev
        m_t = jnp.maximum(m_inter, ig)
        w_inter = jnp.exp(m_inter - m_t)
        q_row = q_ref[h:h + 1, :]
        k_row = k_ref[h:h + 1, :] * (DK_M ** -0.5)
        q_col = qt_ref[:, h:h + 1]
        k_col = kt_ref[:, h:h + 1] * (DK_M ** -0.5)
        v_row = v_ref[h:h + 1, :]
        c_old = c0_ref[h]
        n_old = n0_ref[h:h + 1, :]
        a = jnp.sum(q_row * k_row, axis=1, keepdims=True) * jnp.exp(ig - m_t)
        num = a * v_row + w_inter * jnp.sum(q_col * c_old, axis=0, keepdims=True)
        den = a + w_inter * jnp.sum(q_row * n_old, axis=1, keepdims=True)
        hh = num / jnp.maximum(jnp.abs(den), jnp.exp(-m_t))
        h_ref[h:h + 1, :] = so_ref[h:h + 1, :] * hh
        w_end = jnp.exp(ig - m_t)
        decay = jnp.exp(m_inter - m_t)
        c_ref[h] = decay * c_old + (k_col * w_end) * v_row
        n_ref[h:h + 1, :] = decay * n_old + k_row * w_end
        m_ref[h:h + 1, :] = jnp.broadcast_to(m_t, (1, LANES))


def _mlstm_sample(q, k, v, so, ig, fg, c0, n0, m0):
    nb = q.shape[0]

    def lanes(a):
        return jnp.broadcast_to(a[:, :, None], (nb, H_M, LANES))

    def per_b(*shape):
        return pl.BlockSpec((None,) + shape, lambda b: (b,) + (0,) * len(shape))

    return pl.pallas_call(
        _mlstm_sample_body,
        grid=(nb,),
        in_specs=[per_b(H_M, DK_M), per_b(DK_M, H_M), per_b(H_M, DK_M), per_b(DK_M, H_M),
                  per_b(H_M, DV_M), per_b(H_M, DV_M), per_b(H_M, LANES), per_b(H_M, LANES),
                  per_b(H_M, LANES), per_b(H_M, DK_M, DV_M), per_b(H_M, DK_M)],
        out_specs=[per_b(H_M, DV_M), per_b(H_M, DK_M, DV_M), per_b(H_M, DK_M), per_b(H_M, LANES)],
        out_shape=[jax.ShapeDtypeStruct((nb, H_M, DV_M), F32),
                   jax.ShapeDtypeStruct((nb, H_M, DK_M, DV_M), F32),
                   jax.ShapeDtypeStruct((nb, H_M, DK_M), F32),
                   jax.ShapeDtypeStruct((nb, H_M, LANES), F32)],
        compiler_params=_params("parallel"),
        name="mlstm_sample",
    )(q, jnp.swapaxes(q, 1, 2), k, jnp.swapaxes(k, 1, 2), v, so, lanes(ig), lanes(fg), lanes(m0),
      c0, n0)


def _alibi_slopes():
    n = N_GROUPS * H_A
    e = np.arange(1, n + 1, dtype=np.float64) * (-ALIBI_MAX_EXP / n)
    return np.exp2(e).reshape(N_GROUPS, H_A)


def _softmax_mix(lses, outs):
    m = jnp.maximum(jnp.maximum(lses[0], lses[1]), lses[2])
    e = [jnp.exp(x - m) for x in lses]
    z = e[0] + e[1] + e[2]
    acc = (e[0] / z) * outs[0]
    for g in (1, 2):
        acc = acc + (e[g] / z) * outs[g]
    return acc


def _attn_prompt_body(slopes_ref, *refs):
    n_tiles = BATCH * (SEQ // ATT_SB)
    tile = pl.program_id(1)

    @pl.when(tile < n_tiles)
    def _():
        _attn_prompt_tile(slopes_ref, *refs)

    @pl.when(tile == n_tiles)
    def _():
        _sample_rows_tile(refs[5 * N_GROUPS + 1], refs[5 * N_GROUPS])


def _attn_prompt_tile(slopes_ref, *refs):
    G = N_GROUPS
    q_refs, kc_refs, kp_refs = refs[0:G], refs[G:2 * G], refs[2 * G:3 * G]
    vc_refs, vp_refs = refs[3 * G:4 * G], refs[4 * G:5 * G]
    o_ref = refs[5 * G + 1]
    o_scr, l_scr = refs[5 * G + 2:6 * G + 2], refs[6 * G + 2:7 * G + 2]
    L = Q_BLOCK
    head = pl.program_id(0)
    first_key = jnp.where(lax.rem(pl.program_id(1), SEQ // ATT_SB) == 0, L, 0)
    t_idx = lax.broadcasted_iota(jnp.int32, (L, 2 * L), 0)
    s_idx = lax.broadcasted_iota(jnp.int32, (L, 2 * L), 1)
    step = t_idx + L - s_idx
    in_window = jnp.logical_and(step >= 0, step <= L)
    in_window_first = jnp.logical_and(in_window, s_idx >= first_key)
    nt = (((1,), (1,)), ((), ()))

    def rows(start, size, dil):
        return pl.ds(start, size) if dil == 1 else pl.ds(start, size, stride=dil)

    for g in range(G):
        dil = DILATIONS[g]
        alibi = step.astype(F32) * (slopes_ref[g, head] * dil)
        bias_in = jnp.where(in_window, alibi, -NEG_INF)
        bias_first = jnp.where(in_window_first, alibi, -NEG_INF)

        def sub_block(j, r, g=g, dil=dil, bias_in=bias_in, bias_first=bias_first):
            start = j * L * dil + r
            q = q_refs[g][rows(start, L, dil), :].astype(BF16)
            if j == 0:
                k2 = jnp.concatenate([kp_refs[g][rows(r, L, dil), :], kc_refs[g][rows(start, L, dil), :]], axis=0)
                v2 = jnp.concatenate([vp_refs[g][rows(r, L, dil), :], vc_refs[g][rows(start, L, dil), :]], axis=0)
                bias = bias_first
            else:
                k2 = kc_refs[g][rows(start - L * dil, 2 * L, dil), :]
                v2 = vc_refs[g][rows(start - L * dil, 2 * L, dil), :]
                bias = bias_in
            s = lax.dot_general(q, k2.astype(BF16), nt, preferred_element_type=F32)
            s = s * (DH_A ** -0.5) - bias
            m = jnp.max(s, axis=1, keepdims=True)
            p = jnp.exp(s - m)
            l = jnp.sum(p, axis=1, keepdims=True)
            o = jnp.dot(p.astype(BF16), v2.astype(BF16), preferred_element_type=F32) / l
            o_scr[g][rows(start, L, dil), :] = o
            l_scr[g][rows(start, L, dil), :] = jnp.broadcast_to(m + jnp.log(l), (L, LANES))

        for j in range(ATT_SB // (L * dil)):
            if dil == 1:
                sub_block(j, 0)
            else:
                def residue(r, carry, j=j):
                    sub_block(j, r)
                    return carry
                lax.fori_loop(0, dil, residue, 0, unroll=16)

    chunk = 256
    for c in range(0, ATT_SB, chunk):
        lses = [l_scr[g][c:c + chunk, :] for g in range(G)]
        outs = [o_scr[g][c:c + chunk, :] for g in range(G)]
        o_ref[c:c + chunk, :] = _softmax_mix(lses, outs).astype(o_ref.dtype)


def _attn_prompt(qkv_a, ha_sample):
    ns = SEQ // ATT_SB
    L = Q_BLOCK
    heads_per_part = N_GROUPS * H_A
    last = BATCH * ns - 1

    def cur(part, g):
        return pl.BlockSpec((ATT_SB, DH_A),
                            lambda h, t: (jnp.minimum(t, last), part * heads_per_part + g * H_A + h))

    def prev(part, g):
        span = L * DILATIONS[g]
        f = ATT_SB // span

        def index(h, t):
            t = jnp.minimum(t, last)
            b, i = lax.div(t, ns), lax.rem(t, ns)
            return b * ns * f + jnp.maximum(i * f - 1, 0), part * heads_per_part + g * H_A + h

        return pl.BlockSpec((span, DH_A), index)

    groups = range(N_GROUPS)
    in_specs = ([pl.BlockSpec(memory_space=pltpu.SMEM)]
                + [cur(0, g) for g in groups] + [cur(1, g) for g in groups] + [prev(1, g) for g in groups]
                + [cur(2, g) for g in groups] + [prev(2, g) for g in groups]
                + [pl.BlockSpec((SAMPLE_PAD, DH_A), lambda h, t: (0, h))])
    slopes = jnp.asarray(_alibi_slopes(), F32)
    return pl.pallas_call(
        _attn_prompt_body,
        grid=(H_A, BATCH * ns + 1),
        in_specs=in_specs,
        out_specs=pl.BlockSpec((ATT_SB, DH_A), lambda h, t: (t, h)),
        out_shape=jax.ShapeDtypeStruct((M_ROWS, A_GROUP_COLS), BF16),
        scratch_shapes=[pltpu.VMEM((ATT_SB, DH_A), F32)] * (2 * N_GROUPS),
        compiler_params=_params("parallel", "arbitrary"),
        name="attn_prompt",
    )(slopes, *([qkv_a] * (5 * N_GROUPS)), ha_sample)


def _attn_sample_body(slopes_ref, q_ref, kn_ref, vn_ref, c0_ref, c1_ref, c2_ref, o_ref):
    L = Q_BLOCK
    c_refs = (c0_ref, c1_ref, c2_ref)
    steps = (L - lax.broadcasted_iota(jnp.int32, (L, 1, 1), 0)).astype(F32)
    outs, lses = [], []
    for g in range(N_GROUPS):
        q = q_ref[g]
        slope = slopes_ref[g][:, 0:1]
        s = (jnp.sum(c_refs[g][:, 0] * q[None], axis=-1, keepdims=True) * (DH_A ** -0.5)
             - (slope * DILATIONS[g])[None] * steps)
        s_new = jnp.sum(q * kn_ref[g], axis=-1, keepdims=True) * (DH_A ** -0.5)
        m = jnp.maximum(jnp.max(s, axis=0), s_new)
        p = jnp.exp(s - m[None])
        p_new = jnp.exp(s_new - m)
        l = jnp.sum(p, axis=0) + p_new
        outs.append((jnp.sum(p * c_refs[g][:, 1], axis=0) + p_new * vn_ref[g]) / l)
        lses.append(m + jnp.log(l))
    o_ref[...] = _softmax_mix(lses, outs)


def _attn_sample(q, kn, vn, caches):
    nb = q.shape[0]
    small = pl.BlockSpec((None, N_GROUPS, H_A, DH_A), lambda b: (b, 0, 0, 0))
    cache_specs, cache_views = [], []
    for g, c in enumerate(caches):
        dil = DILATIONS[g]
        assert c.shape[1] == Q_BLOCK * dil
        cache_views.append(c.reshape(nb, Q_BLOCK, dil, 2, H_A, DH_A))
        cache_specs.append(pl.BlockSpec((None, Q_BLOCK, None, 2, H_A, DH_A), lambda b: (b, 0, 0, 0, 0, 0)))
    slopes = jnp.broadcast_to(jnp.asarray(_alibi_slopes(), F32)[:, :, None], (N_GROUPS, H_A, LANES))
    return pl.pallas_call(
        _attn_sample_body,
        grid=(nb,),
        in_specs=[pl.BlockSpec((N_GROUPS, H_A, LANES), lambda b: (0, 0, 0)), small, small, small] + cache_specs,
        out_specs=pl.BlockSpec((None, H_A, DH_A), lambda b: (b, 0, 0)),
        out_shape=jax.ShapeDtypeStruct((nb, H_A, DH_A), F32),
        compiler_params=_params("parallel"),
        name="attn_sample",
    )(slopes, q, kn, vn, *cache_views)


def _logistic(x):
    return 0.5 * jnp.tanh(0.5 * x) + 0.5


def _sigmoid_epilogue(prods, extras):
    return _logistic(prods[0])


def _gate_epilogue(prods, extras):
    return _logistic(prods[0] + extras[0])


def _plain_epilogue(prods, extras):
    return prods[0]


def _mix_epilogue(prods, extras):
    return extras[0].astype(F32) * prods[0] + extras[1].astype(F32) * prods[1]


def _residual_epilogue(prods, extras):
    return extras[0] + prods[0]


def _padded_sample_rows(sample_rows):
    nb, c = sample_rows.shape
    return jnp.concatenate([sample_rows.astype(BF16), jnp.zeros((SAMPLE_PAD - nb, c), BF16)], axis=0)


def kernel(x_prompt, x_sample, state_mlstm_C, state_mlstm_n, state_mlstm_m, cache_win1_kv, cache_win2_kv, cache_win3_kv, g_ffn1, w_ffn1_in, w_ffn1_out, g_mix, w_in, b_gate, b_igate, b_fgate, w_mlstm_out, w_attn_out, w_out, g_ffn2, w_ffn2_in, w_ffn2_out, g_final):
    d = D_MODEL
    nb = DEC_BATCH
    xp, xs = x_prompt.reshape(N_PROMPT, d), x_sample.reshape(nb, d)

    w_in_nk = w_in[0].T
    gate_bias = jnp.pad(jnp.concatenate([b_igate[0], b_fgate[0]]), (0, LANES - 2 * H_M)).reshape(1, LANES)

    xn = _rmsnorm_inputs(xp, xs, g_ffn1[0])
    y = _swiglu(xn, w_ffn1_in[0], w_ffn1_out[0])
    x1, u = _add_rmsnorm_inputs(xp, xs, y, g_mix[0], 0.5)

    qkv_m, w_mo = _mm("proj_qkv_m", [u], [(0, w_in_nk, 0, True)], [], COL_O_M, BF16, _plain_epilogue,
                      side=[(w_mlstm_out[0], 512, 512)])
    sig_o, w_ao = _mm("proj_o_m", [u], [(0, w_in_nk, COL_O_M, True)], [], V_M_COLS, BF16, _sigmoid_epilogue,
                      side=[(w_attn_out[0], 256, 512)])
    igfg = _mm("proj_if", [u], [(0, w_in_nk, COL_IF, True)], [], LANES, F32, _plain_epilogue, tn=LANES)
    qkv_a = _mm("proj_qkv_a", [u], [(0, w_in_nk, COL_Q_A, True)], [], 3 * A_COLS, F32, _plain_epilogue)
    gates, w_o = _mm("proj_gates", [u], [(0, w_in_nk, COL_GATE, True)],
                     [(b_gate[0].reshape(1, 2 * d), "row", 0)], 2 * d, BF16, _gate_epilogue,
                     side=[(w_out[0], 512, 512)])

    qkv_s = qkv_m[N_PROMPT:N_PROMPT + nb].astype(F32)
    if_s = igfg[N_PROMPT:N_PROMPT + nb] + gate_bias
    hm_s, c_s, n_s, m_s = _mlstm_sample(
        qkv_s[:, :QK_M_COLS].reshape(nb, H_M, DK_M),
        qkv_s[:, QK_M_COLS:2 * QK_M_COLS].reshape(nb, H_M, DK_M),
        qkv_s[:, 2 * QK_M_COLS:].reshape(nb, H_M, DV_M),
        sig_o[N_PROMPT:N_PROMPT + nb].astype(F32).reshape(nb, H_M, DV_M),
        if_s[:, :H_M], if_s[:, H_M:2 * H_M],
        state_mlstm_C[0], state_mlstm_n[0], state_mlstm_m[0])
    hm, c_p, n_p, m_p = _mlstm_prompt(qkv_m, sig_o, igfg, gate_bias,
                                      _padded_sample_rows(hm_s.reshape(nb, V_M_COLS)))

    qkv_as = qkv_a[N_PROMPT:N_PROMPT + nb].reshape(nb, 3, N_GROUPS, H_A, DH_A)
    ha_s = _attn_sample(qkv_as[:, 0], qkv_as[:, 1], qkv_as[:, 2],
                        (cache_win1_kv[0], cache_win2_kv[0], cache_win3_kv[0]))
    ha = _attn_prompt(qkv_a, _padded_sample_rows(ha_s.reshape(nb, A_GROUP_COLS)))

    z = _mm("mix", [hm, ha], [(0, w_mo, 0, False), (1, w_ao, 0, False)],
            [(gates, "tile", 0), (gates, "tile", d)], d, BF16, _mix_epilogue)
    x2 = _mm("proj_out", [z], [(0, w_o, 0, False)], [(x1, "tile", 0)], d, F32, _residual_epilogue)

    xn2 = _rmsnorm(x2, g_ffn2[0], BF16)
    y2 = _swiglu(xn2, w_ffn2_in[0], w_ffn2_out[0])
    out_p, out_s = _add_final_norm(x2, y2, g_final, 0.5)

    win_p, win_s = [], []
    for g, w in enumerate(WINDOWS):
        n_last = min(w, SEQ)
        k0, v0 = A_COLS + g * A_GROUP_COLS, 2 * A_COLS + g * A_GROUP_COLS
        per_seq = []
        for b in range(BATCH):
            last = qkv_a[(b + 1) * SEQ - n_last:(b + 1) * SEQ]
            per_seq.append(jnp.stack([last[:, k0:k0 + A_GROUP_COLS], last[:, v0:v0 + A_GROUP_COLS]], axis=1))
        win_p.append(jnp.stack(per_seq).reshape(1, BATCH, n_last, 2, H_A, DH_A))
        win_s.append(qkv_as[:, 1:3, g].reshape(1, nb, 1, 2, H_A, DH_A))
    return (out_p.reshape(BATCH, SEQ, d), out_s[:nb].reshape(nb, 1, d),
            c_p[None], n_p[None], m_p[None],
            c_s[None], n_s[None], m_s[:, :, 0][None],
            win_p[0], win_p[1], win_p[2], win_s[0], win_s[1], win_s[2])
```

```python
import functools

import numpy as np
import jax
import jax.numpy as jnp
from jax import lax
from jax.experimental import pallas as pl
from jax.experimental.pallas import tpu as pltpu

F32 = jnp.float32
BF16 = jnp.bfloat16

D_MODEL = 4096
BATCH = 2
SEQ = 4096
DEC_BATCH = 32
H_M = 8
DK_M = 256
DV_M = 512
MLSTM_CHUNK = 128
WINDOWS = (128, 512, 2048)
DILATIONS = (1, 4, 16)
N_GROUPS = 3
H_A = 8
DH_A = 128
Q_BLOCK = 128
ALIBI_MAX_EXP = 8.0
D_FF = 11008
EPS = 1e-6
NEG_INF = -1e30

QK_M_COLS = H_M * DK_M
V_M_COLS = H_M * DV_M
A_GROUP_COLS = H_A * DH_A
A_COLS = N_GROUPS * A_GROUP_COLS
COL_O_M = 2 * QK_M_COLS + V_M_COLS
COL_IF = COL_O_M + V_M_COLS
COL_Q_A = COL_IF + 2 * H_M
COL_GATE = COL_Q_A + 3 * A_COLS
LANES = 128
SUBLANES = 8

N_PROMPT = BATCH * SEQ
SAMPLE_PAD = 64
M_ROWS = N_PROMPT + SAMPLE_PAD

V7X_VMEM_BYTES = 64 * 1024 * 1024
VMEM_LIMIT_BYTES = V7X_VMEM_BYTES - 8 * 1024 * 1024
FFN_VMEM_LIMIT_BYTES = V7X_VMEM_BYTES - 2 * 1024 * 1024

ROW_TILE = 192
OUT_ROW_TILE = 256
MM_TM = 1376
MM_TN = 512
FFN_TM = 1376
FFN_TF = 256
FFN_TN = 512
ATT_SB = 2048


def _params(*semantics, vmem_limit_bytes=VMEM_LIMIT_BYTES):
    return pltpu.CompilerParams(dimension_semantics=semantics, vmem_limit_bytes=vmem_limit_bytes)


def _rms(x, g):
    ms = jnp.mean(x * x, axis=-1, keepdims=True)
    return x * lax.rsqrt(ms + EPS) * g


def _rmsnorm_body(x_ref, g_ref, o_ref):
    o_ref[...] = _rms(x_ref[...], g_ref[...]).astype(o_ref.dtype)


def _rmsnorm(x, g, out_dtype):
    m, d = x.shape
    return pl.pallas_call(
        _rmsnorm_body,
        grid=(m // ROW_TILE,),
        in_specs=[pl.BlockSpec((ROW_TILE, d), lambda i: (i, 0)),
                  pl.BlockSpec((1, d), lambda i: (0, 0))],
        out_specs=pl.BlockSpec((ROW_TILE, d), lambda i: (i, 0)),
        out_shape=jax.ShapeDtypeStruct((m, d), out_dtype),
        compiler_params=_params("parallel"),
        name="rmsnorm",
    )(x, g.reshape(1, d))


def _input_tile(xp_ref, xs_ref):
    is_sample_tile = pl.program_id(0) == N_PROMPT // OUT_ROW_TILE
    pad = jnp.zeros((OUT_ROW_TILE - xs_ref.shape[0], xs_ref.shape[1]), F32)
    return jnp.where(is_sample_tile, jnp.concatenate([xs_ref[...], pad], axis=0), xp_ref[...])


def _input_specs(d, nb):
    n_p = N_PROMPT // OUT_ROW_TILE
    return [pl.BlockSpec((OUT_ROW_TILE, d), lambda i: (jnp.minimum(i, n_p - 1), 0)),
            pl.BlockSpec((nb, d), lambda i: (0, 0))]


def _rmsnorm_inputs_body(xp_ref, xs_ref, g_ref, o_ref):
    o_ref[...] = _rms(_input_tile(xp_ref, xs_ref), g_ref[...]).astype(o_ref.dtype)


def _rmsnorm_inputs(xp, xs, g):
    d = xp.shape[1]
    row = pl.BlockSpec((OUT_ROW_TILE, d), lambda i: (i, 0))
    return pl.pallas_call(
        _rmsnorm_inputs_body,
        grid=(N_PROMPT // OUT_ROW_TILE + 1,),
        in_specs=_input_specs(d, xs.shape[0]) + [pl.BlockSpec((1, d), lambda i: (0, 0))],
        out_specs=row,
        out_shape=jax.ShapeDtypeStruct((M_ROWS, d), BF16),
        compiler_params=_params("parallel"),
        name="rmsnorm_inputs",
    )(xp, xs, g.reshape(1, d))


def _add_rmsnorm_inputs_body(xp_ref, xs_ref, y_ref, g_ref, xo_ref, no_ref, *, scale):
    x = _input_tile(xp_ref, xs_ref) + scale * y_ref[...]
    xo_ref[...] = x
    no_ref[...] = _rms(x, g_ref[...]).astype(no_ref.dtype)


def _add_rmsnorm_inputs(xp, xs, y, g, scale):
    d = xp.shape[1]
    row = pl.BlockSpec((OUT_ROW_TILE, d), lambda i: (i, 0))
    return pl.pallas_call(
        functools.partial(_add_rmsnorm_inputs_body, scale=scale),
        grid=(N_PROMPT // OUT_ROW_TILE + 1,),
        in_specs=_input_specs(d, xs.shape[0]) + [row, pl.BlockSpec((1, d), lambda i: (0, 0))],
        out_specs=[row, row],
        out_shape=[jax.ShapeDtypeStruct((M_ROWS, d), F32), jax.ShapeDtypeStruct((M_ROWS, d), BF16)],
        compiler_params=_params("parallel"),
        name="add_rmsnorm_inputs",
    )(xp, xs, y, g.reshape(1, d))


def _add_final_norm_body(x_ref, y_ref, g_ref, op_ref, os_ref, *, scale):
    out = _rms(x_ref[...] + scale * y_ref[...], g_ref[...])
    is_sample_tile = pl.program_id(0) == N_PROMPT // OUT_ROW_TILE

    @pl.when(jnp.logical_not(is_sample_tile))
    def _():
        op_ref[...] = out

    @pl.when(is_sample_tile)
    def _():
        os_ref[...] = out[:SAMPLE_PAD]


def _add_final_norm(x, y, g, scale):
    m, d = x.shape
    n_p = N_PROMPT // OUT_ROW_TILE
    row = pl.BlockSpec((OUT_ROW_TILE, d), lambda i: (i, 0))
    return pl.pallas_call(
        functools.partial(_add_final_norm_body, scale=scale),
        grid=(n_p + 1,),
        in_specs=[row, row, pl.BlockSpec((1, d), lambda i: (0, 0))],
        out_specs=[pl.BlockSpec((OUT_ROW_TILE, d), lambda i: (jnp.minimum(i, n_p - 1), 0)),
                   pl.BlockSpec((SAMPLE_PAD, d), lambda i: (0, 0))],
        out_shape=[jax.ShapeDtypeStruct((N_PROMPT, d), F32), jax.ShapeDtypeStruct((SAMPLE_PAD, d), F32)],
        compiler_params=_params("arbitrary"),
        name="add_final_norm",
    )(x, y, g.reshape(1, d))


def _side_cast_specs(side, n_inner):
    in_specs, out_specs, out_shapes = [], [], []
    for src, rb, cb in side:
        r, c = src.shape
        n_cb = c // cb
        last = (r // rb) * n_cb - 1

        def index(i, j, n_cb=n_cb, last=last):
            blk = jnp.minimum(i * n_inner + j, last)
            return lax.div(blk, n_cb), lax.rem(blk, n_cb)

        in_specs.append(pl.BlockSpec((rb, cb), index))
        out_specs.append(pl.BlockSpec((rb, cb), index))
        out_shapes.append(jax.ShapeDtypeStruct((r, c), BF16))
    return in_specs, out_specs, out_shapes


def _side_cast(src_refs, dst_refs):
    for src_ref, dst_ref in zip(src_refs, dst_refs):
        dst_ref[...] = src_ref[...].astype(dst_ref.dtype)


def _mm_body(*refs, dots, n_a, n_extra, n_side, epilogue):
    a_refs = refs[:n_a]
    pos = n_a
    weights = []
    for ai, w_is_nk, shift in dots:
        if shift:
            w = jnp.concatenate([refs[pos][shift:, :].astype(BF16), refs[pos + 1][...].astype(BF16)], axis=0)
            pos += 2
        else:
            w = refs[pos][...].astype(BF16)
            pos += 1
        weights.append(w)
    n_in = pos + n_extra
    e_refs = refs[pos:n_in]
    o_ref = refs[n_in + n_side]
    prods = []
    for (ai, w_is_nk, _), w in zip(dots, weights):
        contract = (((1,), (1,)), ((), ())) if w_is_nk else (((1,), (0,)), ((), ()))
        prods.append(lax.dot_general(a_refs[ai][...], w, contract, preferred_element_type=F32))
    o_ref[...] = epilogue(prods, [e[...] for e in e_refs]).astype(o_ref.dtype)
    _side_cast(refs[n_in:n_in + n_side], refs[n_in + n_side + 1:])


def _mm(name, a_list, w_list, extras, n, out_dtype, epilogue, tm=MM_TM, tn=MM_TN, side=()):
    m = a_list[0].shape[0]
    in_specs = [pl.BlockSpec((tm, a.shape[1]), lambda i, j: (i, 0)) for a in a_list]
    operands = list(a_list)
    dots = []
    for ai, w, off, w_is_nk in w_list:
        shift = off % tn if w_is_nk else 0
        if w_is_nk:
            in_specs.append(pl.BlockSpec((tn, w.shape[1]), lambda i, j, o=off // tn: (j + o, 0)))
        else:
            in_specs.append(pl.BlockSpec((w.shape[0], tn), lambda i, j, o=off // tn: (0, j + o)))
        operands.append(w)
        if shift:
            in_specs.append(pl.BlockSpec((shift, w.shape[1]),
                                         lambda i, j, o=off // tn, r=tn // shift: ((j + o + 1) * r, 0)))
            operands.append(w)
        dots.append((ai, w_is_nk, shift))
    for arr, kind, off in extras:
        if kind == "tile":
            in_specs.append(pl.BlockSpec((tm, tn), lambda i, j, o=off // tn: (i, j + o)))
        else:
            in_specs.append(pl.BlockSpec((1, tn), lambda i, j, o=off // tn: (0, j + o)))
        operands.append(arr)
    side_in, side_out, side_shapes = _side_cast_specs(side, n // tn)
    body = functools.partial(_mm_body, dots=tuple(dots),
                             n_a=len(a_list), n_extra=len(extras), n_side=len(side), epilogue=epilogue)
    outs = pl.pallas_call(
        body,
        grid=(m // tm, n // tn),
        in_specs=in_specs + side_in,
        out_specs=[pl.BlockSpec((tm, tn), lambda i, j: (i, j))] + side_out,
        out_shape=[jax.ShapeDtypeStruct((m, n), out_dtype)] + side_shapes,
        compiler_params=_params("arbitrary", "arbitrary"),
        name=name,
    )(*operands, *[src for src, _, _ in side])
    return outs if side else outs[0]


def _ffn_body(x_ref, wg_ref, wu_ref, wo_ref, o_ref):
    x = x_ref[...]
    g = jnp.dot(x, wg_ref[...].astype(BF16), preferred_element_type=F32)
    u = jnp.dot(x, wu_ref[...].astype(BF16), preferred_element_type=F32)
    h = (g * jax.nn.sigmoid(g) * u).astype(BF16)

    @pl.when(pl.program_id(1) == 0)
    def _():
        o_ref[...] = jnp.zeros_like(o_ref)

    for c in range(0, o_ref.shape[1], FFN_TN):
        o_ref[:, c:c + FFN_TN] += jnp.dot(h, wo_ref[:, c:c + FFN_TN].astype(BF16),
                                          preferred_element_type=F32)


def _swiglu(xn, w_in, w_out):
    m, d = xn.shape
    nf = D_FF // FFN_TF
    return pl.pallas_call(
        _ffn_body,
        grid=(m // FFN_TM, nf),
        in_specs=[pl.BlockSpec((FFN_TM, d), lambda i, f: (i, 0), pipeline_mode=pl.Buffered(1)),
                  pl.BlockSpec((d, FFN_TF), lambda i, f: (0, f)),
                  pl.BlockSpec((d, FFN_TF), lambda i, f: (0, f + nf)),
                  pl.BlockSpec((FFN_TF, d), lambda i, f: (f, 0))],
        out_specs=pl.BlockSpec((FFN_TM, d), lambda i, f: (i, 0), pipeline_mode=pl.Buffered(1)),
        out_shape=jax.ShapeDtypeStruct((m, d), F32),
        compiler_params=_params("parallel", "arbitrary", vmem_limit_bytes=FFN_VMEM_LIMIT_BYTES),
        name="swiglu",
    )(xn, w_in, w_in, w_out)


def _log_sigmoid(x):
    return jnp.minimum(x, 0.0) - jnp.log1p(jnp.exp(-jnp.abs(x)))


def _sample_rows_tile(o_ref, s_ref):
    o_ref[:SAMPLE_PAD, :] = s_ref[...]
    o_ref[SAMPLE_PAD:, :] = jnp.zeros((o_ref.shape[0] - SAMPLE_PAD, o_ref.shape[1]), o_ref.dtype)


def _mlstm_prompt_body(q_ref, k_ref, v_ref, so_ref, if_ref, bias_ref, hs_ref,
                       h_ref, c_ref, n_ref, m_ref):
    nc = SEQ // MLSTM_CHUNK
    step = pl.program_id(0)

    @pl.when(step < BATCH * nc)
    def _():
        _mlstm_chunk(lax.rem(step, nc) == 0, q_ref, k_ref, v_ref, so_ref, if_ref, bias_ref,
                     h_ref, c_ref, n_ref, m_ref)

    @pl.when(step == BATCH * nc)
    def _():
        _sample_rows_tile(h_ref, hs_ref)


def _mlstm_chunk(is_first_chunk, q_ref, k_ref, v_ref, so_ref, if_ref, bias_ref, h_ref, c_ref, n_ref, m_ref):
    L = MLSTM_CHUNK

    @pl.when(is_first_chunk)
    def _():
        c_ref[...] = jnp.zeros_like(c_ref)
        n_ref[...] = jnp.zeros_like(n_ref)
        m_ref[...] = jnp.zeros_like(m_ref)

    x = if_ref[...] + bias_ref[...]
    xt = x.T
    t_idx = lax.broadcasted_iota(jnp.int32, (L, L), 0)
    s_idx = lax.broadcasted_iota(jnp.int32, (L, L), 1)
    causal = s_idx <= t_idx
    b_cols = jnp.dot(causal.astype(F32), _log_sigmoid(x), precision=lax.Precision.HIGHEST,
                     preferred_element_type=F32)
    b_rows = b_cols.T
    for h in range(H_M):
        ig_col = x[:, h:h + 1]
        ig_row = xt[h:h + 1, :]
        b_col = b_cols[:, H_M + h:H_M + h + 1]
        b_row = b_rows[H_M + h:H_M + h + 1, :]
        m_prev = m_ref[h, 0:1, 0:1]
        d = jnp.where(causal, b_col - b_row + ig_row, -jnp.inf)
        m_inter = b_col + m_prev
        m_t = jnp.maximum(m_inter, jnp.max(d, axis=1, keepdims=True))
        w_inter = jnp.exp(m_inter - m_t)
        m_new = m_t[L - 1:L, :]
        b_last = b_col[L - 1:L, :]
        w_end = jnp.exp(b_last - b_col + ig_col - m_new)
        decay = jnp.exp(b_last + m_prev - m_new)

        qh = q_ref[:, h * DK_M:(h + 1) * DK_M]
        kf = k_ref[:, h * DK_M:(h + 1) * DK_M].astype(F32) * (DK_M ** -0.5)
        s = lax.dot_general(qh, kf.astype(BF16), (((1,), (1,)), ((), ())),
                            preferred_element_type=F32)
        c_old = c_ref[h]
        qc = jnp.dot(qh, c_old.astype(BF16), preferred_element_type=F32)
        n_tile = n_ref[h]
        n_old = n_tile[0:1, :]
        qn = lax.dot_general(qh, n_tile.astype(BF16), (((1,), (1,)), ((), ())),
                             preferred_element_type=F32)[:, 0:1]
        kw = kf * w_end
        a = s * jnp.exp(d - m_t)
        vh = v_ref[:, h * DV_M:(h + 1) * DV_M]
        num = jnp.dot(a.astype(BF16), vh, preferred_element_type=F32) + w_inter * qc
        den = jnp.sum(a, axis=1, keepdims=True) + w_inter * qn
        hh = num / jnp.maximum(jnp.abs(den), jnp.exp(-m_t))
        gate = so_ref[:, h * DV_M:(h + 1) * DV_M].astype(F32)
        h_ref[:, h * DV_M:(h + 1) * DV_M] = (gate * hh).astype(h_ref.dtype)

        dc = jnp.dot(kw.T.astype(BF16), vh, preferred_element_type=F32)
        c_ref[h] = decay * c_old + dc
        n_new = decay * n_old + jnp.sum(kw, axis=0, keepdims=True)
        n_ref[h] = jnp.broadcast_to(n_new, n_ref.shape[1:])
        m_ref[h] = jnp.broadcast_to(m_new, m_ref.shape[1:])


def _mlstm_prompt(qkv_m, sig_o, igfg, gate_bias, hm_sample):
    nc = SEQ // MLSTM_CHUNK
    L = MLSTM_CHUNK
    last = BATCH * nc - 1

    def rows(width, col):
        return pl.BlockSpec((L, width), lambda s: (jnp.minimum(s, last), col))

    def state(*shape):
        return pl.BlockSpec((None,) + shape, lambda s: (jnp.minimum(lax.div(s, nc), BATCH - 1),) + (0,) * len(shape))

    hm, c, n, m = pl.pallas_call(
        _mlstm_prompt_body,
        grid=(BATCH * nc + 1,),
        in_specs=[rows(QK_M_COLS, 0), rows(QK_M_COLS, 1), rows(V_M_COLS, 1), rows(V_M_COLS, 0),
                  rows(LANES, 0), pl.BlockSpec((1, LANES), lambda s: (0, 0)),
                  pl.BlockSpec((SAMPLE_PAD, V_M_COLS), lambda s: (0, 0))],
        out_specs=[pl.BlockSpec((L, V_M_COLS), lambda s: (s, 0)),
                   state(H_M, DK_M, DV_M), state(H_M, SUBLANES, DK_M), state(H_M, SUBLANES, LANES)],
        out_shape=[jax.ShapeDtypeStruct((M_ROWS, V_M_COLS), BF16),
                   jax.ShapeDtypeStruct((BATCH, H_M, DK_M, DV_M), F32),
                   jax.ShapeDtypeStruct((BATCH, H_M, SUBLANES, DK_M), F32),
                   jax.ShapeDtypeStruct((BATCH, H_M, SUBLANES, LANES), F32)],
        compiler_params=_params("arbitrary"),
        name="mlstm_prompt",
    )(qkv_m, qkv_m, qkv_m, sig_o, igfg, gate_bias, hm_sample)
    return hm, c, n[:, :, 0, :], m[:, :, 0, 0]


def _mlstm_sample_body(q_ref, qt_ref, k_ref, kt_ref, v_ref, so_ref, ig_ref, fg_ref, m0_ref,
                       c0_ref, n0_ref, h_ref, c_ref, n_ref, m_ref):
    for h in range(H_M):
        ig = ig_ref[h:h + 1, 0:1]
        lf = _log_sigmoid(fg_ref[h:h + 1, 0:1])
        m_prev = m0_ref[h:h + 1, 0:1]
        m_inter = lf + m_prev
        m_t = jnp.maximum(m_inter, ig)
        w_inter = jnp.exp(m_inter - m_t)
        q_row = q_ref[h:h + 1, :]
        k_row = k_ref[h:h + 1, :] * (DK_M ** -0.5)
        q_col = qt_ref[:, h:h + 1]
        k_col = kt_ref[:, h:h + 1] * (DK_M ** -0.5)
        v_row = v_ref[h:h + 1, :]
        c_old = c0_ref[h]
        n_old = n0_ref[h:h + 1, :]
        a = jnp.sum(q_row * k_row, axis=1, keepdims=True) * jnp.exp(ig - m_t)
        num = a * v_row + w_inter * jnp.sum(q_col * c_old, axis=0, keepdims=True)
        den = a + w_inter * jnp.sum(q_row * n_old, axis=1, keepdims=True)
        hh = num / jnp.maximum(jnp.abs(den), jnp.exp(-m_t))
        h_ref[h:h + 1, :] = so_ref[h:h + 1, :] * hh
        w_end = jnp.exp(ig - m_t)
        decay = jnp.exp(m_inter - m_t)
        c_ref[h] = decay * c_old + (k_col * w_end) * v_row
        n_ref[h:h + 1, :] = decay * n_old + k_row * w_end
        m_ref[h:h + 1, :] = jnp.broadcast_to(m_t, (1, LANES))


def _mlstm_sample(q, k, v, so, ig, fg, c0, n0, m0):
    nb = q.shape[0]

    def lanes(a):
        return jnp.broadcast_to(a[:, :, None], (nb, H_M, LANES))

    def per_b(*shape):
        return pl.BlockSpec((None,) + shape, lambda b: (b,) + (0,) * len(shape))

    return pl.pallas_call(
        _mlstm_sample_body,
        grid=(nb,),
        in_specs=[per_b(H_M, DK_M), per_b(DK_M, H_M), per_b(H_M, DK_M), per_b(DK_M, H_M),
                  per_b(H_M, DV_M), per_b(H_M, DV_M), per_b(H_M, LANES), per_b(H_M, LANES),
                  per_b(H_M, LANES), per_b(H_M, DK_M, DV_M), per_b(H_M, DK_M)],
        out_specs=[per_b(H_M, DV_M), per_b(H_M, DK_M, DV_M), per_b(H_M, DK_M), per_b(H_M, LANES)],
        out_shape=[jax.ShapeDtypeStruct((nb, H_M, DV_M), F32),
                   jax.ShapeDtypeStruct((nb, H_M, DK_M, DV_M), F32),
                   jax.ShapeDtypeStruct((nb, H_M, DK_M), F32),
                   jax.ShapeDtypeStruct((nb, H_M, LANES), F32)],
        compiler_params=_params("parallel"),
        name="mlstm_sample",
    )(q, jnp.swapaxes(q, 1, 2), k, jnp.swapaxes(k, 1, 2), v, so, lanes(ig), lanes(fg), lanes(m0),
      c0, n0)


def _alibi_slopes():
    n = N_GROUPS * H_A
    e = np.arange(1, n + 1, dtype=np.float64) * (-ALIBI_MAX_EXP / n)
    return np.exp2(e).reshape(N_GROUPS, H_A)


def _softmax_mix(lses, outs):
    m = jnp.maximum(jnp.maximum(lses[0], lses[1]), lses[2])
    e = [jnp.exp(x - m) for x in lses]
    z = e[0] + e[1] + e[2]
    acc = (e[0] / z) * outs[0]
    for g in (1, 2):
        acc = acc + (e[g] / z) * outs[g]
    return acc


def _attn_prompt_body(slopes_ref, *refs):
    n_tiles = BATCH * (SEQ // ATT_SB)
    tile = pl.program_id(0)

    @pl.when(tile < n_tiles)
    def _():
        _attn_prompt_tile(slopes_ref, *refs)

    @pl.when(tile == n_tiles)
    def _():
        _sample_rows_tile(refs[5 * N_GROUPS + 1], refs[5 * N_GROUPS])


def _attn_prompt_tile(slopes_ref, *refs):
    G = N_GROUPS
    q_refs, kc_refs, kp_refs = refs[0:G], refs[G:2 * G], refs[2 * G:3 * G]
    vc_refs, vp_refs = refs[3 * G:4 * G], refs[4 * G:5 * G]
    o_ref = refs[5 * G + 1]
    o_scr, l_scr = refs[5 * G + 2:6 * G + 2], refs[6 * G + 2:7 * G + 2]
    L = Q_BLOCK
    head = pl.program_id(1)
    first_key = jnp.where(lax.rem(pl.program_id(0), SEQ // ATT_SB) == 0, L, 0)
    t_idx = lax.broadcasted_iota(jnp.int32, (L, 2 * L), 0)
    s_idx = lax.broadcasted_iota(jnp.int32, (L, 2 * L), 1)
    step = t_idx + L - s_idx
    in_window = jnp.logical_and(step >= 0, step <= L)
    in_window_first = jnp.logical_and(in_window, s_idx >= first_key)
    nt = (((1,), (1,)), ((), ()))

    def rows(start, size, dil):
        return pl.ds(start, size) if dil == 1 else pl.ds(start, size, stride=dil)

    for g in range(G):
        dil = DILATIONS[g]
        alibi = step.astype(F32) * (slopes_ref[g, head] * dil)
        bias_in = jnp.where(in_window, alibi, -NEG_INF)
        bias_first = jnp.where(in_window_first, alibi, -NEG_INF)

        def sub_block(j, r, g=g, dil=dil, bias_in=bias_in, bias_first=bias_first):
            start = j * L * dil + r
            q = q_refs[g][rows(start, L, dil), :].astype(BF16)
            if j == 0:
                k2 = jnp.concatenate([kp_refs[g][rows(r, L, dil), :], kc_refs[g][rows(start, L, dil), :]], axis=0)
                v2 = jnp.concatenate([vp_refs[g][rows(r, L, dil), :], vc_refs[g][rows(start, L, dil), :]], axis=0)
                bias = bias_first
            else:
                k2 = kc_refs[g][rows(start - L * dil, 2 * L, dil), :]
                v2 = vc_refs[g][rows(start - L * dil, 2 * L, dil), :]
                bias = bias_in
            s = lax.dot_general(q, k2.astype(BF16), nt, preferred_element_type=F32)
            s = s * (DH_A ** -0.5) - bias
            m = jnp.max(s, axis=1, keepdims=True)
            p = jnp.exp(s - m)
            l = jnp.sum(p, axis=1, keepdims=True)
            o = jnp.dot(p.astype(BF16), v2.astype(BF16), preferred_element_type=F32) / l
            o_scr[g][rows(start, L, dil), :] = o
            l_scr[g][rows(start, L, dil), :] = jnp.broadcast_to(m + jnp.log(l), (L, LANES))

        for j in range(ATT_SB // (L * dil)):
            if dil == 1:
                sub_block(j, 0)
            else:
                def residue(r, carry, j=j):
                    sub_block(j, r)
                    return carry
                lax.fori_loop(0, dil, residue, 0, unroll=16)

    chunk = 256
    for c in range(0, ATT_SB, chunk):
        lses = [l_scr[g][c:c + chunk, :] for g in range(G)]
        outs = [o_scr[g][c:c + chunk, :] for g in range(G)]
        o_ref[c:c + chunk, :] = _softmax_mix(lses, outs).astype(o_ref.dtype)


def _attn_prompt(qkv_a, ha_sample):
    ns = SEQ // ATT_SB
    L = Q_BLOCK
    heads_per_part = N_GROUPS * H_A
    last = BATCH * ns - 1

    def cur(part, g):
        return pl.BlockSpec((ATT_SB, DH_A),
                            lambda t, h: (jnp.minimum(t, last), part * heads_per_part + g * H_A + h))

    def prev(part, g):
        span = L * DILATIONS[g]
        f = ATT_SB // span

        def index(t, h):
            t = jnp.minimum(t, last)
            b, i = lax.div(t, ns), lax.rem(t, ns)
            return b * ns * f + jnp.maximum(i * f - 1, 0), part * heads_per_part + g * H_A + h

        return pl.BlockSpec((span, DH_A), index)

    groups = range(N_GROUPS)
    in_specs = ([pl.BlockSpec(memory_space=pltpu.SMEM)]
                + [cur(0, g) for g in groups] + [cur(1, g) for g in groups] + [prev(1, g) for g in groups]
                + [cur(2, g) for g in groups] + [prev(2, g) for g in groups]
                + [pl.BlockSpec((SAMPLE_PAD, DH_A), lambda t, h: (0, h))])
    slopes = jnp.asarray(_alibi_slopes(), F32)
    return pl.pallas_call(
        _attn_prompt_body,
        grid=(BATCH * ns + 1, H_A),
        in_specs=in_specs,
        out_specs=pl.BlockSpec((ATT_SB, DH_A), lambda t, h: (t, h)),
        out_shape=jax.ShapeDtypeStruct((M_ROWS, A_GROUP_COLS), BF16),
        scratch_shapes=[pltpu.VMEM((ATT_SB, DH_A), F32)] * (2 * N_GROUPS),
        compiler_params=_params("parallel", "arbitrary"),
        name="attn_prompt",
    )(slopes, *([qkv_a] * (5 * N_GROUPS)), ha_sample)


def _attn_sample_body(slopes_ref, q_ref, kn_ref, vn_ref, c0_ref, c1_ref, c2_ref, o_ref):
    L = Q_BLOCK
    c_refs = (c0_ref, c1_ref, c2_ref)
    steps = (L - lax.broadcasted_iota(jnp.int32, (L, 1, 1), 0)).astype(F32)
    outs, lses = [], []
    for g in range(N_GROUPS):
        q = q_ref[g]
        slope = slopes_ref[g][:, 0:1]
        s = (jnp.sum(c_refs[g][:, 0] * q[None], axis=-1, keepdims=True) * (DH_A ** -0.5)
             - (slope * DILATIONS[g])[None] * steps)
        s_new = jnp.sum(q * kn_ref[g], axis=-1, keepdims=True) * (DH_A ** -0.5)
        m = jnp.maximum(jnp.max(s, axis=0), s_new)
        p = jnp.exp(s - m[None])
        p_new = jnp.exp(s_new - m)
        l = jnp.sum(p, axis=0) + p_new
        outs.append((jnp.sum(p * c_refs[g][:, 1], axis=0) + p_new * vn_ref[g]) / l)
        lses.append(m + jnp.log(l))
    o_ref[...] = _softmax_mix(lses, outs)


def _attn_sample(q, kn, vn, caches):
    nb = q.shape[0]
    small = pl.BlockSpec((None, N_GROUPS, H_A, DH_A), lambda b: (b, 0, 0, 0))
    cache_specs, cache_views = [], []
    for g, c in enumerate(caches):
        dil = DILATIONS[g]
        assert c.shape[1] == Q_BLOCK * dil
        cache_views.append(c.reshape(nb, Q_BLOCK, dil, 2, H_A, DH_A))
        cache_specs.append(pl.BlockSpec((None, Q_BLOCK, None, 2, H_A, DH_A), lambda b: (b, 0, 0, 0, 0, 0)))
    slopes = jnp.broadcast_to(jnp.asarray(_alibi_slopes(), F32)[:, :, None], (N_GROUPS, H_A, LANES))
    return pl.pallas_call(
        _attn_sample_body,
        grid=(nb,),
        in_specs=[pl.BlockSpec((N_GROUPS, H_A, LANES), lambda b: (0, 0, 0)), small, small, small] + cache_specs,
        out_specs=pl.BlockSpec((None, H_A, DH_A), lambda b: (b, 0, 0)),
        out_shape=jax.ShapeDtypeStruct((nb, H_A, DH_A), F32),
        compiler_params=_params("parallel"),
        name="attn_sample",
    )(slopes, q, kn, vn, *cache_views)


def _logistic(x):
    return 0.5 * jnp.tanh(0.5 * x) + 0.5


def _sigmoid_epilogue(prods, extras):
    return _logistic(prods[0])


def _gate_epilogue(prods, extras):
    return _logistic(prods[0] + extras[0])


def _plain_epilogue(prods, extras):
    return prods[0]


def _mix_epilogue(prods, extras):
    return extras[0].astype(F32) * prods[0] + extras[1].astype(F32) * prods[1]


def _residual_epilogue(prods, extras):
    return extras[0] + prods[0]


def _padded_sample_rows(sample_rows):
    nb, c = sample_rows.shape
    return jnp.concatenate([sample_rows.astype(BF16), jnp.zeros((SAMPLE_PAD - nb, c), BF16)], axis=0)


def kernel(x_prompt, x_sample, state_mlstm_C, state_mlstm_n, state_mlstm_m, cache_win1_kv, cache_win2_kv, cache_win3_kv, g_ffn1, w_ffn1_in, w_ffn1_out, g_mix, w_in, b_gate, b_igate, b_fgate, w_mlstm_out, w_attn_out, w_out, g_ffn2, w_ffn2_in, w_ffn2_out, g_final):
    d = D_MODEL
    nb = DEC_BATCH
    xp, xs = x_prompt.reshape(N_PROMPT, d), x_sample.reshape(nb, d)

    w_in_nk = w_in[0].T
    gate_bias = jnp.pad(jnp.concatenate([b_igate[0], b_fgate[0]]), (0, LANES - 2 * H_M)).reshape(1, LANES)

    xn = _rmsnorm_inputs(xp, xs, g_ffn1[0])
    y = _swiglu(xn, w_ffn1_in[0], w_ffn1_out[0])
    x1, u = _add_rmsnorm_inputs(xp, xs, y, g_mix[0], 0.5)

    qkv_m, w_mo = _mm("proj_qkv_m", [u], [(0, w_in_nk, 0, True)], [], COL_O_M, BF16, _plain_epilogue,
                      side=[(w_mlstm_out[0], 512, 512)])
    sig_o, w_ao = _mm("proj_o_m", [u], [(0, w_in_nk, COL_O_M, True)], [], V_M_COLS, BF16, _sigmoid_epilogue,
                      side=[(w_attn_out[0], 256, 512)])
    igfg = _mm("proj_if", [u], [(0, w_in_nk, COL_IF, True)], [], LANES, F32, _plain_epilogue, tn=LANES)
    qkv_a = _mm("proj_qkv_a", [u], [(0, w_in_nk, COL_Q_A, True)], [], 3 * A_COLS, F32, _plain_epilogue)
    gates, w_o = _mm("proj_gates", [u], [(0, w_in_nk, COL_GATE, True)],
                     [(b_gate[0].reshape(1, 2 * d), "row", 0)], 2 * d, BF16, _gate_epilogue,
                     side=[(w_out[0], 512, 512)])

    qkv_s = qkv_m[N_PROMPT:N_PROMPT + nb].astype(F32)
    if_s = igfg[N_PROMPT:N_PROMPT + nb] + gate_bias
    hm_s, c_s, n_s, m_s = _mlstm_sample(
        qkv_s[:, :QK_M_COLS].reshape(nb, H_M, DK_M),
        qkv_s[:, QK_M_COLS:2 * QK_M_COLS].reshape(nb, H_M, DK_M),
        qkv_s[:, 2 * QK_M_COLS:].reshape(nb, H_M, DV_M),
        sig_o[N_PROMPT:N_PROMPT + nb].astype(F32).reshape(nb, H_M, DV_M),
        if_s[:, :H_M], if_s[:, H_M:2 * H_M],
        state_mlstm_C[0], state_mlstm_n[0], state_mlstm_m[0])
    hm, c_p, n_p, m_p = _mlstm_prompt(qkv_m, sig_o, igfg, gate_bias,
                                      _padded_sample_rows(hm_s.reshape(nb, V_M_COLS)))

    qkv_as = qkv_a[N_PROMPT:N_PROMPT + nb].reshape(nb, 3, N_GROUPS, H_A, DH_A)
    ha_s = _attn_sample(qkv_as[:, 0], qkv_as[:, 1], qkv_as[:, 2],
                        (cache_win1_kv[0], cache_win2_kv[0], cache_win3_kv[0]))
    ha = _attn_prompt(qkv_a, _padded_sample_rows(ha_s.reshape(nb, A_GROUP_COLS)))

    z = _mm("mix", [hm, ha], [(0, w_mo, 0, False), (1, w_ao, 0, False)],
            [(gates, "tile", 0), (gates, "tile", d)], d, BF16, _mix_epilogue)
    x2 = _mm("proj_out", [z], [(0, w_o, 0, False)], [(x1, "tile", 0)], d, F32, _residual_epilogue)

    xn2 = _rmsnorm(x2, g_ffn2[0], BF16)
    y2 = _swiglu(xn2, w_ffn2_in[0], w_ffn2_out[0])
    out_p, out_s = _add_final_norm(x2, y2, g_final, 0.5)

    win_p, win_s = [], []
    for g, w in enumerate(WINDOWS):
        n_last = min(w, SEQ)
        k0, v0 = A_COLS + g * A_GROUP_COLS, 2 * A_COLS + g * A_GROUP_COLS
        per_seq = []
        for b in range(BATCH):
            last = qkv_a[(b + 1) * SEQ - n_last:(b + 1) * SEQ]
            per_seq.append(jnp.stack([last[:, k0:k0 + A_GROUP_COLS], last[:, v0:v0 + A_GROUP_COLS]], axis=1))
        win_p.append(jnp.stack(per_seq).reshape(1, BATCH, n_last, 2, H_A, DH_A))
        win_s.append(qkv_as[:, 1:3, g].reshape(1, nb, 1, 2, H_A, DH_A))
    return (out_p.reshape(BATCH, SEQ, d), out_s[:nb].reshape(nb, 1, d),
            c_p[None], n_p[None], m_p[None],
            c_s[None], n_s[None], m_s[:, :, 0][None],
            win_p[0], win_p[1], win_p[2], win_s[0], win_s[1], win_s[2])
```

```python
import functools

import numpy as np
import jax
import jax.numpy as jnp
from jax import lax
from jax.experimental import pallas as pl
from jax.experimental.pallas import tpu as pltpu

F32 = jnp.float32
BF16 = jnp.bfloat16

D_MODEL = 4096
BATCH = 2
SEQ = 4096
DEC_BATCH = 32
H_M = 8
DK_M = 256
DV_M = 512
MLSTM_CHUNK = 128
WINDOWS = (128, 512, 2048)
DILATIONS = (1, 4, 16)
N_GROUPS = 3
H_A = 8
DH_A = 128
Q_BLOCK = 128
ALIBI_MAX_EXP = 8.0
D_FF = 11008
EPS = 1e-6
NEG_INF = -1e30

QK_M_COLS = H_M * DK_M
V_M_COLS = H_M * DV_M
A_GROUP_COLS = H_A * DH_A
A_COLS = N_GROUPS * A_GROUP_COLS
COL_O_M = 2 * QK_M_COLS + V_M_COLS
COL_IF = COL_O_M + V_M_COLS
COL_Q_A = COL_IF + 2 * H_M
COL_GATE = COL_Q_A + 3 * A_COLS
LANES = 128
SUBLANES = 8

N_PROMPT = BATCH * SEQ
SAMPLE_PAD = 64
M_ROWS = N_PROMPT + SAMPLE_PAD

V7X_VMEM_BYTES = 64 * 1024 * 1024
VMEM_LIMIT_BYTES = V7X_VMEM_BYTES - 8 * 1024 * 1024

ROW_TILE = 192
OUT_ROW_TILE = 256
MM_TM = 1376
MM_TN = 512
FFN_TM = 1376
FFN_TF = 256
FFN_TN = 512
ATT_SB = 2048


def _params(*semantics, vmem_limit_bytes=VMEM_LIMIT_BYTES):
    return pltpu.CompilerParams(dimension_semantics=semantics, vmem_limit_bytes=vmem_limit_bytes)


def _rms(x, g):
    ms = jnp.mean(x * x, axis=-1, keepdims=True)
    return x * lax.rsqrt(ms + EPS) * g


def _rmsnorm_body(x_ref, g_ref, o_ref):
    o_ref[...] = _rms(x_ref[...], g_ref[...]).astype(o_ref.dtype)


def _rmsnorm(x, g):
    m, d = x.shape
    return pl.pallas_call(
        _rmsnorm_body,
        grid=(m // ROW_TILE,),
        in_specs=[pl.BlockSpec((ROW_TILE, d), lambda i: (i, 0)),
                  pl.BlockSpec((1, d), lambda i: (0, 0))],
        out_specs=pl.BlockSpec((ROW_TILE, d), lambda i: (i, 0)),
        out_shape=jax.ShapeDtypeStruct((m, d), BF16),
        compiler_params=_params("parallel"),
        name="rmsnorm",
    )(x, g.reshape(1, d))


def _input_tile(xp_ref, xs_ref):
    is_sample_tile = pl.program_id(0) == N_PROMPT // OUT_ROW_TILE
    pad = jnp.zeros((OUT_ROW_TILE - xs_ref.shape[0], xs_ref.shape[1]), F32)
    return jnp.where(is_sample_tile, jnp.concatenate([xs_ref[...], pad], axis=0), xp_ref[...])


def _input_specs(d, nb):
    n_p = N_PROMPT // OUT_ROW_TILE
    return [pl.BlockSpec((OUT_ROW_TILE, d), lambda i: (jnp.minimum(i, n_p - 1), 0)),
            pl.BlockSpec((nb, d), lambda i: (0, 0))]


def _rmsnorm_inputs_body(xp_ref, xs_ref, g_ref, o_ref):
    o_ref[...] = _rms(_input_tile(xp_ref, xs_ref), g_ref[...]).astype(o_ref.dtype)


def _rmsnorm_inputs(xp, xs, g):
    d = xp.shape[1]
    row = pl.BlockSpec((OUT_ROW_TILE, d), lambda i: (i, 0))
    return pl.pallas_call(
        _rmsnorm_inputs_body,
        grid=(N_PROMPT // OUT_ROW_TILE + 1,),
        in_specs=_input_specs(d, xs.shape[0]) + [pl.BlockSpec((1, d), lambda i: (0, 0))],
        out_specs=row,
        out_shape=jax.ShapeDtypeStruct((M_ROWS, d), BF16),
        compiler_params=_params("parallel"),
        name="rmsnorm_inputs",
    )(xp, xs, g.reshape(1, d))


def _add_rmsnorm_inputs_body(xp_ref, xs_ref, y_ref, g_ref, xo_ref, no_ref, *, scale):
    x = _input_tile(xp_ref, xs_ref) + scale * y_ref[...]
    xo_ref[...] = x
    no_ref[...] = _rms(x, g_ref[...]).astype(no_ref.dtype)


def _add_rmsnorm_inputs(xp, xs, y, g, scale):
    d = xp.shape[1]
    row = pl.BlockSpec((OUT_ROW_TILE, d), lambda i: (i, 0))
    return pl.pallas_call(
        functools.partial(_add_rmsnorm_inputs_body, scale=scale),
        grid=(N_PROMPT // OUT_ROW_TILE + 1,),
        in_specs=_input_specs(d, xs.shape[0]) + [row, pl.BlockSpec((1, d), lambda i: (0, 0))],
        out_specs=[row, row],
        out_shape=[jax.ShapeDtypeStruct((M_ROWS, d), F32), jax.ShapeDtypeStruct((M_ROWS, d), BF16)],
        compiler_params=_params("parallel"),
        name="add_rmsnorm_inputs",
    )(xp, xs, y, g.reshape(1, d))


def _add_final_norm_body(x_ref, y_ref, g_ref, op_ref, os_ref, *, scale):
    out = _rms(x_ref[...] + scale * y_ref[...], g_ref[...])
    is_sample_tile = pl.program_id(0) == N_PROMPT // OUT_ROW_TILE

    @pl.when(jnp.logical_not(is_sample_tile))
    def _():
        op_ref[...] = out

    @pl.when(is_sample_tile)
    def _():
        os_ref[...] = out[:SAMPLE_PAD]


def _add_final_norm(x, y, g, scale):
    m, d = x.shape
    n_p = N_PROMPT // OUT_ROW_TILE
    row = pl.BlockSpec((OUT_ROW_TILE, d), lambda i: (i, 0))
    return pl.pallas_call(
        functools.partial(_add_final_norm_body, scale=scale),
        grid=(n_p + 1,),
        in_specs=[row, row, pl.BlockSpec((1, d), lambda i: (0, 0))],
        out_specs=[pl.BlockSpec((OUT_ROW_TILE, d), lambda i: (jnp.minimum(i, n_p - 1), 0)),
                   pl.BlockSpec((SAMPLE_PAD, d), lambda i: (0, 0))],
        out_shape=[jax.ShapeDtypeStruct((N_PROMPT, d), F32), jax.ShapeDtypeStruct((SAMPLE_PAD, d), F32)],
        compiler_params=_params("arbitrary"),
        name="add_final_norm",
    )(x, y, g.reshape(1, d))


def _side_cast_specs(side, n_inner):
    in_specs, out_specs, out_shapes = [], [], []
    for src, rb, cb in side:
        r, c = src.shape
        n_cb = c // cb
        last = (r // rb) * n_cb - 1

        def index(i, j, n_cb=n_cb, last=last):
            blk = jnp.minimum(i * n_inner + j, last)
            return lax.div(blk, n_cb), lax.rem(blk, n_cb)

        in_specs.append(pl.BlockSpec((rb, cb), index))
        out_specs.append(pl.BlockSpec((rb, cb), index))
        out_shapes.append(jax.ShapeDtypeStruct((r, c), BF16))
    return in_specs, out_specs, out_shapes


def _side_cast(src_refs, dst_refs):
    for src_ref, dst_ref in zip(src_refs, dst_refs):
        dst_ref[...] = src_ref[...].astype(dst_ref.dtype)


def _mm_body(*refs, dots, n_a, n_extra, n_side, epilogue):
    a_refs = refs[:n_a]
    pos = n_a
    weights = []
    for ai, w_is_nk, shift in dots:
        if shift:
            w = jnp.concatenate([refs[pos][shift:, :].astype(BF16), refs[pos + 1][...].astype(BF16)], axis=0)
            pos += 2
        else:
            w = refs[pos][...].astype(BF16)
            pos += 1
        weights.append(w)
    n_in = pos + n_extra
    e_refs = refs[pos:n_in]
    o_ref = refs[n_in + n_side]
    prods = []
    for (ai, w_is_nk, _), w in zip(dots, weights):
        contract = (((1,), (1,)), ((), ())) if w_is_nk else (((1,), (0,)), ((), ()))
        prods.append(lax.dot_general(a_refs[ai][...], w, contract, preferred_element_type=F32))
    o_ref[...] = epilogue(prods, [e[...] for e in e_refs]).astype(o_ref.dtype)
    _side_cast(refs[n_in:n_in + n_side], refs[n_in + n_side + 1:])


def _mm(name, a_list, w_list, extras, n, out_dtype, epilogue, tm=MM_TM, tn=MM_TN, side=()):
    m = a_list[0].shape[0]
    in_specs = [pl.BlockSpec((tm, a.shape[1]), lambda i, j: (i, 0)) for a in a_list]
    operands = list(a_list)
    dots = []
    for ai, w, off, w_is_nk in w_list:
        shift = off % tn if w_is_nk else 0
        if w_is_nk:
            in_specs.append(pl.BlockSpec((tn, w.shape[1]), lambda i, j, o=off // tn: (j + o, 0)))
        else:
            in_specs.append(pl.BlockSpec((w.shape[0], tn), lambda i, j, o=off // tn: (0, j + o)))
        operands.append(w)
        if shift:
            in_specs.append(pl.BlockSpec((shift, w.shape[1]),
                                         lambda i, j, o=off // tn, r=tn // shift: ((j + o + 1) * r, 0)))
            operands.append(w)
        dots.append((ai, w_is_nk, shift))
    for arr, kind, off in extras:
        if kind == "tile":
            in_specs.append(pl.BlockSpec((tm, tn), lambda i, j, o=off // tn: (i, j + o)))
        else:
            in_specs.append(pl.BlockSpec((1, tn), lambda i, j, o=off // tn: (0, j + o)))
        operands.append(arr)
    side_in, side_out, side_shapes = _side_cast_specs(side, n // tn)
    body = functools.partial(_mm_body, dots=tuple(dots),
                             n_a=len(a_list), n_extra=len(extras), n_side=len(side), epilogue=epilogue)
    outs = pl.pallas_call(
        body,
        grid=(m // tm, n // tn),
        in_specs=in_specs + side_in,
        out_specs=[pl.BlockSpec((tm, tn), lambda i, j: (i, j))] + side_out,
        out_shape=[jax.ShapeDtypeStruct((m, n), out_dtype)] + side_shapes,
        compiler_params=_params("arbitrary", "arbitrary"),
        name=name,
    )(*operands, *[src for src, _, _ in side])
    return outs if side else outs[0]


def _ffn_body(x_ref, wg_ref, wu_ref, wo_ref, o_ref):
    x = x_ref[...]
    g = jnp.dot(x, wg_ref[...].astype(BF16), preferred_element_type=F32)
    u = jnp.dot(x, wu_ref[...].astype(BF16), preferred_element_type=F32)
    h = (g * jax.nn.sigmoid(g) * u).astype(BF16)

    @pl.when(pl.program_id(1) == 0)
    def _():
        o_ref[...] = jnp.zeros_like(o_ref)

    for c in range(0, o_ref.shape[1], FFN_TN):
        o_ref[:, c:c + FFN_TN] += jnp.dot(h, wo_ref[:, c:c + FFN_TN].astype(BF16),
                                          preferred_element_type=F32)


def _swiglu(xn, w_in, w_out):
    m, d = xn.shape
    nf = D_FF // FFN_TF
    weight_tiles = 3 * d * FFN_TF
    vmem_bytes = (FFN_TM * d * (2 + 4)
                  + weight_tiles * (2 * 4 + 2))
    assert vmem_bytes <= V7X_VMEM_BYTES
    return pl.pallas_call(
        _ffn_body,
        grid=(m // FFN_TM, nf),
        in_specs=[pl.BlockSpec((FFN_TM, d), lambda i, f: (i, 0), pipeline_mode=pl.Buffered(1)),
                  pl.BlockSpec((d, FFN_TF), lambda i, f: (0, f)),
                  pl.BlockSpec((d, FFN_TF), lambda i, f: (0, f + nf)),
                  pl.BlockSpec((FFN_TF, d), lambda i, f: (f, 0))],
        out_specs=pl.BlockSpec((FFN_TM, d), lambda i, f: (i, 0), pipeline_mode=pl.Buffered(1)),
        out_shape=jax.ShapeDtypeStruct((m, d), F32),
        compiler_params=_params("parallel", "arbitrary", vmem_limit_bytes=vmem_bytes),
        name="swiglu",
    )(xn, w_in, w_in, w_out)


def _log_sigmoid(x):
    return jnp.minimum(x, 0.0) - jnp.log1p(jnp.exp(-jnp.abs(x)))


def _sample_rows_tile(o_ref, s_ref):
    o_ref[:SAMPLE_PAD, :] = s_ref[...]
    o_ref[SAMPLE_PAD:, :] = jnp.zeros((o_ref.shape[0] - SAMPLE_PAD, o_ref.shape[1]), o_ref.dtype)


def _mlstm_prompt_body(q_ref, k_ref, v_ref, so_ref, if_ref, bias_ref, hs_ref,
                       h_ref, c_ref, n_ref, m_ref):
    nc = SEQ // MLSTM_CHUNK
    step = pl.program_id(0)

    @pl.when(step < BATCH * nc)
    def _():
        _mlstm_chunk(lax.rem(step, nc) == 0, q_ref, k_ref, v_ref, so_ref, if_ref, bias_ref,
                     h_ref, c_ref, n_ref, m_ref)

    @pl.when(step == BATCH * nc)
    def _():
        _sample_rows_tile(h_ref, hs_ref)


def _mlstm_chunk(is_first_chunk, q_ref, k_ref, v_ref, so_ref, if_ref, bias_ref, h_ref, c_ref, n_ref, m_ref):
    L = MLSTM_CHUNK

    @pl.when(is_first_chunk)
    def _():
        c_ref[...] = jnp.zeros_like(c_ref)
        n_ref[...] = jnp.zeros_like(n_ref)
        m_ref[...] = jnp.zeros_like(m_ref)

    x = if_ref[...] + bias_ref[...]
    xt = x.T
    t_idx = lax.broadcasted_iota(jnp.int32, (L, L), 0)
    s_idx = lax.broadcasted_iota(jnp.int32, (L, L), 1)
    causal = s_idx <= t_idx
    b_cols = jnp.dot(causal.astype(F32), _log_sigmoid(x), precision=lax.Precision.HIGHEST,
                     preferred_element_type=F32)
    b_rows = b_cols.T
    for h in range(H_M):
        ig_col = x[:, h:h + 1]
        ig_row = xt[h:h + 1, :]
        b_col = b_cols[:, H_M + h:H_M + h + 1]
        b_row = b_rows[H_M + h:H_M + h + 1, :]
        m_prev = m_ref[h, 0:1, 0:1]
        d = jnp.where(causal, b_col - b_row + ig_row, -jnp.inf)
        m_inter = b_col + m_prev
        m_t = jnp.maximum(m_inter, jnp.max(d, axis=1, keepdims=True))
        w_inter = jnp.exp(m_inter - m_t)
        m_new = m_t[L - 1:L, :]
        b_last = b_col[L - 1:L, :]
        w_end = jnp.exp(b_last - b_col + ig_col - m_new)
        decay = jnp.exp(b_last + m_prev - m_new)

        qh = q_ref[:, h * DK_M:(h + 1) * DK_M]
        kf = k_ref[:, h * DK_M:(h + 1) * DK_M].astype(F32) * (DK_M ** -0.5)
        s = lax.dot_general(qh, kf.astype(BF16), (((1,), (1,)), ((), ())),
                            preferred_element_type=F32)
        c_old = c_ref[h]
        qc = jnp.dot(qh, c_old.astype(BF16), preferred_element_type=F32)
        n_tile = n_ref[h]
        n_old = n_tile[0:1, :]
        qn = lax.dot_general(qh, n_tile.astype(BF16), (((1,), (1,)), ((), ())),
                             preferred_element_type=F32)[:, 0:1]
        kw = kf * w_end
        a = s * jnp.exp(d - m_t)
        vh = v_ref[:, h * DV_M:(h + 1) * DV_M]
        num = jnp.dot(a.astype(BF16), vh, preferred_element_type=F32) + w_inter * qc
        den = jnp.sum(a, axis=1, keepdims=True) + w_inter * qn
        hh = num / jnp.maximum(jnp.abs(den), jnp.exp(-m_t))
        gate = so_ref[:, h * DV_M:(h + 1) * DV_M].astype(F32)
        h_ref[:, h * DV_M:(h + 1) * DV_M] = (gate * hh).astype(h_ref.dtype)

        dc = jnp.dot(kw.T.astype(BF16), vh, preferred_element_type=F32)
        c_ref[h] = decay * c_old + dc
        n_new = decay * n_old + jnp.sum(kw, axis=0, keepdims=True)
        n_ref[h] = jnp.broadcast_to(n_new, n_ref.shape[1:])
        m_ref[h] = jnp.broadcast_to(m_new, m_ref.shape[1:])


def _mlstm_prompt(qkv_m, sig_o, igfg, gate_bias, hm_sample):
    nc = SEQ // MLSTM_CHUNK
    L = MLSTM_CHUNK
    last = BATCH * nc - 1

    def rows(width, col):
        return pl.BlockSpec((L, width), lambda s: (jnp.minimum(s, last), col))

    def state(*shape):
        return pl.BlockSpec((None,) + shape, lambda s: (jnp.minimum(lax.div(s, nc), BATCH - 1),) + (0,) * len(shape))

    hm, c, n, m = pl.pallas_call(
        _mlstm_prompt_body,
        grid=(BATCH * nc + 1,),
        in_specs=[rows(QK_M_COLS, 0), rows(QK_M_COLS, 1), rows(V_M_COLS, 1), rows(V_M_COLS, 0),
                  rows(LANES, 0), pl.BlockSpec((1, LANES), lambda s: (0, 0)),
                  pl.BlockSpec((SAMPLE_PAD, V_M_COLS), lambda s: (0, 0))],
        out_specs=[pl.BlockSpec((L, V_M_COLS), lambda s: (s, 0)),
                   state(H_M, DK_M, DV_M), state(H_M, SUBLANES, DK_M), state(H_M, SUBLANES, LANES)],
        out_shape=[jax.ShapeDtypeStruct((M_ROWS, V_M_COLS), BF16),
                   jax.ShapeDtypeStruct((BATCH, H_M, DK_M, DV_M), F32),
                   jax.ShapeDtypeStruct((BATCH, H_M, SUBLANES, DK_M), F32),
                   jax.ShapeDtypeStruct((BATCH, H_M, SUBLANES, LANES), F32)],
        compiler_params=_params("arbitrary"),
        name="mlstm_prompt",
    )(qkv_m, qkv_m, qkv_m, sig_o, igfg, gate_bias, hm_sample)
    return hm, c, n[:, :, 0, :], m[:, :, 0, 0]


def _mlstm_sample_body(q_ref, qt_ref, k_ref, kt_ref, v_ref, so_ref, ig_ref, fg_ref, m0_ref,
                       c0_ref, n0_ref, h_ref, c_ref, n_ref, m_ref):
    for h in range(H_M):
        ig = ig_ref[h:h + 1, 0:1]
        lf = _log_sigmoid(fg_ref[h:h + 1, 0:1])
        m_prev = m0_ref[h:h + 1, 0:1]
        m_inter = lf + m_prev
        m_t = jnp.maximum(m_inter, ig)
        w_inter = jnp.exp(m_inter - m_t)
        q_row = q_ref[h:h + 1, :]
        k_row = k_ref[h:h + 1, :] * (DK_M ** -0.5)
        q_col = qt_ref[:, h:h + 1]
        k_col = kt_ref[:, h:h + 1] * (DK_M ** -0.5)
        v_row = v_ref[h:h + 1, :]
        c_old = c0_ref[h]
        n_old = n0_ref[h:h + 1, :]
        a = jnp.sum(q_row * k_row, axis=1, keepdims=True) * jnp.exp(ig - m_t)
        num = a * v_row + w_inter * jnp.sum(q_col * c_old, axis=0, keepdims=True)
        den = a + w_inter * jnp.sum(q_row * n_old, axis=1, keepdims=True)
        hh = num / jnp.maximum(jnp.abs(den), jnp.exp(-m_t))
        h_ref[h:h + 1, :] = so_ref[h:h + 1, :] * hh
        w_end = jnp.exp(ig - m_t)
        decay = jnp.exp(m_inter - m_t)
        c_ref[h] = decay * c_old + (k_col * w_end) * v_row
        n_ref[h:h + 1, :] = decay * n_old + k_row * w_end
        m_ref[h:h + 1, :] = jnp.broadcast_to(m_t, (1, LANES))


def _mlstm_sample(q, k, v, so, ig, fg, c0, n0, m0):
    nb = q.shape[0]

    def lanes(a):
        return jnp.broadcast_to(a[:, :, None], (nb, H_M, LANES))

    def per_b(*shape):
        return pl.BlockSpec((None,) + shape, lambda b: (b,) + (0,) * len(shape))

    return pl.pallas_call(
        _mlstm_sample_body,
        grid=(nb,),
        in_specs=[per_b(H_M, DK_M), per_b(DK_M, H_M), per_b(H_M, DK_M), per_b(DK_M, H_M),
                  per_b(H_M, DV_M), per_b(H_M, DV_M), per_b(H_M, LANES), per_b(H_M, LANES),
                  per_b(H_M, LANES), per_b(H_M, DK_M, DV_M), per_b(H_M, DK_M)],
        out_specs=[per_b(H_M, DV_M), per_b(H_M, DK_M, DV_M), per_b(H_M, DK_M), per_b(H_M, LANES)],
        out_shape=[jax.ShapeDtypeStruct((nb, H_M, DV_M), F32),
                   jax.ShapeDtypeStruct((nb, H_M, DK_M, DV_M), F32),
                   jax.ShapeDtypeStruct((nb, H_M, DK_M), F32),
                   jax.ShapeDtypeStruct((nb, H_M, LANES), F32)],
        compiler_params=_params("parallel"),
        name="mlstm_sample",
    )(q, jnp.swapaxes(q, 1, 2), k, jnp.swapaxes(k, 1, 2), v, so, lanes(ig), lanes(fg), lanes(m0),
      c0, n0)


def _alibi_slopes():
    n = N_GROUPS * H_A
    e = np.arange(1, n + 1, dtype=np.float64) * (-ALIBI_MAX_EXP / n)
    return np.exp2(e).reshape(N_GROUPS, H_A)


def _softmax_mix(lses, outs):
    m = jnp.maximum(jnp.maximum(lses[0], lses[1]), lses[2])
    e = [jnp.exp(x - m) for x in lses]
    z = e[0] + e[1] + e[2]
    acc = (e[0] / z) * outs[0]
    for g in (1, 2):
        acc = acc + (e[g] / z) * outs[g]
    return acc


def _attn_prompt_body(slopes_ref, *refs):
    n_tiles = BATCH * (SEQ // ATT_SB)
    tile = pl.program_id(0)

    @pl.when(tile < n_tiles)
    def _():
        _attn_prompt_tile(slopes_ref, *refs)

    @pl.when(tile == n_tiles)
    def _():
        _sample_rows_tile(refs[5 * N_GROUPS + 1], refs[5 * N_GROUPS])


def _attn_prompt_tile(slopes_ref, *refs):
    G = N_GROUPS
    q_refs, kc_refs, kp_refs = refs[0:G], refs[G:2 * G], refs[2 * G:3 * G]
    vc_refs, vp_refs = refs[3 * G:4 * G], refs[4 * G:5 * G]
    o_ref = refs[5 * G + 1]
    o_scr, l_scr = refs[5 * G + 2:6 * G + 2], refs[6 * G + 2:7 * G + 2]
    L = Q_BLOCK
    head = pl.program_id(1)
    first_key = jnp.where(lax.rem(pl.program_id(0), SEQ // ATT_SB) == 0, L, 0)
    t_idx = lax.broadcasted_iota(jnp.int32, (L, 2 * L), 0)
    s_idx = lax.broadcasted_iota(jnp.int32, (L, 2 * L), 1)
    step = t_idx + L - s_idx
    in_window = jnp.logical_and(step >= 0, step <= L)
    in_window_first = jnp.logical_and(in_window, s_idx >= first_key)
    nt = (((1,), (1,)), ((), ()))

    def rows(start, size, dil):
        return pl.ds(start, size) if dil == 1 else pl.ds(start, size, stride=dil)

    for g in range(G):
        dil = DILATIONS[g]
        alibi = step.astype(F32) * (slopes_ref[g, head] * dil)
        bias_in = jnp.where(in_window, alibi, -NEG_INF)
        bias_first = jnp.where(in_window_first, alibi, -NEG_INF)

        def sub_block(j, r, g=g, dil=dil, bias_in=bias_in, bias_first=bias_first):
            start = j * L * dil + r
            q = q_refs[g][rows(start, L, dil), :].astype(BF16)
            if j == 0:
                k2 = jnp.concatenate([kp_refs[g][rows(r, L, dil), :], kc_refs[g][rows(start, L, dil), :]], axis=0)
                v2 = jnp.concatenate([vp_refs[g][rows(r, L, dil), :], vc_refs[g][rows(start, L, dil), :]], axis=0)
                bias = bias_first
            else:
                k2 = kc_refs[g][rows(start - L * dil, 2 * L, dil), :]
                v2 = vc_refs[g][rows(start - L * dil, 2 * L, dil), :]
                bias = bias_in
            s = lax.dot_general(q, k2.astype(BF16), nt, preferred_element_type=F32)
            s = s * (DH_A ** -0.5) - bias
            m = jnp.max(s, axis=1, keepdims=True)
            p = jnp.exp(s - m)
            l = jnp.sum(p, axis=1, keepdims=True)
            o = jnp.dot(p.astype(BF16), v2.astype(BF16), preferred_element_type=F32) / l
            o_scr[g][rows(start, L, dil), :] = o
            l_scr[g][rows(start, L, dil), :] = jnp.broadcast_to(m + jnp.log(l), (L, LANES))

        for j in range(ATT_SB // (L * dil)):
            if dil == 1:
                sub_block(j, 0)
            else:
                def residue(r, carry, j=j):
                    sub_block(j, r)
                    return carry
                lax.fori_loop(0, dil, residue, 0, unroll=16)

    chunk = 256
    for c in range(0, ATT_SB, chunk):
        lses = [l_scr[g][c:c + chunk, :] for g in range(G)]
        outs = [o_scr[g][c:c + chunk, :] for g in range(G)]
        o_ref[c:c + chunk, :] = _softmax_mix(lses, outs).astype(o_ref.dtype)


def _attn_prompt(qkv_a, ha_sample):
    ns = SEQ // ATT_SB
    L = Q_BLOCK
    heads_per_part = N_GROUPS * H_A
    last = BATCH * ns - 1

    def cur(part, g):
        return pl.BlockSpec((ATT_SB, DH_A),
                            lambda t, h: (jnp.minimum(t, last), part * heads_per_part + g * H_A + h))

    def prev(part, g):
        span = L * DILATIONS[g]
        f = ATT_SB // span

        def index(t, h):
            t = jnp.minimum(t, last)
            b, i = lax.div(t, ns), lax.rem(t, ns)
            return b * ns * f + jnp.maximum(i * f - 1, 0), part * heads_per_part + g * H_A + h

        return pl.BlockSpec((span, DH_A), index)

    groups = range(N_GROUPS)
    in_specs = ([pl.BlockSpec(memory_space=pltpu.SMEM)]
                + [cur(0, g) for g in groups] + [cur(1, g) for g in groups] + [prev(1, g) for g in groups]
                + [cur(2, g) for g in groups] + [prev(2, g) for g in groups]
                + [pl.BlockSpec((SAMPLE_PAD, DH_A), lambda t, h: (0, h))])
    slopes = jnp.asarray(_alibi_slopes(), F32)
    return pl.pallas_call(
        _attn_prompt_body,
        grid=(BATCH * ns + 1, H_A),
        in_specs=in_specs,
        out_specs=pl.BlockSpec((ATT_SB, DH_A), lambda t, h: (t, h)),
        out_shape=jax.ShapeDtypeStruct((M_ROWS, A_GROUP_COLS), BF16),
        scratch_shapes=[pltpu.VMEM((ATT_SB, DH_A), F32)] * (2 * N_GROUPS),
        compiler_params=_params("parallel", "arbitrary"),
        name="attn_prompt",
    )(slopes, *([qkv_a] * (5 * N_GROUPS)), ha_sample)


def _attn_sample_body(slopes_ref, q_ref, kn_ref, vn_ref, c0_ref, c1_ref, c2_ref, o_ref):
    L = Q_BLOCK
    c_refs = (c0_ref, c1_ref, c2_ref)
    steps = (L - lax.broadcasted_iota(jnp.int32, (L, 1, 1), 0)).astype(F32)
    outs, lses = [], []
    for g in range(N_GROUPS):
        q = q_ref[g]
        slope = slopes_ref[g][:, 0:1]
        s = (jnp.sum(c_refs[g][:, 0] * q[None], axis=-1, keepdims=True) * (DH_A ** -0.5)
             - (slope * DILATIONS[g])[None] * steps)
        s_new = jnp.sum(q * kn_ref[g], axis=-1, keepdims=True) * (DH_A ** -0.5)
        m = jnp.maximum(jnp.max(s, axis=0), s_new)
        p = jnp.exp(s - m[None])
        p_new = jnp.exp(s_new - m)
        l = jnp.sum(p, axis=0) + p_new
        outs.append((jnp.sum(p * c_refs[g][:, 1], axis=0) + p_new * vn_ref[g]) / l)
        lses.append(m + jnp.log(l))
    o_ref[...] = _softmax_mix(lses, outs)


def _attn_sample(q, kn, vn, caches):
    nb = q.shape[0]
    small = pl.BlockSpec((None, N_GROUPS, H_A, DH_A), lambda b: (b, 0, 0, 0))
    cache_specs, cache_views = [], []
    for g, c in enumerate(caches):
        dil = DILATIONS[g]
        assert c.shape[1] == Q_BLOCK * dil
        cache_views.append(c.reshape(nb, Q_BLOCK, dil, 2, H_A, DH_A))
        cache_specs.append(pl.BlockSpec((None, Q_BLOCK, None, 2, H_A, DH_A), lambda b: (b, 0, 0, 0, 0, 0)))
    slopes = jnp.broadcast_to(jnp.asarray(_alibi_slopes(), F32)[:, :, None], (N_GROUPS, H_A, LANES))
    return pl.pallas_call(
        _attn_sample_body,
        grid=(nb,),
        in_specs=[pl.BlockSpec((N_GROUPS, H_A, LANES), lambda b: (0, 0, 0)), small, small, small] + cache_specs,
        out_specs=pl.BlockSpec((None, H_A, DH_A), lambda b: (b, 0, 0)),
        out_shape=jax.ShapeDtypeStruct((nb, H_A, DH_A), F32),
        compiler_params=_params("parallel"),
        name="attn_sample",
    )(slopes, q, kn, vn, *cache_views)


def _logistic(x):
    return 0.5 * jnp.tanh(0.5 * x) + 0.5


def _sigmoid_epilogue(prods, extras):
    return _logistic(prods[0])


def _gate_epilogue(prods, extras):
    return _logistic(prods[0] + extras[0])


def _plain_epilogue(prods, extras):
    return prods[0]


def _mix_epilogue(prods, extras):
    return extras[0].astype(F32) * prods[0] + extras[1].astype(F32) * prods[1]


def _residual_epilogue(prods, extras):
    return extras[0] + prods[0]


def _padded_sample_rows(sample_rows):
    nb, c = sample_rows.shape
    return jnp.concatenate([sample_rows.astype(BF16), jnp.zeros((SAMPLE_PAD - nb, c), BF16)], axis=0)


def kernel(x_prompt, x_sample, state_mlstm_C, state_mlstm_n, state_mlstm_m, cache_win1_kv, cache_win2_kv, cache_win3_kv, g_ffn1, w_ffn1_in, w_ffn1_out, g_mix, w_in, b_gate, b_igate, b_fgate, w_mlstm_out, w_attn_out, w_out, g_ffn2, w_ffn2_in, w_ffn2_out, g_final):
    d = D_MODEL
    nb = DEC_BATCH
    xp, xs = x_prompt.reshape(N_PROMPT, d), x_sample.reshape(nb, d)

    w_in_nk = w_in[0].T
    gate_bias = jnp.pad(jnp.concatenate([b_igate[0], b_fgate[0]]), (0, LANES - 2 * H_M)).reshape(1, LANES)

    xn = _rmsnorm_inputs(xp, xs, g_ffn1[0])
    y = _swiglu(xn, w_ffn1_in[0], w_ffn1_out[0])
    x1, u = _add_rmsnorm_inputs(xp, xs, y, g_mix[0], 0.5)

    qkv_m, w_mo = _mm("proj_qkv_m", [u], [(0, w_in_nk, 0, True)], [], COL_O_M, BF16, _plain_epilogue,
                      side=[(w_mlstm_out[0], 512, 512)])
    sig_o, w_ao = _mm("proj_o_m", [u], [(0, w_in_nk, COL_O_M, True)], [], V_M_COLS, BF16, _sigmoid_epilogue,
                      side=[(w_attn_out[0], 256, 512)])
    igfg = _mm("proj_if", [u], [(0, w_in_nk, COL_IF, True)], [], LANES, F32, _plain_epilogue, tn=LANES)
    qkv_a = _mm("proj_qkv_a", [u], [(0, w_in_nk, COL_Q_A, True)], [], 3 * A_COLS, F32, _plain_epilogue)
    gates, w_o = _mm("proj_gates", [u], [(0, w_in_nk, COL_GATE, True)],
                     [(b_gate[0].reshape(1, 2 * d), "row", 0)], 2 * d, BF16, _gate_epilogue,
                     side=[(w_out[0], 512, 512)])

    qkv_s = qkv_m[N_PROMPT:N_PROMPT + nb].astype(F32)
    if_s = igfg[N_PROMPT:N_PROMPT + nb] + gate_bias
    hm_s, c_s, n_s, m_s = _mlstm_sample(
        qkv_s[:, :QK_M_COLS].reshape(nb, H_M, DK_M),
        qkv_s[:, QK_M_COLS:2 * QK_M_COLS].reshape(nb, H_M, DK_M),
        qkv_s[:, 2 * QK_M_COLS:].reshape(nb, H_M, DV_M),
        sig_o[N_PROMPT:N_PROMPT + nb].astype(F32).reshape(nb, H_M, DV_M),
        if_s[:, :H_M], if_s[:, H_M:2 * H_M],
        state_mlstm_C[0], state_mlstm_n[0], state_mlstm_m[0])
    hm, c_p, n_p, m_p = _mlstm_prompt(qkv_m, sig_o, igfg, gate_bias,
                                      _padded_sample_rows(hm_s.reshape(nb, V_M_COLS)))

    qkv_as = qkv_a[N_PROMPT:N_PROMPT + nb].reshape(nb, 3, N_GROUPS, H_A, DH_A)
    ha_s = _attn_sample(qkv_as[:, 0], qkv_as[:, 1], qkv_as[:, 2],
                        (cache_win1_kv[0], cache_win2_kv[0], cache_win3_kv[0]))
    ha = _attn_prompt(qkv_a, _padded_sample_rows(ha_s.reshape(nb, A_GROUP_COLS)))

    z = _mm("mix", [hm, ha], [(0, w_mo, 0, False), (1, w_ao, 0, False)],
            [(gates, "tile", 0), (gates, "tile", d)], d, BF16, _mix_epilogue)
    x2 = _mm("proj_out", [z], [(0, w_o, 0, False)], [(x1, "tile", 0)], d, F32, _residual_epilogue)

    xn2 = _rmsnorm(x2, g_ffn2[0])
    y2 = _swiglu(xn2, w_ffn2_in[0], w_ffn2_out[0])
    out_p, out_s = _add_final_norm(x2, y2, g_final, 0.5)

    win_p, win_s = [], []
    for g, w in enumerate(WINDOWS):
        n_last = min(w, SEQ)
        k0, v0 = A_COLS + g * A_GROUP_COLS, 2 * A_COLS + g * A_GROUP_COLS
        per_seq = []
        for b in range(BATCH):
            last = qkv_a[(b + 1) * SEQ - n_last:(b + 1) * SEQ]
            per_seq.append(jnp.stack([last[:, k0:k0 + A_GROUP_COLS], last[:, v0:v0 + A_GROUP_COLS]], axis=1))
        win_p.append(jnp.stack(per_seq).reshape(1, BATCH, n_last, 2, H_A, DH_A))
        win_s.append(qkv_as[:, 1:3, g].reshape(1, nb, 1, 2, H_A, DH_A))
    return (out_p.reshape(BATCH, SEQ, d), out_s[:nb].reshape(nb, 1, d),
            c_p[None], n_p[None], m_p[None],
            c_s[None], n_s[None], m_s[:, :, 0][None],
            win_p[0], win_p[1], win_p[2], win_s[0], win_s[1], win_s[2])
```
